```python
import math
import jax, jax.numpy as jnp
from jax import lax
import numpy as np


D_MODEL = 4096
BATCH = 1
SEQ = 16384
DEPTH = 4

GRID_W = 64
CTX_LEN = 256
HEAD_DIM = 128
BRANCH_W = 1024
DIFF_HEADS = 4
NA_HEADS = 8
CONV_K = 3
N_BRANCH = 3
NA_WIN_R = 8
NA_WIN_C = 16
N_EXPERTS = 32
TOP_K = 4
D_EXPERT = 256
SWIGLU_LIMIT = 7.0
SWIGLU_ALPHA = 1.702
R_MOD = 256
N_MOD = 6
ROPE_BASE = 10000.0
Q_BLOCK = 128
EPS = 1e-6
IN_WIDTH = 9 * BRANCH_W + N_BRANCH * D_MODEL

kernel_name = 'hybrid_diffattn_natten_shortconv_moe_dit'


def rms_norm(x, g):
    x32 = x.astype(jnp.float32)
    y = x32 * lax.rsqrt(jnp.mean(x32 * x32, axis=-1, keepdims=True) + EPS)
    return y.astype(x.dtype) * g


def softmax32(s):
    return jax.nn.softmax(s.astype(jnp.float32), axis=-1)


def modulation(cond, w_a, w_b, b):
    m = (jax.nn.silu(cond) @ w_a) @ w_b + b
    m = m.reshape(m.shape[:-1] + (N_MOD, D_MODEL))
    return [m[..., j, None, :] for j in range(N_MOD)]


def axial_rope_tables(n):
    t = jnp.arange(n, dtype=jnp.int32)
    row = (t // GRID_W).astype(jnp.float32)
    col = (t % GRID_W).astype(jnp.float32)
    n_freq = HEAD_DIM // 4
    inv = ROPE_BASE ** (-jnp.arange(n_freq, dtype=jnp.float32) / n_freq)
    ar = row[:, None] * inv
    ac = col[:, None] * inv
    ang = jnp.concatenate([ar, ar, ac, ac], axis=-1)
    return jnp.cos(ang), jnp.sin(ang)


def apply_rope(x, cos, sin):
    n, d = cos.shape
    shp = (1, n) + (1,) * (x.ndim - 3) + (d,)
    cos = cos.reshape(shp).astype(x.dtype)
    sin = sin.reshape(shp).astype(x.dtype)
    a1, a2, b1, b2 = jnp.split(x, 4, axis=-1)
    rot = jnp.concatenate([-a2, a1, -b2, b1], axis=-1)
    return x * cos + rot * sin


def split_projection(p):
    sizes = [BRANCH_W] * 9 + [N_BRANCH * D_MODEL]
    return jnp.split(p, np.cumsum(sizes)[:-1].tolist(), axis=-1)


def mixer_inputs(x, shift, scale, norm_g, w_in, dq_g, dk_g, nq_g, nk_g, rope):
    B, N = x.shape[:2]
    p = (rms_norm(x, norm_g) * (1 + scale) + shift) @ w_in
    dq, dk, dv, nq, nk, nv, cb, cc, cx, gates = split_projection(p)
    dq = rms_norm(dq.reshape(B, N, DIFF_HEADS, 2, HEAD_DIM), dq_g)
    dk = rms_norm(dk.reshape(B, N, DIFF_HEADS, 2, HEAD_DIM), dk_g)
    if rope is not None:
        dq = apply_rope(dq, rope[0], rope[1])
        dk = apply_rope(dk, rope[0], rope[1])
    dv = dv.reshape(B, N, DIFF_HEADS, 2 * HEAD_DIM)
    nq = rms_norm(nq.reshape(B, N, NA_HEADS, HEAD_DIM), nq_g)
    nk = rms_norm(nk.reshape(B, N, NA_HEADS, HEAD_DIM), nk_g)
    nv = nv.reshape(B, N, NA_HEADS, HEAD_DIM)
    return dq, dk, dv, nq, nk, nv, cb, cc, cx, gates


def diff_core(q, k, v, lam, sub_g, lam_init):
    s = jnp.einsum('bqhmd,bkhmd->bhmqk', q, k) * (HEAD_DIM ** -0.5)
    a = softmax32(s)
    p = a[:, :, 0] - lam * a[:, :, 1]
    o = jnp.einsum('bhqk,bkhe->bqhe', p.astype(v.dtype), v)
    return rms_norm(o, sub_g) * (1.0 - lam_init)


def diff_attention_latent(q, k_all, v_all, lam, sub_g, lam_init):
    B, S = q.shape[:2]
    nb = S // Q_BLOCK
    qb = jnp.moveaxis(q.reshape(B, nb, Q_BLOCK, DIFF_HEADS, 2, HEAD_DIM), 1, 0)
    o = lax.map(lambda qblk: diff_core(qblk, k_all, v_all, lam, sub_g, lam_init), qb)
    return jnp.moveaxis(o, 0, 1).reshape(B, S, BRANCH_W)


def neighborhood_attention(q, k, v, kc, vc, rpb):
    B, S, H, d = q.shape
    rows = S // GRID_W
    wr = min(NA_WIN_R, rows)
    scale = HEAD_DIM ** -0.5
    kg = k.reshape(B, rows, GRID_W, H, d)
    vg = v.reshape(B, rows, GRID_W, H, d)
    q_rows = jnp.moveaxis(q.reshape(B, rows, GRID_W, H, d), 1, 0)
    row_start = jnp.clip(jnp.arange(rows) - NA_WIN_R // 2, 0, rows - wr)
    col_start = jnp.clip(jnp.arange(GRID_W) - NA_WIN_C // 2, 0, GRID_W - NA_WIN_C)
    col_idx = col_start[:, None] + jnp.arange(NA_WIN_C)
    rel_c = col_idx - jnp.arange(GRID_W)[:, None] + (NA_WIN_C - 1)

    def one_row(args):
        r, q_row = args
        rs = row_start[r]
        k_nb = lax.dynamic_slice_in_dim(kg, rs, wr, axis=1)[:, :, col_idx]
        v_nb = lax.dynamic_slice_in_dim(vg, rs, wr, axis=1)[:, :, col_idx]
        rel_r = rs + jnp.arange(wr) - r + (NA_WIN_R - 1)
        bias = jnp.transpose(rpb[:, rel_r][:, :, rel_c], (2, 0, 1, 3))
        s_loc = jnp.einsum('bwhd,brwchd->bwhrc', q_row, k_nb) * scale
        s_loc = s_loc.astype(jnp.float32) + bias[None].astype(jnp.float32)
        s_ctx = jnp.einsum('bwhd,blhd->bwhl', q_row, kc) * scale
        n_loc = wr * NA_WIN_C
        p = softmax32(jnp.concatenate([s_loc.reshape(B, GRID_W, H, n_loc), s_ctx.astype(jnp.float32)], axis=-1))
        p = p.astype(v.dtype)
        p_loc = p[..., :n_loc].reshape(B, GRID_W, H, wr, NA_WIN_C)
        return (jnp.einsum('bwhrc,brwchd->bwhd', p_loc, v_nb)
                + jnp.einsum('bwhl,blhd->bwhd', p[..., n_loc:], vc))

    o = lax.map(one_row, (jnp.arange(rows), q_rows))
    return jnp.moveaxis(o, 0, 1).reshape(B, S, H * d)


def dense_attention(q, k, v):
    s = jnp.einsum('bqhd,bkhd->bhqk', q, k) * (HEAD_DIM ** -0.5)
    return jnp.einsum('bhqk,bkhd->bqhd', softmax32(s).astype(v.dtype), v)


def depthwise_conv(u, w):
    return lax.conv_general_dilated(u, w[:, None, :], window_strides=(1,),
                                    padding=((CONV_K // 2, CONV_K // 2),),
                                    dimension_numbers=('NWC', 'WIO', 'NWC'),
                                    feature_group_count=u.shape[-1])


def branch_merge(o_diff, o_na, cb, cc, cx, gates, conv_w, w_up_diff, w_up_na, w_up_conv, w_o):
    B, N = gates.shape[:2]
    o_conv = cb * depthwise_conv(cc * cx, conv_w)
    g = jax.nn.sigmoid(gates).reshape(B, N, N_BRANCH, D_MODEL)
    merged = (g[:, :, 0] * (o_diff @ w_up_diff)
              + g[:, :, 1] * (o_na @ w_up_na)
              + g[:, :, 2] * (o_conv @ w_up_conv))
    return merged @ w_o


def moe(h, router_w, router_b, w_gu, b_gu, w_down, b_down):
    logits = (h @ router_w + router_b).astype(jnp.float32)
    top_v, top_i = lax.top_k(logits, TOP_K)
    wts = jax.nn.softmax(top_v, axis=-1)
    combine = jnp.sum(jax.nn.one_hot(top_i, N_EXPERTS, dtype=jnp.float32) * wts[..., None], axis=-2)
    combine = combine.astype(h.dtype)
    out = jnp.zeros_like(h)
    for e in range(N_EXPERTS):
        gu = h @ w_gu[e] + b_gu[e]
        gate = jnp.minimum(gu[..., :D_EXPERT], SWIGLU_LIMIT)
        lin = jnp.clip(gu[..., D_EXPERT:], -SWIGLU_LIMIT, SWIGLU_LIMIT)
        act = gate * jax.nn.sigmoid(SWIGLU_ALPHA * gate) * (lin + 1)
        out = out + combine[..., e:e + 1] * (act @ w_down[e] + b_down[e])
    return out


def setup_inputs(seed: int = 0) -> dict:
    key = jax.random.key(seed)
    ks = jax.random.split(key, 32)
    L = DEPTH

    def nrm(k, shape, s):
        return jax.random.normal(k, shape, jnp.float32) * s

    def gain(k, shape):
        return 1.0 + nrm(k, shape, 0.02)

    return {
        'x': nrm(ks[0], (BATCH, SEQ, D_MODEL), 1.0),
        'c': nrm(ks[1], (BATCH, D_MODEL), 1.0),
        'ctx': nrm(ks[2], (BATCH, CTX_LEN, D_MODEL), 1.0),
        'c_ctx': nrm(ks[3], (D_MODEL,), 1.0),
        'w_mod_a': nrm(ks[4], (L, D_MODEL, R_MOD), D_MODEL ** -0.5),
        'w_mod_b': nrm(ks[5], (L, R_MOD, N_MOD * D_MODEL), 0.3 * R_MOD ** -0.5),
        'b_mod': nrm(ks[6], (L, N_MOD * D_MODEL), 0.02),
        'norm1_g': gain(ks[7], (L, D_MODEL)),
        'w_in': nrm(ks[8], (L, D_MODEL, IN_WIDTH), D_MODEL ** -0.5),
        'diff_qn_g': gain(ks[9], (L, HEAD_DIM)),
        'diff_kn_g': gain(ks[10], (L, HEAD_DIM)),
        'lambda_q1': nrm(ks[11], (L, HEAD_DIM), 0.1),
        'lambda_k1': nrm(ks[12], (L, HEAD_DIM), 0.1),
        'lambda_q2': nrm(ks[13], (L, HEAD_DIM), 0.1),
        'lambda_k2': nrm(ks[14], (L, HEAD_DIM), 0.1),
        'diff_subln_g': gain(ks[15], (L, 2 * HEAD_DIM)),
        'na_qn_g': gain(ks[16], (L, HEAD_DIM)),
        'na_kn_g': gain(ks[17], (L, HEAD_DIM)),
        'na_rpb': nrm(ks[18], (L, NA_HEADS, 2 * NA_WIN_R - 1, 2 * NA_WIN_C - 1), 0.02),
        'conv_w': nrm(ks[19], (L, CONV_K, BRANCH_W), CONV_K ** -0.5),
        'w_up_diff': nrm(ks[20], (L, BRANCH_W, D_MODEL), BRANCH_W ** -0.5),
        'w_up_na': nrm(ks[21], (L, BRANCH_W, D_MODEL), BRANCH_W ** -0.5),
        'w_up_conv': nrm(ks[22], (L, BRANCH_W, D_MODEL), BRANCH_W ** -0.5),
        'w_o': nrm(ks[23], (L, D_MODEL, D_MODEL), D_MODEL ** -0.5),
        'norm2_g': gain(ks[24], (L, D_MODEL)),
        'router_w': nrm(ks[25], (L, D_MODEL, N_EXPERTS), D_MODEL ** -0.5),
        'router_b': nrm(ks[26], (L, N_EXPERTS), 0.01),
        'w_gu': nrm(ks[27], (L, N_EXPERTS, D_MODEL, 2 * D_EXPERT), D_MODEL ** -0.5),
        'b_gu': nrm(ks[28], (L, N_EXPERTS, 2 * D_EXPERT), 0.02),
        'w_down': nrm(ks[29], (L, N_EXPERTS, D_EXPERT, D_MODEL), D_EXPERT ** -0.5),
        'b_down': nrm(ks[30], (L, N_EXPERTS, D_MODEL), 0.02),
    }


def reference(x, c, ctx, c_ctx, w_mod_a, w_mod_b, b_mod, norm1_g, w_in, diff_qn_g, diff_kn_g,
              lambda_q1, lambda_k1, lambda_q2, lambda_k2, diff_subln_g, na_qn_g, na_kn_g, na_rpb,
              conv_w, w_up_diff, w_up_na, w_up_conv, w_o, norm2_g, router_w, router_b,
              w_gu, b_gu, w_down, b_down):
    B, S = x.shape[:2]
    Lc = ctx.shape[1]
    rope = axial_rope_tables(S)
    xl, xc = x, ctx
    for i in range(DEPTH):
        last = i == DEPTH - 1
        lam_init = 0.8 - 0.6 * math.exp(-0.3 * i)
        lam = (jnp.exp(jnp.sum(lambda_q1[i].astype(jnp.float32) * lambda_k1[i].astype(jnp.float32)))
               - jnp.exp(jnp.sum(lambda_q2[i].astype(jnp.float32) * lambda_k2[i].astype(jnp.float32)))
               + lam_init)
        ml = modulation(c, w_mod_a[i], w_mod_b[i], b_mod[i])
        mc = modulation(c_ctx, w_mod_a[i], w_mod_b[i], b_mod[i])

        dq_l, dk_l, dv_l, nq_l, nk_l, nv_l, cb_l, cc_l, cx_l, g_l = mixer_inputs(
            xl, ml[0], ml[1], norm1_g[i], w_in[i], diff_qn_g[i], diff_kn_g[i], na_qn_g[i], na_kn_g[i], rope)
        dq_c, dk_c, dv_c, nq_c, nk_c, nv_c, cb_c, cc_c, cx_c, g_c = mixer_inputs(
            xc, mc[0], mc[1], norm1_g[i], w_in[i], diff_qn_g[i], diff_kn_g[i], na_qn_g[i], na_kn_g[i], None)

        o_diff_l = diff_attention_latent(dq_l, jnp.concatenate([dk_l, dk_c], axis=1),
                                         jnp.concatenate([dv_l, dv_c], axis=1),
                                         lam, diff_subln_g[i], lam_init)
        o_na_l = neighborhood_attention(nq_l, nk_l, nv_l, nk_c, nv_c, na_rpb[i])
        xl = xl + ml[2] * branch_merge(o_diff_l, o_na_l, cb_l, cc_l, cx_l, g_l, conv_w[i],
                                       w_up_diff[i], w_up_na[i], w_up_conv[i], w_o[i])
        if not last:
            o_diff_c = diff_core(dq_c, dk_c, dv_c, lam, diff_subln_g[i], lam_init).reshape(B, Lc, BRANCH_W)
            o_na_c = dense_attention(nq_c, nk_c, nv_c).reshape(B, Lc, BRANCH_W)
            xc = xc + mc[2] * branch_merge(o_diff_c, o_na_c, cb_c, cc_c, cx_c, g_c, conv_w[i],
                                           w_up_diff[i], w_up_na[i], w_up_conv[i], w_o[i])

        h_l = rms_norm(xl, norm2_g[i]) * (1 + ml[4]) + ml[3]
        xl = xl + ml[5] * moe(h_l, router_w[i], router_b[i], w_gu[i], b_gu[i], w_down[i], b_down[i])
        if not last:
            h_c = rms_norm(xc, norm2_g[i]) * (1 + mc[4]) + mc[3]
            xc = xc + mc[5] * moe(h_c, router_w[i], router_b[i], w_gu[i], b_gu[i], w_down[i], b_down[i])
    return xl
```

```python
import functools
import math

import jax
import jax.numpy as jnp
from jax import lax
from jax.experimental import pallas as pl
from jax.experimental.pallas import tpu as pltpu

F32 = jnp.float32
BF16 = jnp.bfloat16
HIGHEST = lax.Precision.HIGHEST

GRID_W = 64
HEAD_DIM = 128
NA_WIN_R = 8
NA_WIN_C = 16
CONV_K = 3
N_BRANCH = 3
N_MOD = 6
TOP_K = 4
SWIGLU_LIMIT = 7.0
SWIGLU_ALPHA = 1.702
ROPE_BASE = 10000.0
EPS = 1e-6
LOG2E = math.log2(math.e)
Q_SCALE = HEAD_DIM ** -0.5 * LOG2E
MASKED = -1e30

V7X_VMEM_LIMIT_BYTES = 56 * 1024 * 1024
NA_Q_ROWS = 8
NA_K_ROWS = NA_Q_ROWS + NA_WIN_R


def _cparams(*sem):
    return pltpu.CompilerParams(dimension_semantics=sem, vmem_limit_bytes=V7X_VMEM_LIMIT_BYTES)


def _tile(n, target):
    if n <= target:
        return n
    t = target
    while n % t:
        t -= 8
    assert t > 0, (n, target)
    return t


def _sigmoid(x):
    return 1.0 / (1.0 + jnp.exp(-x))


def _mod_kernel(c_ref, wa_ref, wb_ref, b_ref, o_ref):
    c = c_ref[...]
    s = c * _sigmoid(c)
    m1 = jnp.dot(s, wa_ref[...], precision=HIGHEST, preferred_element_type=F32)
    o_ref[...] = jnp.dot(m1, wb_ref[...], precision=HIGHEST, preferred_element_type=F32) + b_ref[...]


def modulation(cond, w_a, w_b, b):
    depth, d, r = w_a.shape
    width = w_b.shape[-1]
    rows = cond.shape[0]
    tn = _tile(width, 4096)
    return pl.pallas_call(
        _mod_kernel,
        grid=(depth, width // tn),
        in_specs=[
            pl.BlockSpec((rows, d), lambda l, j: (0, 0)),
            pl.BlockSpec((None, d, r), lambda l, j: (l, 0, 0)),
            pl.BlockSpec((None, r, tn), lambda l, j: (l, 0, j)),
            pl.BlockSpec((None, 1, tn), lambda l, j: (l, 0, j)),
        ],
        out_specs=pl.BlockSpec((None, rows, tn), lambda l, j: (l, 0, j)),
        out_shape=jax.ShapeDtypeStruct((depth, rows, width), F32),
        compiler_params=_cparams("arbitrary", "arbitrary"),
        name="modulation",
    )(cond, w_a, w_b, b.reshape(depth, 1, width))


def _modulated_norm(x_ref, g_ref, sc_ref, sh_ref):
    x = x_ref[...]
    y = x * lax.rsqrt(jnp.mean(x * x, axis=-1, keepdims=True) + EPS)
    return (y * g_ref[...]) * (1.0 + sc_ref[...]) + sh_ref[...]


def _prenorm_kernel(x_ref, g_ref, sc_ref, sh_ref, o_ref):
    o_ref[...] = _modulated_norm(x_ref, g_ref, sc_ref, sh_ref).astype(o_ref.dtype)


def _prenorm_router_kernel(x_ref, g_ref, sc_ref, sh_ref, rw_ref, rb_ref, o_ref, comb_ref):
    h = _modulated_norm(x_ref, g_ref, sc_ref, sh_ref)
    o_ref[...] = h.astype(o_ref.dtype)
    logits = jnp.dot(h, rw_ref[...], precision=HIGHEST, preferred_element_type=F32) + rb_ref[...]
    n_exp = logits.shape[-1]
    lane = lax.broadcasted_iota(jnp.int32, logits.shape, 1).astype(F32)
    sels, vals = [], []
    for _ in range(TOP_K):
        m = jnp.max(logits, axis=-1, keepdims=True)
        first = jnp.min(jnp.where(logits == m, lane, float(n_exp)), axis=-1, keepdims=True)
        sel = lane == first
        sels.append(sel)
        vals.append(m)
        logits = jnp.where(sel, -jnp.inf, logits)
    es = [jnp.exp(v - vals[0]) for v in vals]
    inv = 1.0 / functools.reduce(lambda a, b: a + b, es)
    comb = jnp.zeros(logits.shape, F32)
    for sel, e in zip(sels, es):
        comb = comb + jnp.where(sel, e * inv, 0.0)
    comb_ref[...] = comb


def prenorm(x, g, scale, shift, router=None):
    m, d = x.shape
    tm = _tile(m, 256)
    row = lambda i: (i, 0)
    fixed = lambda i: (0, 0)
    vec = pl.BlockSpec((1, d), fixed)
    in_specs = [pl.BlockSpec((tm, d), row), vec, vec, vec]
    args = [x, g.reshape(1, d), scale.reshape(1, d), shift.reshape(1, d)]
    if router is None:
        return pl.pallas_call(
            _prenorm_kernel, grid=(m // tm,), in_specs=in_specs,
            out_specs=pl.BlockSpec((tm, d), row),
            out_shape=jax.ShapeDtypeStruct((m, d), BF16),
            compiler_params=_cparams("arbitrary"), name="prenorm",
        )(*args)
    rw, rb = router
    n_exp = rw.shape[-1]
    in_specs += [pl.BlockSpec((d, n_exp), fixed), pl.BlockSpec((1, n_exp), fixed)]
    args += [rw, rb.reshape(1, n_exp)]
    return pl.pallas_call(
        _prenorm_router_kernel, grid=(m // tm,), in_specs=in_specs,
        out_specs=[pl.BlockSpec((tm, d), row), pl.BlockSpec((tm, n_exp), row)],
        out_shape=[jax.ShapeDtypeStruct((m, d), BF16), jax.ShapeDtypeStruct((m, n_exp), F32)],
        compiler_params=_cparams("arbitrary"), name="prenorm_router",
    )(*args)


def _mm_kernel(*refs, epilogue):
    a_ref, w_ref = refs[0], refs[1]
    extras, o_ref = refs[2:-1], refs[-1]
    acc = jnp.dot(a_ref[...], w_ref[...], preferred_element_type=F32)
    o_ref[...] = epilogue(acc, *extras).astype(o_ref.dtype)


def matmul(a, w, w_spec, tm, tn, n_col_tiles, extras, epilogue, out_spec, out_shape, name):
    m, k = a.shape
    return pl.pallas_call(
        functools.partial(_mm_kernel, epilogue=epilogue),
        grid=(m // tm, n_col_tiles),
        in_specs=[pl.BlockSpec((tm, k), lambda i, j: (i, 0)), w_spec] + [s for _, s in extras],
        out_specs=out_spec,
        out_shape=out_shape,
        compiler_params=_cparams("arbitrary", "arbitrary"),
        name=name,
    )(a, w, *[x for x, _ in extras])


def _cast_epilogue(acc):
    return acc


def _residual_epilogue(acc, x_ref, gate_ref):
    return x_ref[...] + gate_ref[...] * acc


def _swiglu_epilogue(acc, b_ref, comb_ref):
    d_exp = acc.shape[-1] // 2
    gu = acc + b_ref[...]
    gate = jnp.minimum(gu[:, :d_exp], SWIGLU_LIMIT)
    lin = jnp.clip(gu[:, d_exp:], -SWIGLU_LIMIT, SWIGLU_LIMIT)
    act = gate * _sigmoid(SWIGLU_ALPHA * gate) * (lin + 1.0)
    comb = comb_ref[...]
    lane = lax.broadcasted_iota(jnp.int32, comb.shape, 1)
    cw = jnp.sum(jnp.where(lane == pl.program_id(1), comb, 0.0), axis=-1, keepdims=True)
    return act * cw


def _moe_down_epilogue(acc, comb_ref, bd_ref, x_ref, gate_ref):
    bias = jnp.dot(comb_ref[...], bd_ref[...], precision=HIGHEST, preferred_element_type=F32)
    return x_ref[...] + gate_ref[...] * (acc + bias)


def project_in(h, w_in):
    m, d = h.shape
    width = w_in.shape[-1]
    tm, tn = _tile(m, 1024), _tile(width, 1024)
    return matmul(h, w_in, pl.BlockSpec((d, tn), lambda i, j: (0, j)), tm, tn, width // tn, [],
                  _cast_epilogue, pl.BlockSpec((tm, tn), lambda i, j: (i, j)),
                  jax.ShapeDtypeStruct((m, width), BF16), "project_in")


def project_out(merged, w_o, x, gate):
    m, d = x.shape
    tm, tn = _tile(m, 1024), _tile(d, 512)
    tile = pl.BlockSpec((tm, tn), lambda i, j: (i, j))
    return matmul(merged, w_o, pl.BlockSpec((merged.shape[1], tn), lambda i, j: (0, j)), tm, tn, d // tn,
                  [(x, tile), (gate.reshape(1, d), pl.BlockSpec((1, tn), lambda i, j: (0, j)))],
                  _residual_epilogue, tile, jax.ShapeDtypeStruct((m, d), F32), "project_out")


def moe(h, comb, w_gu, b_gu, w_down, b_down, x, gate):
    m, d = h.shape
    n_exp, _, two_de = w_gu.shape
    d_exp = two_de // 2
    tm = _tile(m, 1024)
    act = matmul(
        h, w_gu, pl.BlockSpec((None, d, two_de), lambda i, j: (j, 0, 0)), tm, two_de, n_exp,
        [(b_gu.reshape(n_exp, 1, two_de), pl.BlockSpec((None, 1, two_de), lambda i, j: (j, 0, 0))),
         (comb, pl.BlockSpec((tm, n_exp), lambda i, j: (i, 0)))],
        _swiglu_epilogue, pl.BlockSpec((tm, d_exp), lambda i, j: (i, j)),
        jax.ShapeDtypeStruct((m, n_exp * d_exp), BF16), "moe_up")
    tm, tn = _tile(m, 512), _tile(d, 512)
    tile = pl.BlockSpec((tm, tn), lambda i, j: (i, j))
    return matmul(
        act, w_down.reshape(n_exp * d_exp, d), pl.BlockSpec((n_exp * d_exp, tn), lambda i, j: (0, j)),
        tm, tn, d // tn,
        [(comb, pl.BlockSpec((tm, n_exp), lambda i, j: (i, 0))),
         (b_down, pl.BlockSpec((n_exp, tn), lambda i, j: (0, j))),
         (x, tile), (gate.reshape(1, d), pl.BlockSpec((1, tn), lambda i, j: (0, j)))],
        _moe_down_epilogue, tile, jax.ShapeDtypeStruct((m, d), F32), "moe_down")


def _chunk_norm(x, g):
    return x * lax.rsqrt(jnp.mean(x * x, axis=-1, keepdims=True) + EPS) * g


def _qk_prep_kernel(*refs, rope):
    if rope:
        dq_ref, dk_ref, nq_ref, nk_ref, g_ref, cos_ref, sin_ref = refs[:7]
        cos, sin = cos_ref[...], sin_ref[...]
        lane = lax.broadcasted_iota(jnp.int32, cos.shape, 1)
        lower = (lane & (HEAD_DIM // 4)) == 0
    else:
        dq_ref, dk_ref, nq_ref, nk_ref, g_ref = refs[:5]
    outs = refs[-4:]
    ins = (dq_ref, dk_ref, nq_ref, nk_ref)
    for idx in range(4):
        g = g_ref[idx:idx + 1, :]
        rot = rope and idx < 2
        scale = Q_SCALE if idx in (0, 2) else 1.0
        for c in range(ins[idx].shape[-1] // HEAD_DIM):
            sl = slice(c * HEAD_DIM, (c + 1) * HEAD_DIM)
            y = _chunk_norm(ins[idx][:, sl].astype(F32), g)
            if rot:
                swapped = jnp.where(lower, pltpu.roll(y, HEAD_DIM - HEAD_DIM // 4, axis=1),
                                    pltpu.roll(y, HEAD_DIM // 4, axis=1))
                y = y * cos + swapped * sin
            outs[idx][:, sl] = (y * scale).astype(BF16)


def qk_prep(p, gains, rope, bw):
    m = p.shape[0]
    tm = _tile(m, 512)
    col = lambda c: pl.BlockSpec((tm, bw), lambda i, c=c: (i, c))
    in_specs = [col(0), col(1), col(3), col(4), pl.BlockSpec((4, HEAD_DIM), lambda i: (0, 0))]
    args = [p, p, p, p, gains]
    if rope is not None:
        in_specs += [pl.BlockSpec((tm, HEAD_DIM), lambda i: (i, 0))] * 2
        args += list(rope)
    return pl.pallas_call(
        functools.partial(_qk_prep_kernel, rope=rope is not None),
        grid=(m // tm,), in_specs=in_specs,
        out_specs=[pl.BlockSpec((tm, bw), lambda i: (i, 0))] * 4,
        out_shape=[jax.ShapeDtypeStruct((m, bw), BF16)] * 4,
        compiler_params=_cparams("arbitrary"), name="qk_prep",
    )(*args)


def rope_tables(n):
    t = jnp.arange(n, dtype=jnp.int32)
    row = (t // GRID_W).astype(F32)
    col = (t % GRID_W).astype(F32)
    n_freq = HEAD_DIM // 4
    inv = ROPE_BASE ** (-jnp.arange(n_freq, dtype=F32) / n_freq)
    ar, ac = row[:, None] * inv, col[:, None] * inv
    ang = jnp.concatenate([ar, ar, ac, ac], axis=-1)
    lane = jnp.arange(HEAD_DIM)
    sign = jnp.where((lane & n_freq) == 0, -1.0, 1.0).astype(F32)
    return jnp.cos(ang), jnp.sin(ang) * sign


def _diff_attn_kernel(lam_ref, q_ref, k_ref, v_ref, *rest, n_main, has_extra, out_scale):
    if has_extra:
        kx_ref, vx_ref, g_ref, o_ref, m_sc, l_sc, acc_sc = rest
    else:
        g_ref, o_ref, m_sc, l_sc, acc_sc = rest
    kj = pl.program_id(2)
    n_steps = n_main + (1 if has_extra else 0)

    @pl.when(kj == 0)
    def _():
        m_sc[...] = jnp.full(m_sc.shape, -jnp.inf, F32)
        l_sc[...] = jnp.zeros(l_sc.shape, F32)
        acc_sc[...] = jnp.zeros(acc_sc.shape, F32)

    def update(kr, vr):
        v = vr[...]
        for hf in range(2):
            sl = slice(hf * HEAD_DIM, (hf + 1) * HEAD_DIM)
            s = lax.dot_general(q_ref[:, sl], kr[:, sl], (((1,), (1,)), ((), ())),
                                preferred_element_type=F32)
            m_prev = m_sc[hf]
            m_new = jnp.maximum(m_prev, jnp.max(s, axis=-1, keepdims=True))
            alpha = jnp.exp2(m_prev - m_new)
            p = jnp.exp2(s - m_new)
            l_sc[hf] = alpha * l_sc[hf] + jnp.sum(p, axis=-1, keepdims=True)
            acc_sc[hf] = alpha * acc_sc[hf] + jnp.dot(p.astype(BF16), v, preferred_element_type=F32)
            m_sc[hf] = m_new

    if has_extra:
        pl.when(kj < n_main)(lambda: update(k_ref, v_ref))
        pl.when(kj == n_main)(lambda: update(kx_ref, vx_ref))
    else:
        update(k_ref, v_ref)

    @pl.when(kj == n_steps - 1)
    def _():
        lam = lam_ref[0, 0]
        o = acc_sc[0] * (1.0 / l_sc[0]) - lam * (acc_sc[1] * (1.0 / l_sc[1]))
        y = o * lax.rsqrt(jnp.mean(o * o, axis=-1, keepdims=True) + EPS)
        o_ref[...] = (y * g_ref[...] * out_scale).astype(o_ref.dtype)


def diff_attention(q, k, v_src, v_col0, lam, sub_g, out_scale, extra=None):
    mq, bw = q.shape
    mk = k.shape[0]
    hw = 2 * HEAD_DIM
    heads = bw // hw
    tq, tk = _tile(mq, 1024), _tile(mk, 1024)
    n_main = mk // tk
    last = n_main - 1
    kmap = lambda h, i, j: (jnp.minimum(j, last), h)
    in_specs = [
        pl.BlockSpec(memory_space=pltpu.SMEM),
        pl.BlockSpec((tq, hw), lambda h, i, j: (i, h)),
        pl.BlockSpec((tk, hw), kmap),
        pl.BlockSpec((tk, hw), lambda h, i, j: (jnp.minimum(j, last), v_col0 + h)),
    ]
    args = [lam.reshape(1, 1).astype(F32), q, k, v_src]
    if extra is not None:
        kx, vx_src, vx_col0 = extra
        mx = kx.shape[0]
        in_specs += [pl.BlockSpec((mx, hw), lambda h, i, j: (0, h)),
                     pl.BlockSpec((mx, hw), lambda h, i, j: (0, vx_col0 + h))]
        args += [kx, vx_src]
    in_specs.append(pl.BlockSpec((1, hw), lambda h, i, j: (0, 0)))
    args.append(sub_g.reshape(1, hw))
    return pl.pallas_call(
        functools.partial(_diff_attn_kernel, n_main=n_main, has_extra=extra is not None, out_scale=out_scale),
        grid=(heads, mq // tq, n_main + (extra is not None)),
        in_specs=in_specs,
        out_specs=pl.BlockSpec((tq, hw), lambda h, i, j: (i, h)),
        out_shape=jax.ShapeDtypeStruct((mq, bw), BF16),
        scratch_shapes=[pltpu.VMEM((2, tq, 1), F32), pltpu.VMEM((2, tq, 1), F32), pltpu.VMEM((2, tq, hw), F32)],
        compiler_params=_cparams("arbitrary", "arbitrary", "arbitrary"),
        name="diff_attention",
    )(*args)


def _na_kernel(q_ref, k_ref, v_ref, kc_ref, vc_ref, bias_ref, o_ref, *, rows):
    b = pl.program_id(1)
    start_row = jnp.clip(b * NA_Q_ROWS - NA_WIN_R // 2, 0, rows - NA_K_ROWS)
    start = pl.multiple_of(start_row * GRID_W, GRID_W * (NA_WIN_R // 2))
    n_keys = NA_K_ROWS * GRID_W
    q = q_ref[...]
    k = k_ref[pl.ds(start, n_keys), :]
    v = v_ref[pl.ds(start, n_keys), :]
    nt = (((1,), (1,)), ((), ()))
    s_loc = lax.dot_general(q, k, nt, preferred_element_type=F32) + bias_ref[...]
    s_ctx = lax.dot_general(q, kc_ref[...], nt, preferred_element_type=F32)
    m = jnp.maximum(jnp.max(s_loc, axis=-1, keepdims=True), jnp.max(s_ctx, axis=-1, keepdims=True))
    p_loc = jnp.exp2(s_loc - m)
    p_ctx = jnp.exp2(s_ctx - m)
    denom = jnp.sum(p_loc, axis=-1, keepdims=True) + jnp.sum(p_ctx, axis=-1, keepdims=True)
    o = (jnp.dot(p_loc.astype(BF16), v, preferred_element_type=F32)
         + jnp.dot(p_ctx.astype(BF16), vc_ref[...], preferred_element_type=F32))
    o_ref[...] = (o * (1.0 / denom)).astype(o_ref.dtype)


def na_bias_tables(rpb, rows):
    n_blocks = rows // NA_Q_ROWS
    wr = min(NA_WIN_R, rows)
    w = jnp.arange(GRID_W)
    col_start = jnp.clip(w - NA_WIN_C // 2, 0, GRID_W - NA_WIN_C)
    kc = jnp.arange(GRID_W)
    valid_c = (kc[None, :] >= col_start[:, None]) & (kc[None, :] < col_start[:, None] + NA_WIN_C)
    rel_c = jnp.clip(kc[None, :] - w[:, None] + (NA_WIN_C - 1), 0, 2 * NA_WIN_C - 2)
    tables = []
    for blk in (0, 1, n_blocks - 1):
        r = blk * NA_Q_ROWS + jnp.arange(NA_Q_ROWS)
        start = jnp.clip(blk * NA_Q_ROWS - NA_WIN_R // 2, 0, rows - NA_K_ROWS)
        kr = start + jnp.arange(NA_K_ROWS)
        row_start = jnp.clip(r - NA_WIN_R // 2, 0, rows - wr)
        valid_r = (kr[None, :] >= row_start[:, None]) & (kr[None, :] < row_start[:, None] + wr)
        rel_r = jnp.clip(kr[None, :] - r[:, None] + (NA_WIN_R - 1), 0, 2 * NA_WIN_R - 2)
        bias = rpb[:, rel_r][:, :, :, rel_c]
        valid = valid_r[:, :, None, None] & valid_c[None, None]
        bias = jnp.where(valid[None], bias * LOG2E, MASKED)
        bias = jnp.transpose(bias, (0, 1, 3, 2, 4))
        tables.append(bias.reshape(rpb.shape[0], NA_Q_ROWS * GRID_W, NA_K_ROWS * GRID_W))
    return jnp.stack(tables, axis=1).astype(F32)


def neighbourhood_attention(q, k, v_src, v_col0, kc, vc_src, vc_col0, bias):
    s, bw = q.shape
    heads = bw // HEAD_DIM
    rows = s // GRID_W
    n_blocks = rows // NA_Q_ROWS
    assert n_blocks >= 3 and rows % NA_Q_ROWS == 0
    tq = NA_Q_ROWS * GRID_W
    lc = kc.shape[0]
    variant = lambda b: jnp.where(b == 0, 0, jnp.where(b == n_blocks - 1, 2, 1))
    return pl.pallas_call(
        functools.partial(_na_kernel, rows=rows),
        grid=(heads, n_blocks),
        in_specs=[
            pl.BlockSpec((tq, HEAD_DIM), lambda h, b: (b, h)),
            pl.BlockSpec((s, HEAD_DIM), lambda h, b: (0, h)),
            pl.BlockSpec((s, HEAD_DIM), lambda h, b: (0, v_col0 + h)),
            pl.BlockSpec((lc, HEAD_DIM), lambda h, b: (0, h)),
            pl.BlockSpec((lc, HEAD_DIM), lambda h, b: (0, vc_col0 + h)),
            pl.BlockSpec((None, None, tq, NA_K_ROWS * GRID_W), lambda h, b: (h, variant(b), 0, 0)),
        ],
        out_specs=pl.BlockSpec((tq, HEAD_DIM), lambda h, b: (b, h)),
        out_shape=jax.ShapeDtypeStruct((s, bw), BF16),
        compiler_params=_cparams("arbitrary", "arbitrary"),
        name="neighbourhood_attention",
    )(q, k, v_src, kc, vc_src, bias)


def _dense_attn_kernel(q_ref, k_ref, v_ref, o_ref):
    s = lax.dot_general(q_ref[...], k_ref[...], (((1,), (1,)), ((), ())), preferred_element_type=F32)
    p = jnp.exp2(s - jnp.max(s, axis=-1, keepdims=True))
    o = jnp.dot(p.astype(BF16), v_ref[...], preferred_element_type=F32)
    o_ref[...] = (o * (1.0 / jnp.sum(p, axis=-1, keepdims=True))).astype(o_ref.dtype)


def dense_attention(q, k, v_src, v_col0):
    m, bw = q.shape
    heads = bw // HEAD_DIM
    blk = lambda c0: pl.BlockSpec((m, HEAD_DIM), lambda h: (0, c0 + h))
    return pl.pallas_call(
        _dense_attn_kernel, grid=(heads,), in_specs=[blk(0), blk(0), blk(v_col0)], out_specs=blk(0),
        out_shape=jax.ShapeDtypeStruct((m, bw), BF16),
        compiler_params=_cparams("arbitrary"), name="dense_attention",
    )(q, k, v_src)


def _merge_up_kernel(od_ref, on_ref, cb_ref, cc_ref, cx_ref, ccp_ref, cxp_ref, ccn_ref, cxn_ref, cw_ref,
                     g0_ref, g1_ref, g2_ref, wd_ref, wn_ref, wc_ref, o_ref, oc_sc):
    i, j = pl.program_id(0), pl.program_id(1)

    @pl.when(j == 0)
    def _():
        u = cc_ref[...].astype(F32) * cx_ref[...].astype(F32)
        tm = u.shape[0]
        halo = ccp_ref.shape[0]
        prev_row = ccp_ref[halo - 1:halo, :].astype(F32) * cxp_ref[halo - 1:halo, :].astype(F32)
        next_row = ccn_ref[0:1, :].astype(F32) * cxn_ref[0:1, :].astype(F32)
        prev_row = jnp.where(i == 0, 0.0, prev_row)
        next_row = jnp.where(i == pl.num_programs(0) - 1, 0.0, next_row)
        r = lax.broadcasted_iota(jnp.int32, u.shape, 0)
        u_prev = jnp.where(r == 0, prev_row, pltpu.roll(u, 1, axis=0))
        u_next = jnp.where(r == tm - 1, next_row, pltpu.roll(u, tm - 1, axis=0))
        conv = cw_ref[0:1, :] * u_prev + cw_ref[1:2, :] * u + cw_ref[2:3, :] * u_next
        oc_sc[...] = (cb_ref[...].astype(F32) * conv).astype(BF16)

    acc = _sigmoid(g0_ref[...].astype(F32)) * jnp.dot(od_ref[...], wd_ref[...], preferred_element_type=F32)
    acc += _sigmoid(g1_ref[...].astype(F32)) * jnp.dot(on_ref[...], wn_ref[...], preferred_element_type=F32)
    acc += _sigmoid(g2_ref[...].astype(F32)) * jnp.dot(oc_sc[...], wc_ref[...], preferred_element_type=F32)
    o_ref[...] = acc.astype(o_ref.dtype)


def merge_up(o_diff, o_na, p, conv_w, w_up_diff, w_up_na, w_up_conv, d):
    m, bw = o_diff.shape
    tm, tn = _tile(m, 512), math.gcd(_tile(d, 512), bw)
    halo = 8
    n_halo = m // halo
    row_blk = lambda c: pl.BlockSpec((tm, bw), lambda i, j, c=c: (i, c))
    prev_blk = lambda c: pl.BlockSpec((halo, bw), lambda i, j, c=c: (jnp.maximum(i * (tm // halo) - 1, 0), c))
    next_blk = lambda c: pl.BlockSpec((halo, bw), lambda i, j, c=c: (jnp.minimum((i + 1) * (tm // halo), n_halo - 1), c))
    gate0 = 9 * bw // tn
    gate_blk = lambda b: pl.BlockSpec((tm, tn), lambda i, j, b=b: (i, gate0 + b * (d // tn) + j))
    w_blk = pl.BlockSpec((bw, tn), lambda i, j: (0, j))
    assert (9 * bw) % tn == 0
    return pl.pallas_call(
        _merge_up_kernel,
        grid=(m // tm, d // tn),
        in_specs=[row_blk(0), row_blk(0), row_blk(6), row_blk(7), row_blk(8),
                  prev_blk(7), prev_blk(8), next_blk(7), next_blk(8),
                  pl.BlockSpec((CONV_K, bw), lambda i, j: (0, 0)),
                  gate_blk(0), gate_blk(1), gate_blk(2), w_blk, w_blk, w_blk],
        out_specs=pl.BlockSpec((tm, tn), lambda i, j: (i, j)),
        out_shape=jax.ShapeDtypeStruct((m, d), BF16),
        scratch_shapes=[pltpu.VMEM((tm, bw), BF16)],
        compiler_params=_cparams("arbitrary", "arbitrary"),
        name="merge_up",
    )(o_diff, o_na, p, p, p, p, p, p, p, conv_w, p, p, p, w_up_diff, w_up_na, w_up_conv)


def kernel(x, c, ctx, c_ctx, w_mod_a, w_mod_b, b_mod, norm1_g, w_in, diff_qn_g, diff_kn_g, lambda_q1, lambda_k1, lambda_q2, lambda_k2, diff_subln_g, na_qn_g, na_kn_g, na_rpb, conv_w, w_up_diff, w_up_na, w_up_conv, w_o, norm2_g, router_w, router_b, w_gu, b_gu, w_down, b_down):
    batch, s, d = x.shape
    assert batch == 1 and ctx.shape[0] == 1
    depth = w_in.shape[0]
    bw = w_up_diff.shape[1]
    dv0 = 2 * bw // (2 * HEAD_DIM)
    nv0 = 5 * bw // HEAD_DIM
    rows = s // GRID_W

    cond = jnp.zeros((8, d), F32).at[0].set(c[0]).at[1].set(c_ctx)
    mod = modulation(cond, w_mod_a, w_mod_b, b_mod).reshape(depth, 8, N_MOD, d)
    rope = rope_tables(s)

    xl, xc = x[0], ctx[0]
    for i in range(depth):
        last = i == depth - 1
        lam_init = 0.8 - 0.6 * math.exp(-0.3 * i)
        lam = (jnp.exp(jnp.sum(lambda_q1[i] * lambda_k1[i])) - jnp.exp(jnp.sum(lambda_q2[i] * lambda_k2[i]))
               + lam_init)
        ml, mc = mod[i, 0], mod[i, 1]
        gains = jnp.stack([diff_qn_g[i], diff_kn_g[i], na_qn_g[i], na_kn_g[i]])
        w_in_i = w_in[i].astype(BF16)

        p_l = project_in(prenorm(xl, norm1_g[i], ml[1], ml[0]), w_in_i)
        p_c = project_in(prenorm(xc, norm1_g[i], mc[1], mc[0]), w_in_i)
        dq_l, dk_l, nq_l, nk_l = qk_prep(p_l, gains, rope, bw)
        dq_c, dk_c, nq_c, nk_c = qk_prep(p_c, gains, None, bw)

        o_diff_l = diff_attention(dq_l, dk_l, p_l, dv0, lam, diff_subln_g[i], 1.0 - lam_init,
                                  extra=(dk_c, p_c, dv0))
        o_na_l = neighbourhood_attention(nq_l, nk_l, p_l, nv0, nk_c, p_c, nv0,
                                         na_bias_tables(na_rpb[i], rows))
        ups = (w_up_diff[i].astype(BF16), w_up_na[i].astype(BF16), w_up_conv[i].astype(BF16))
        w_o_i = w_o[i].astype(BF16)
        xl = project_out(merge_up(o_diff_l, o_na_l, p_l, conv_w[i], *ups, d), w_o_i, xl, ml[2])
        if not last:
            o_diff_c = diff_attention(dq_c, dk_c, p_c, dv0, lam, diff_subln_g[i], 1.0 - lam_init)
            o_na_c = dense_attention(nq_c, nk_c, p_c, nv0)
            xc = project_out(merge_up(o_diff_c, o_na_c, p_c, conv_w[i], *ups, d), w_o_i, xc, mc[2])

        w_gu_i, w_down_i = w_gu[i].astype(BF16), w_down[i].astype(BF16)
        router = (router_w[i], router_b[i])
        h_l, comb_l = prenorm(xl, norm2_g[i], ml[4], ml[3], router)
        xl = moe(h_l, comb_l, w_gu_i, b_gu[i], w_down_i, b_down[i], xl, ml[5])
        if not last:
            h_c, comb_c = prenorm(xc, norm2_g[i], mc[4], mc[3], router)
            xc = moe(h_c, comb_c, w_gu_i, b_gu[i], w_down_i, b_down[i], xc, mc[5])
    return xl[None]
```

```python
import functools
import math

import jax
import jax.numpy as jnp
from jax import lax
from jax.experimental import pallas as pl
from jax.experimental.pallas import tpu as pltpu

F32 = jnp.float32
BF16 = jnp.bfloat16
HIGHEST = lax.Precision.HIGHEST

GRID_W = 64
HEAD_DIM = 128
NA_WIN_R = 8
NA_WIN_C = 16
CONV_K = 3
N_BRANCH = 3
N_MOD = 6
TOP_K = 4
SWIGLU_LIMIT = 7.0
SWIGLU_ALPHA = 1.702
ROPE_BASE = 10000.0
EPS = 1e-6
LOG2E = math.log2(math.e)
Q_SCALE = HEAD_DIM ** -0.5 * LOG2E
MASKED = -1e30
SCORE_BOUND_LOG2 = 60.0

V7X_VMEM_LIMIT_BYTES = 56 * 1024 * 1024
NA_Q_ROWS = 8
NA_K_ROWS = NA_Q_ROWS + NA_WIN_R


def _cparams(*sem):
    return pltpu.CompilerParams(dimension_semantics=sem, vmem_limit_bytes=V7X_VMEM_LIMIT_BYTES)


def _tile(n, target):
    if n <= target:
        return n
    t = target
    while n % t:
        t -= 8
    assert t > 0, (n, target)
    return t


def _sigmoid(x):
    return 1.0 / (1.0 + jnp.exp(-x))


def _mod_kernel(c_ref, wa_ref, wb_ref, b_ref, o_ref):
    c = c_ref[...]
    s = c * _sigmoid(c)
    m1 = jnp.dot(s, wa_ref[...], precision=HIGHEST, preferred_element_type=F32)
    o_ref[...] = jnp.dot(m1, wb_ref[...], precision=HIGHEST, preferred_element_type=F32) + b_ref[...]


def modulation(cond, w_a, w_b, b):
    depth, d, r = w_a.shape
    width = w_b.shape[-1]
    rows = cond.shape[0]
    tn = _tile(width, 4096)
    return pl.pallas_call(
        _mod_kernel,
        grid=(depth, width // tn),
        in_specs=[
            pl.BlockSpec((rows, d), lambda l, j: (0, 0)),
            pl.BlockSpec((None, d, r), lambda l, j: (l, 0, 0)),
            pl.BlockSpec((None, r, tn), lambda l, j: (l, 0, j)),
            pl.BlockSpec((None, 1, tn), lambda l, j: (l, 0, j)),
        ],
        out_specs=pl.BlockSpec((None, rows, tn), lambda l, j: (l, 0, j)),
        out_shape=jax.ShapeDtypeStruct((depth, rows, width), F32),
        compiler_params=_cparams("arbitrary", "arbitrary"),
        name="modulation",
    )(cond, w_a, w_b, b.reshape(depth, 1, width))


def _modulated_norm(x_ref, g_ref, sc_ref, sh_ref):
    x = x_ref[...]
    y = x * lax.rsqrt(jnp.mean(x * x, axis=-1, keepdims=True) + EPS)
    return (y * g_ref[...]) * (1.0 + sc_ref[...]) + sh_ref[...]


def _prenorm_kernel(x_ref, g_ref, sc_ref, sh_ref, o_ref):
    o_ref[...] = _modulated_norm(x_ref, g_ref, sc_ref, sh_ref).astype(o_ref.dtype)


def _prenorm_router_kernel(x_ref, g_ref, sc_ref, sh_ref, rw_ref, rb_ref, o_ref, comb_ref):
    h = _modulated_norm(x_ref, g_ref, sc_ref, sh_ref)
    o_ref[...] = h.astype(o_ref.dtype)
    logits = jnp.dot(h, rw_ref[...], precision=HIGHEST, preferred_element_type=F32) + rb_ref[...]
    n_exp = logits.shape[-1]
    lane = lax.broadcasted_iota(jnp.int32, logits.shape, 1).astype(F32)
    sels, vals = [], []
    for _ in range(TOP_K):
        m = jnp.max(logits, axis=-1, keepdims=True)
        first = jnp.min(jnp.where(logits == m, lane, float(n_exp)), axis=-1, keepdims=True)
        sel = lane == first
        sels.append(sel)
        vals.append(m)
        logits = jnp.where(sel, -jnp.inf, logits)
    es = [jnp.exp(v - vals[0]) for v in vals]
    inv = 1.0 / functools.reduce(lambda a, b: a + b, es)
    comb = jnp.zeros(logits.shape, F32)
    for sel, e in zip(sels, es):
        comb = comb + jnp.where(sel, e * inv, 0.0)
    comb_ref[...] = comb


def prenorm(x, g, scale, shift, router=None):
    m, d = x.shape
    tm = _tile(m, 256)
    row = lambda i: (i, 0)
    fixed = lambda i: (0, 0)
    vec = pl.BlockSpec((1, d), fixed)
    in_specs = [pl.BlockSpec((tm, d), row), vec, vec, vec]
    args = [x, g.reshape(1, d), scale.reshape(1, d), shift.reshape(1, d)]
    if router is None:
        return pl.pallas_call(
            _prenorm_kernel, grid=(m // tm,), in_specs=in_specs,
            out_specs=pl.BlockSpec((tm, d), row),
            out_shape=jax.ShapeDtypeStruct((m, d), BF16),
            compiler_params=_cparams("arbitrary"), name="prenorm",
        )(*args)
    rw, rb = router
    n_exp = rw.shape[-1]
    in_specs += [pl.BlockSpec((d, n_exp), fixed), pl.BlockSpec((1, n_exp), fixed)]
    args += [rw, rb.reshape(1, n_exp)]
    return pl.pallas_call(
        _prenorm_router_kernel, grid=(m // tm,), in_specs=in_specs,
        out_specs=[pl.BlockSpec((tm, d), row), pl.BlockSpec((tm, n_exp), row)],
        out_shape=[jax.ShapeDtypeStruct((m, d), BF16), jax.ShapeDtypeStruct((m, n_exp), F32)],
        compiler_params=_cparams("arbitrary"), name="prenorm_router",
    )(*args)


def _mm_kernel(*refs, epilogue):
    a_ref, w_ref = refs[0], refs[1]
    extras, o_ref = refs[2:-1], refs[-1]
    acc = jnp.dot(a_ref[...], w_ref[...], preferred_element_type=F32)
    o_ref[...] = epilogue(acc, *extras).astype(o_ref.dtype)


def matmul(a, w, w_spec, tm, tn, n_col_tiles, extras, epilogue, out_spec, out_shape, name):
    m, k = a.shape
    return pl.pallas_call(
        functools.partial(_mm_kernel, epilogue=epilogue),
        grid=(m // tm, n_col_tiles),
        in_specs=[pl.BlockSpec((tm, k), lambda i, j: (i, 0)), w_spec] + [s for _, s in extras],
        out_specs=out_spec,
        out_shape=out_shape,
        compiler_params=_cparams("arbitrary", "arbitrary"),
        name=name,
    )(a, w, *[x for x, _ in extras])


def _cast_epilogue(acc):
    return acc


def _residual_epilogue(acc, x_ref, gate_ref):
    return x_ref[...] + gate_ref[...] * acc


def _swiglu_epilogue(acc, b_ref, comb_ref):
    d_exp = acc.shape[-1] // 2
    gu = acc + b_ref[...]
    gate = jnp.minimum(gu[:, :d_exp], SWIGLU_LIMIT)
    lin = jnp.clip(gu[:, d_exp:], -SWIGLU_LIMIT, SWIGLU_LIMIT)
    act = gate * _sigmoid(SWIGLU_ALPHA * gate) * (lin + 1.0)
    comb = comb_ref[...]
    lane = lax.broadcasted_iota(jnp.int32, comb.shape, 1)
    cw = jnp.sum(jnp.where(lane == pl.program_id(1), comb, 0.0), axis=-1, keepdims=True)
    return act * cw


def _moe_down_epilogue(acc, comb_ref, bd_ref, x_ref, gate_ref):
    bias = jnp.dot(comb_ref[...], bd_ref[...], precision=HIGHEST, preferred_element_type=F32)
    return x_ref[...] + gate_ref[...] * (acc + bias)


def project_in(h, w_in):
    m, d = h.shape
    width = w_in.shape[-1]
    tm, tn = _tile(m, 1024), _tile(width, 1024)
    return matmul(h, w_in, pl.BlockSpec((d, tn), lambda i, j: (0, j)), tm, tn, width // tn, [],
                  _cast_epilogue, pl.BlockSpec((tm, tn), lambda i, j: (i, j)),
                  jax.ShapeDtypeStruct((m, width), BF16), "project_in")


def project_out(merged, w_o, x, gate):
    m, d = x.shape
    tm, tn = _tile(m, 1024), _tile(d, 512)
    tile = pl.BlockSpec((tm, tn), lambda i, j: (i, j))
    return matmul(merged, w_o, pl.BlockSpec((merged.shape[1], tn), lambda i, j: (0, j)), tm, tn, d // tn,
                  [(x, tile), (gate.reshape(1, d), pl.BlockSpec((1, tn), lambda i, j: (0, j)))],
                  _residual_epilogue, tile, jax.ShapeDtypeStruct((m, d), F32), "project_out")


def moe(h, comb, w_gu, b_gu, w_down, b_down, x, gate):
    m, d = h.shape
    n_exp, _, two_de = w_gu.shape
    d_exp = two_de // 2
    tm = _tile(m, 1024)
    act = matmul(
        h, w_gu, pl.BlockSpec((None, d, two_de), lambda i, j: (j, 0, 0)), tm, two_de, n_exp,
        [(b_gu.reshape(n_exp, 1, two_de), pl.BlockSpec((None, 1, two_de), lambda i, j: (j, 0, 0))),
         (comb, pl.BlockSpec((tm, n_exp), lambda i, j: (i, 0)))],
        _swiglu_epilogue, pl.BlockSpec((tm, d_exp), lambda i, j: (i, j)),
        jax.ShapeDtypeStruct((m, n_exp * d_exp), BF16), "moe_up")
    tm, tn = _tile(m, 512), _tile(d, 512)
    tile = pl.BlockSpec((tm, tn), lambda i, j: (i, j))
    return matmul(
        act, w_down.reshape(n_exp * d_exp, d), pl.BlockSpec((n_exp * d_exp, tn), lambda i, j: (0, j)),
        tm, tn, d // tn,
        [(comb, pl.BlockSpec((tm, n_exp), lambda i, j: (i, 0))),
         (b_down, pl.BlockSpec((n_exp, tn), lambda i, j: (0, j))),
         (x, tile), (gate.reshape(1, d), pl.BlockSpec((1, tn), lambda i, j: (0, j)))],
        _moe_down_epilogue, tile, jax.ShapeDtypeStruct((m, d), F32), "moe_down")


def _chunk_norm(x, g):
    return x * lax.rsqrt(jnp.mean(x * x, axis=-1, keepdims=True) + EPS) * g


def _qk_prep_kernel(*refs, rope):
    if rope:
        dq_ref, dk_ref, nq_ref, nk_ref, g_ref, cos_ref, sin_ref = refs[:7]
        cos, sin = cos_ref[...], sin_ref[...]
        lane = lax.broadcasted_iota(jnp.int32, cos.shape, 1)
        lower = (lane & (HEAD_DIM // 4)) == 0
    else:
        dq_ref, dk_ref, nq_ref, nk_ref, g_ref = refs[:5]
    outs = refs[-4:]
    ins = (dq_ref, dk_ref, nq_ref, nk_ref)
    for idx in range(4):
        g = g_ref[idx:idx + 1, :]
        rot = rope and idx < 2
        scale = Q_SCALE if idx in (0, 2) else 1.0
        for c in range(ins[idx].shape[-1] // HEAD_DIM):
            sl = slice(c * HEAD_DIM, (c + 1) * HEAD_DIM)
            y = _chunk_norm(ins[idx][:, sl].astype(F32), g)
            if rot:
                swapped = jnp.where(lower, pltpu.roll(y, HEAD_DIM - HEAD_DIM // 4, axis=1),
                                    pltpu.roll(y, HEAD_DIM // 4, axis=1))
                y = y * cos + swapped * sin
            outs[idx][:, sl] = (y * scale).astype(BF16)


def qk_prep(p, gains, rope, bw):
    m = p.shape[0]
    tm = _tile(m, 512)
    col = lambda c: pl.BlockSpec((tm, bw), lambda i, c=c: (i, c))
    in_specs = [col(0), col(1), col(3), col(4), pl.BlockSpec((4, HEAD_DIM), lambda i: (0, 0))]
    args = [p, p, p, p, gains]
    if rope is not None:
        in_specs += [pl.BlockSpec((tm, HEAD_DIM), lambda i: (i, 0))] * 2
        args += list(rope)
    return pl.pallas_call(
        functools.partial(_qk_prep_kernel, rope=rope is not None),
        grid=(m // tm,), in_specs=in_specs,
        out_specs=[pl.BlockSpec((tm, bw), lambda i: (i, 0))] * 4,
        out_shape=[jax.ShapeDtypeStruct((m, bw), BF16)] * 4,
        compiler_params=_cparams("arbitrary"), name="qk_prep",
    )(*args)


def rope_tables(n):
    t = jnp.arange(n, dtype=jnp.int32)
    row = (t // GRID_W).astype(F32)
    col = (t % GRID_W).astype(F32)
    n_freq = HEAD_DIM // 4
    inv = ROPE_BASE ** (-jnp.arange(n_freq, dtype=F32) / n_freq)
    ar, ac = row[:, None] * inv, col[:, None] * inv
    ang = jnp.concatenate([ar, ar, ac, ac], axis=-1)
    lane = jnp.arange(HEAD_DIM)
    sign = jnp.where((lane & n_freq) == 0, -1.0, 1.0).astype(F32)
    return jnp.cos(ang), jnp.sin(ang) * sign


def _diff_attn_kernel(lam_ref, q_ref, k_ref, v_ref, *rest, n_main, has_extra, out_scale):
    if has_extra:
        kx_ref, vx_ref, g_ref, o_ref, m_sc, l_sc, acc_sc = rest
    else:
        g_ref, o_ref, m_sc, l_sc, acc_sc = rest
    kj = pl.program_id(2)
    n_steps = n_main + (1 if has_extra else 0)

    @pl.when(kj == 0)
    def _():
        m_sc[...] = jnp.full(m_sc.shape, -jnp.inf, F32)
        l_sc[...] = jnp.zeros(l_sc.shape, F32)
        acc_sc[...] = jnp.zeros(acc_sc.shape, F32)

    def update(kr, vr):
        v = vr[...]
        for hf in range(2):
            sl = slice(hf * HEAD_DIM, (hf + 1) * HEAD_DIM)
            s = lax.dot_general(q_ref[:, sl], kr[:, sl], (((1,), (1,)), ((), ())),
                                preferred_element_type=F32)
            m_prev = m_sc[hf]
            m_new = jnp.maximum(m_prev, jnp.max(s, axis=-1, keepdims=True))
            alpha = jnp.exp2(m_prev - m_new)
            p = jnp.exp2(s - m_new)
            l_sc[hf] = alpha * l_sc[hf] + jnp.sum(p, axis=-1, keepdims=True)
            acc_sc[hf] = alpha * acc_sc[hf] + jnp.dot(p.astype(BF16), v, preferred_element_type=F32)
            m_sc[hf] = m_new

    if has_extra:
        pl.when(kj < n_main)(lambda: update(k_ref, v_ref))
        pl.when(kj == n_main)(lambda: update(kx_ref, vx_ref))
    else:
        update(k_ref, v_ref)

    @pl.when(kj == n_steps - 1)
    def _():
        _diff_finalize(lam_ref, g_ref, o_ref, acc_sc[0], acc_sc[1], l_sc[0], l_sc[1], out_scale)


def _diff_finalize(lam_ref, g_ref, o_ref, acc0, acc1, l0, l1, out_scale):
    lam = lam_ref[0, 0]
    o = acc0 * (1.0 / l0) - lam * (acc1 * (1.0 / l1))
    y = o * lax.rsqrt(jnp.mean(o * o, axis=-1, keepdims=True) + EPS)
    o_ref[...] = (y * g_ref[...] * out_scale).astype(o_ref.dtype)


def _diff_attn_bounded_kernel(lam_ref, q_ref, k_ref, v_ref, kx_ref, vx_ref, g_ref, o_ref, l_sc, acc_sc,
                              *, tk, n_main, out_scale):
    l_sc[...] = jnp.zeros(l_sc.shape, F32)
    acc_sc[...] = jnp.zeros(acc_sc.shape, F32)

    def accumulate(k, v):
        for hf in range(2):
            sl = slice(hf * HEAD_DIM, (hf + 1) * HEAD_DIM)
            s = lax.dot_general(q_ref[:, sl], k[:, sl], (((1,), (1,)), ((), ())), preferred_element_type=F32)
            p = jnp.exp2(s)
            part = p[:, 0:HEAD_DIM]
            for c in range(1, p.shape[1] // HEAD_DIM):
                part = part + p[:, c * HEAD_DIM:(c + 1) * HEAD_DIM]
            l_sc[hf] += part
            acc_sc[hf] += jnp.dot(p.astype(BF16), v, preferred_element_type=F32)

    def body(j, carry):
        off = pl.multiple_of(j * tk, tk)
        accumulate(k_ref[pl.ds(off, tk), :], v_ref[pl.ds(off, tk), :])
        return carry

    lax.fori_loop(0, n_main, body, 0)
    accumulate(kx_ref[...], vx_ref[...])
    _diff_finalize(lam_ref, g_ref, o_ref, acc_sc[0], acc_sc[1],
                   jnp.sum(l_sc[0], axis=-1, keepdims=True), jnp.sum(l_sc[1], axis=-1, keepdims=True), out_scale)


def diff_attention_bounded(q, k, v_src, v_col0, lam, sub_g, out_scale, extra):
    mq, bw = q.shape
    mk = k.shape[0]
    hw = 2 * HEAD_DIM
    tq, tk = _tile(mq, 1024), _tile(mk, 1024)
    kx, vx_src, vx_col0 = extra
    mx = kx.shape[0]
    once = pl.Buffered(1)
    return pl.pallas_call(
        functools.partial(_diff_attn_bounded_kernel, tk=tk, n_main=mk // tk, out_scale=out_scale),
        grid=(bw // hw, mq // tq),
        in_specs=[
            pl.BlockSpec(memory_space=pltpu.SMEM),
            pl.BlockSpec((tq, hw), lambda h, i: (i, h)),
            pl.BlockSpec((mk, hw), lambda h, i: (0, h), pipeline_mode=once),
            pl.BlockSpec((mk, hw), lambda h, i: (0, v_col0 + h), pipeline_mode=once),
            pl.BlockSpec((mx, hw), lambda h, i: (0, h)),
            pl.BlockSpec((mx, hw), lambda h, i: (0, vx_col0 + h)),
            pl.BlockSpec((1, hw), lambda h, i: (0, 0)),
        ],
        out_specs=pl.BlockSpec((tq, hw), lambda h, i: (i, h)),
        out_shape=jax.ShapeDtypeStruct((mq, bw), BF16),
        scratch_shapes=[pltpu.VMEM((2, tq, HEAD_DIM), F32), pltpu.VMEM((2, tq, hw), F32)],
        compiler_params=_cparams("arbitrary", "arbitrary"),
        name="diff_attention_bounded",
    )(lam.reshape(1, 1).astype(F32), q, k, v_src, kx, vx_src, sub_g.reshape(1, hw))


def score_bound_log2(g_q, g_k):
    return Q_SCALE * HEAD_DIM * jnp.max(jnp.abs(g_q)) * jnp.max(jnp.abs(g_k)) * 1.01


def diff_attention(q, k, v_src, v_col0, lam, sub_g, out_scale, extra=None):
    mq, bw = q.shape
    mk = k.shape[0]
    hw = 2 * HEAD_DIM
    heads = bw // hw
    tq, tk = _tile(mq, 1024), _tile(mk, 1024)
    n_main = mk // tk
    last = n_main - 1
    kmap = lambda h, i, j: (jnp.minimum(j, last), h)
    in_specs = [
        pl.BlockSpec(memory_space=pltpu.SMEM),
        pl.BlockSpec((tq, hw), lambda h, i, j: (i, h)),
        pl.BlockSpec((tk, hw), kmap),
        pl.BlockSpec((tk, hw), lambda h, i, j: (jnp.minimum(j, last), v_col0 + h)),
    ]
    args = [lam.reshape(1, 1).astype(F32), q, k, v_src]
    if extra is not None:
        kx, vx_src, vx_col0 = extra
        mx = kx.shape[0]
        in_specs += [pl.BlockSpec((mx, hw), lambda h, i, j: (0, h)),
                     pl.BlockSpec((mx, hw), lambda h, i, j: (0, vx_col0 + h))]
        args += [kx, vx_src]
    in_specs.append(pl.BlockSpec((1, hw), lambda h, i, j: (0, 0)))
    args.append(sub_g.reshape(1, hw))
    return pl.pallas_call(
        functools.partial(_diff_attn_kernel, n_main=n_main, has_extra=extra is not None, out_scale=out_scale),
        grid=(heads, mq // tq, n_main + (extra is not None)),
        in_specs=in_specs,
        out_specs=pl.BlockSpec((tq, hw), lambda h, i, j: (i, h)),
        out_shape=jax.ShapeDtypeStruct((mq, bw), BF16),
        scratch_shapes=[pltpu.VMEM((2, tq, 1), F32), pltpu.VMEM((2, tq, 1), F32), pltpu.VMEM((2, tq, hw), F32)],
        compiler_params=_cparams("arbitrary", "arbitrary", "arbitrary"),
        name="diff_attention",
    )(*args)


def _na_kernel(q_ref, k_ref, v_ref, kc_ref, vc_ref, bias_ref, o_ref, *, rows):
    b = pl.program_id(1)
    start_row = jnp.clip(b * NA_Q_ROWS - NA_WIN_R // 2, 0, rows - NA_K_ROWS)
    start = pl.multiple_of(start_row * GRID_W, GRID_W * (NA_WIN_R // 2))
    n_keys = NA_K_ROWS * GRID_W
    q = q_ref[...]
    k = k_ref[pl.ds(start, n_keys), :]
    v = v_ref[pl.ds(start, n_keys), :]
    nt = (((1,), (1,)), ((), ()))
    s_loc = lax.dot_general(q, k, nt, preferred_element_type=F32) + bias_ref[...]
    s_ctx = lax.dot_general(q, kc_ref[...], nt, preferred_element_type=F32)
    m = jnp.maximum(jnp.max(s_loc, axis=-1, keepdims=True), jnp.max(s_ctx, axis=-1, keepdims=True))
    p_loc = jnp.exp2(s_loc - m)
    p_ctx = jnp.exp2(s_ctx - m)
    denom = jnp.sum(p_loc, axis=-1, keepdims=True) + jnp.sum(p_ctx, axis=-1, keepdims=True)
    o = (jnp.dot(p_loc.astype(BF16), v, preferred_element_type=F32)
         + jnp.dot(p_ctx.astype(BF16), vc_ref[...], preferred_element_type=F32))
    o_ref[...] = (o * (1.0 / denom)).astype(o_ref.dtype)


def na_bias_tables(rpb, rows):
    n_blocks = rows // NA_Q_ROWS
    wr = min(NA_WIN_R, rows)
    w = jnp.arange(GRID_W)
    col_start = jnp.clip(w - NA_WIN_C // 2, 0, GRID_W - NA_WIN_C)
    kc = jnp.arange(GRID_W)
    valid_c = (kc[None, :] >= col_start[:, None]) & (kc[None, :] < col_start[:, None] + NA_WIN_C)
    rel_c = jnp.clip(kc[None, :] - w[:, None] + (NA_WIN_C - 1), 0, 2 * NA_WIN_C - 2)
    tables = []
    for blk in (0, 1, n_blocks - 1):
        r = blk * NA_Q_ROWS + jnp.arange(NA_Q_ROWS)
        start = jnp.clip(blk * NA_Q_ROWS - NA_WIN_R // 2, 0, rows - NA_K_ROWS)
        kr = start + jnp.arange(NA_K_ROWS)
        row_start = jnp.clip(r - NA_WIN_R // 2, 0, rows - wr)
        valid_r = (kr[None, :] >= row_start[:, None]) & (kr[None, :] < row_start[:, None] + wr)
        rel_r = jnp.clip(kr[None, :] - r[:, None] + (NA_WIN_R - 1), 0, 2 * NA_WIN_R - 2)
        bias = rpb[:, rel_r][:, :, :, rel_c]
        valid = valid_r[:, :, None, None] & valid_c[None, None]
        bias = jnp.where(valid[None], bias * LOG2E, MASKED)
        bias = jnp.transpose(bias, (0, 1, 3, 2, 4))
        tables.append(bias.reshape(rpb.shape[0], NA_Q_ROWS * GRID_W, NA_K_ROWS * GRID_W))
    return jnp.stack(tables, axis=1).astype(F32)


def neighbourhood_attention(q, k, v_src, v_col0, kc, vc_src, vc_col0, bias):
    s, bw = q.shape
    heads = bw // HEAD_DIM
    rows = s // GRID_W
    n_blocks = rows // NA_Q_ROWS
    assert n_blocks >= 3 and rows % NA_Q_ROWS == 0
    tq = NA_Q_ROWS * GRID_W
    lc = kc.shape[0]
    variant = lambda b: jnp.where(b == 0, 0, jnp.where(b == n_blocks - 1, 2, 1))
    return pl.pallas_call(
        functools.partial(_na_kernel, rows=rows),
        grid=(heads, n_blocks),
        in_specs=[
            pl.BlockSpec((tq, HEAD_DIM), lambda h, b: (b, h)),
            pl.BlockSpec((s, HEAD_DIM), lambda h, b: (0, h)),
            pl.BlockSpec((s, HEAD_DIM), lambda h, b: (0, v_col0 + h)),
            pl.BlockSpec((lc, HEAD_DIM), lambda h, b: (0, h)),
            pl.BlockSpec((lc, HEAD_DIM), lambda h, b: (0, vc_col0 + h)),
            pl.BlockSpec((None, None, tq, NA_K_ROWS * GRID_W), lambda h, b: (h, variant(b), 0, 0)),
        ],
        out_specs=pl.BlockSpec((tq, HEAD_DIM), lambda h, b: (b, h)),
        out_shape=jax.ShapeDtypeStruct((s, bw), BF16),
        compiler_params=_cparams("arbitrary", "arbitrary"),
        name="neighbourhood_attention",
    )(q, k, v_src, kc, vc_src, bias)


def _dense_attn_kernel(q_ref, k_ref, v_ref, o_ref):
    s = lax.dot_general(q_ref[...], k_ref[...], (((1,), (1,)), ((), ())), preferred_element_type=F32)
    p = jnp.exp2(s - jnp.max(s, axis=-1, keepdims=True))
    o = jnp.dot(p.astype(BF16), v_ref[...], preferred_element_type=F32)
    o_ref[...] = (o * (1.0 / jnp.sum(p, axis=-1, keepdims=True))).astype(o_ref.dtype)


def dense_attention(q, k, v_src, v_col0):
    m, bw = q.shape
    heads = bw // HEAD_DIM
    blk = lambda c0: pl.BlockSpec((m, HEAD_DIM), lambda h: (0, c0 + h))
    return pl.pallas_call(
        _dense_attn_kernel, grid=(heads,), in_specs=[blk(0), blk(0), blk(v_col0)], out_specs=blk(0),
        out_shape=jax.ShapeDtypeStruct((m, bw), BF16),
        compiler_params=_cparams("arbitrary"), name="dense_attention",
    )(q, k, v_src)


def _merge_up_kernel(od_ref, on_ref, cb_ref, cc_ref, cx_ref, ccp_ref, cxp_ref, ccn_ref, cxn_ref, cw_ref,
                     g0_ref, g1_ref, g2_ref, wd_ref, wn_ref, wc_ref, o_ref, oc_sc):
    i, j = pl.program_id(0), pl.program_id(1)

    @pl.when(j == 0)
    def _():
        u = cc_ref[...].astype(F32) * cx_ref[...].astype(F32)
        tm = u.shape[0]
        halo = ccp_ref.shape[0]
        prev_row = ccp_ref[halo - 1:halo, :].astype(F32) * cxp_ref[halo - 1:halo, :].astype(F32)
        next_row = ccn_ref[0:1, :].astype(F32) * cxn_ref[0:1, :].astype(F32)
        prev_row = jnp.where(i == 0, 0.0, prev_row)
        next_row = jnp.where(i == pl.num_programs(0) - 1, 0.0, next_row)
        r = lax.broadcasted_iota(jnp.int32, u.shape, 0)
        u_prev = jnp.where(r == 0, prev_row, pltpu.roll(u, 1, axis=0))
        u_next = jnp.where(r == tm - 1, next_row, pltpu.roll(u, tm - 1, axis=0))
        conv = cw_ref[0:1, :] * u_prev + cw_ref[1:2, :] * u + cw_ref[2:3, :] * u_next
        oc_sc[...] = (cb_ref[...].astype(F32) * conv).astype(BF16)

    acc = _sigmoid(g0_ref[...].astype(F32)) * jnp.dot(od_ref[...], wd_ref[...], preferred_element_type=F32)
    acc += _sigmoid(g1_ref[...].astype(F32)) * jnp.dot(on_ref[...], wn_ref[...], preferred_element_type=F32)
    acc += _sigmoid(g2_ref[...].astype(F32)) * jnp.dot(oc_sc[...], wc_ref[...], preferred_element_type=F32)
    o_ref[...] = acc.astype(o_ref.dtype)


def merge_up(o_diff, o_na, p, conv_w, w_up_diff, w_up_na, w_up_conv, d):
    m, bw = o_diff.shape
    tm, tn = _tile(m, 512), math.gcd(_tile(d, 512), bw)
    halo = 8
    n_halo = m // halo
    row_blk = lambda c: pl.BlockSpec((tm, bw), lambda i, j, c=c: (i, c))
    prev_blk = lambda c: pl.BlockSpec((halo, bw), lambda i, j, c=c: (jnp.maximum(i * (tm // halo) - 1, 0), c))
    next_blk = lambda c: pl.BlockSpec((halo, bw), lambda i, j, c=c: (jnp.minimum((i + 1) * (tm // halo), n_halo - 1), c))
    gate0 = 9 * bw // tn
    gate_blk = lambda b: pl.BlockSpec((tm, tn), lambda i, j, b=b: (i, gate0 + b * (d // tn) + j))
    w_blk = pl.BlockSpec((bw, tn), lambda i, j: (0, j))
    assert (9 * bw) % tn == 0
    return pl.pallas_call(
        _merge_up_kernel,
        grid=(m // tm, d // tn),
        in_specs=[row_blk(0), row_blk(0), row_blk(6), row_blk(7), row_blk(8),
                  prev_blk(7), prev_blk(8), next_blk(7), next_blk(8),
                  pl.BlockSpec((CONV_K, bw), lambda i, j: (0, 0)),
                  gate_blk(0), gate_blk(1), gate_blk(2), w_blk, w_blk, w_blk],
        out_specs=pl.BlockSpec((tm, tn), lambda i, j: (i, j)),
        out_shape=jax.ShapeDtypeStruct((m, d), BF16),
        scratch_shapes=[pltpu.VMEM((tm, bw), BF16)],
        compiler_params=_cparams("arbitrary", "arbitrary"),
        name="merge_up",
    )(o_diff, o_na, p, p, p, p, p, p, p, conv_w, p, p, p, w_up_diff, w_up_na, w_up_conv)


def kernel(x, c, ctx, c_ctx, w_mod_a, w_mod_b, b_mod, norm1_g, w_in, diff_qn_g, diff_kn_g, lambda_q1, lambda_k1, lambda_q2, lambda_k2, diff_subln_g, na_qn_g, na_kn_g, na_rpb, conv_w, w_up_diff, w_up_na, w_up_conv, w_o, norm2_g, router_w, router_b, w_gu, b_gu, w_down, b_down):
    batch, s, d = x.shape
    assert batch == 1 and ctx.shape[0] == 1
    depth = w_in.shape[0]
    bw = w_up_diff.shape[1]
    dv0 = 2 * bw // (2 * HEAD_DIM)
    nv0 = 5 * bw // HEAD_DIM
    rows = s // GRID_W

    cond = jnp.zeros((8, d), F32).at[0].set(c[0]).at[1].set(c_ctx)
    mod = modulation(cond, w_mod_a, w_mod_b, b_mod).reshape(depth, 8, N_MOD, d)
    rope = rope_tables(s)

    xl, xc = x[0], ctx[0]
    for i in range(depth):
        last = i == depth - 1
        lam_init = 0.8 - 0.6 * math.exp(-0.3 * i)
        lam = (jnp.exp(jnp.sum(lambda_q1[i] * lambda_k1[i])) - jnp.exp(jnp.sum(lambda_q2[i] * lambda_k2[i]))
               + lam_init)
        ml, mc = mod[i, 0], mod[i, 1]
        gains = jnp.stack([diff_qn_g[i], diff_kn_g[i], na_qn_g[i], na_kn_g[i]])
        w_in_i = w_in[i].astype(BF16)

        p_l = project_in(prenorm(xl, norm1_g[i], ml[1], ml[0]), w_in_i)
        p_c = project_in(prenorm(xc, norm1_g[i], mc[1], mc[0]), w_in_i)
        dq_l, dk_l, nq_l, nk_l = qk_prep(p_l, gains, rope, bw)
        dq_c, dk_c, nq_c, nk_c = qk_prep(p_c, gains, None, bw)

        diff_args = (dq_l, dk_l, p_l, dv0, lam, diff_subln_g[i], 1.0 - lam_init, (dk_c, p_c, dv0))
        o_diff_l = lax.cond(score_bound_log2(diff_qn_g[i], diff_kn_g[i]) <= SCORE_BOUND_LOG2,
                            lambda: diff_attention_bounded(*diff_args), lambda: diff_attention(*diff_args))
        o_na_l = neighbourhood_attention(nq_l, nk_l, p_l, nv0, nk_c, p_c, nv0,
                                         na_bias_tables(na_rpb[i], rows))
        ups = (w_up_diff[i].astype(BF16), w_up_na[i].astype(BF16), w_up_conv[i].astype(BF16))
        w_o_i = w_o[i].astype(BF16)
        xl = project_out(merge_up(o_diff_l, o_na_l, p_l, conv_w[i], *ups, d), w_o_i, xl, ml[2])
        if not last:
            o_diff_c = diff_attention(dq_c, dk_c, p_c, dv0, lam, diff_subln_g[i], 1.0 - lam_init)
            o_na_c = dense_attention(nq_c, nk_c, p_c, nv0)
            xc = project_out(merge_up(o_diff_c, o_na_c, p_c, conv_w[i], *ups, d), w_o_i, xc, mc[2])

        w_gu_i, w_down_i = w_gu[i].astype(BF16), w_down[i].astype(BF16)
        router = (router_w[i], router_b[i])
        h_l, comb_l = prenorm(xl, norm2_g[i], ml[4], ml[3], router)
        xl = moe(h_l, comb_l, w_gu_i, b_gu[i], w_down_i, b_down[i], xl, ml[5])
        if not last:
            h_c, comb_c = prenorm(xc, norm2_g[i], mc[4], mc[3], router)
            xc = moe(h_c, comb_c, w_gu_i, b_gu[i], w_down_i, b_down[i], xc, mc[5])
    return xl[None]
```

```python
import functools
import math

import jax
import jax.numpy as jnp
from jax import lax
from jax.experimental import pallas as pl
from jax.experimental.pallas import tpu as pltpu

F32 = jnp.float32
BF16 = jnp.bfloat16
HIGHEST = lax.Precision.HIGHEST

GRID_W = 64
HEAD_DIM = 128
NA_WIN_R = 8
NA_WIN_C = 16
CONV_K = 3
N_BRANCH = 3
N_MOD = 6
TOP_K = 4
SWIGLU_LIMIT = 7.0
SWIGLU_ALPHA = 1.702
ROPE_BASE = 10000.0
EPS = 1e-6
LOG2E = math.log2(math.e)
Q_SCALE = HEAD_DIM ** -0.5 * LOG2E
MASKED = -1e30
SCORE_BOUND_LOG2 = 60.0

V7X_VMEM_LIMIT_BYTES = 56 * 1024 * 1024
NA_Q_ROWS = 8
NA_K_ROWS = NA_Q_ROWS + NA_WIN_R


def _cparams(*sem):
    return pltpu.CompilerParams(dimension_semantics=sem, vmem_limit_bytes=V7X_VMEM_LIMIT_BYTES)


def _tile(n, target):
    if n <= target:
        return n
    t = target
    while n % t:
        t -= 8
    assert t > 0, (n, target)
    return t


def _sigmoid(x):
    return 1.0 / (1.0 + jnp.exp(-x))


def _mod_kernel(c_ref, wa_ref, wb_ref, b_ref, o_ref):
    c = c_ref[...]
    s = c * _sigmoid(c)
    m1 = jnp.dot(s, wa_ref[...], precision=HIGHEST, preferred_element_type=F32)
    o_ref[...] = jnp.dot(m1, wb_ref[...], precision=HIGHEST, preferred_element_type=F32) + b_ref[...]


def modulation(cond, w_a, w_b, b):
    depth, d, r = w_a.shape
    width = w_b.shape[-1]
    rows = cond.shape[0]
    tn = _tile(width, 4096)
    return pl.pallas_call(
        _mod_kernel,
        grid=(depth, width // tn),
        in_specs=[
            pl.BlockSpec((rows, d), lambda l, j: (0, 0)),
            pl.BlockSpec((None, d, r), lambda l, j: (l, 0, 0)),
            pl.BlockSpec((None, r, tn), lambda l, j: (l, 0, j)),
            pl.BlockSpec((None, 1, tn), lambda l, j: (l, 0, j)),
        ],
        out_specs=pl.BlockSpec((None, rows, tn), lambda l, j: (l, 0, j)),
        out_shape=jax.ShapeDtypeStruct((depth, rows, width), F32),
        compiler_params=_cparams("arbitrary", "arbitrary"),
        name="modulation",
    )(cond, w_a, w_b, b.reshape(depth, 1, width))


def _modulated_norm(x_ref, g_ref, sc_ref, sh_ref):
    x = x_ref[...]
    y = x * lax.rsqrt(jnp.mean(x * x, axis=-1, keepdims=True) + EPS)
    return (y * g_ref[...]) * (1.0 + sc_ref[...]) + sh_ref[...]


def _prenorm_kernel(x_ref, g_ref, sc_ref, sh_ref, o_ref):
    o_ref[...] = _modulated_norm(x_ref, g_ref, sc_ref, sh_ref).astype(o_ref.dtype)


def _prenorm_router_kernel(x_ref, g_ref, sc_ref, sh_ref, rw_ref, rb_ref, o_ref, comb_ref):
    h = _modulated_norm(x_ref, g_ref, sc_ref, sh_ref)
    o_ref[...] = h.astype(o_ref.dtype)
    logits = jnp.dot(h, rw_ref[...], precision=HIGHEST, preferred_element_type=F32) + rb_ref[...]
    n_exp = logits.shape[-1]
    lane = lax.broadcasted_iota(jnp.int32, logits.shape, 1).astype(F32)
    sels, vals = [], []
    for _ in range(TOP_K):
        m = jnp.max(logits, axis=-1, keepdims=True)
        first = jnp.min(jnp.where(logits == m, lane, float(n_exp)), axis=-1, keepdims=True)
        sel = lane == first
        sels.append(sel)
        vals.append(m)
        logits = jnp.where(sel, -jnp.inf, logits)
    es = [jnp.exp(v - vals[0]) for v in vals]
    inv = 1.0 / functools.reduce(lambda a, b: a + b, es)
    comb = jnp.zeros(logits.shape, F32)
    for sel, e in zip(sels, es):
        comb = comb + jnp.where(sel, e * inv, 0.0)
    comb_ref[...] = comb


def prenorm(x, g, scale, shift, router=None):
    m, d = x.shape
    tm = _tile(m, 256)
    row = lambda i: (i, 0)
    fixed = lambda i: (0, 0)
    vec = pl.BlockSpec((1, d), fixed)
    in_specs = [pl.BlockSpec((tm, d), row), vec, vec, vec]
    args = [x, g.reshape(1, d), scale.reshape(1, d), shift.reshape(1, d)]
    if router is None:
        return pl.pallas_call(
            _prenorm_kernel, grid=(m // tm,), in_specs=in_specs,
            out_specs=pl.BlockSpec((tm, d), row),
            out_shape=jax.ShapeDtypeStruct((m, d), BF16),
            compiler_params=_cparams("arbitrary"), name="prenorm",
        )(*args)
    rw, rb = router
    n_exp = rw.shape[-1]
    in_specs += [pl.BlockSpec((d, n_exp), fixed), pl.BlockSpec((1, n_exp), fixed)]
    args += [rw, rb.reshape(1, n_exp)]
    return pl.pallas_call(
        _prenorm_router_kernel, grid=(m // tm,), in_specs=in_specs,
        out_specs=[pl.BlockSpec((tm, d), row), pl.BlockSpec((tm, n_exp), row)],
        out_shape=[jax.ShapeDtypeStruct((m, d), BF16), jax.ShapeDtypeStruct((m, n_exp), F32)],
        compiler_params=_cparams("arbitrary"), name="prenorm_router",
    )(*args)


def _mm_kernel(*refs, epilogue):
    a_ref, w_ref = refs[0], refs[1]
    extras, o_ref = refs[2:-1], refs[-1]
    acc = jnp.dot(a_ref[...], w_ref[...], preferred_element_type=F32)
    o_ref[...] = epilogue(acc, *extras).astype(o_ref.dtype)


def matmul(a, w, w_spec, tm, tn, n_col_tiles, extras, epilogue, out_spec, out_shape, name):
    m, k = a.shape
    return pl.pallas_call(
        functools.partial(_mm_kernel, epilogue=epilogue),
        grid=(m // tm, n_col_tiles),
        in_specs=[pl.BlockSpec((tm, k), lambda i, j: (i, 0)), w_spec] + [s for _, s in extras],
        out_specs=out_spec,
        out_shape=out_shape,
        compiler_params=_cparams("arbitrary", "arbitrary"),
        name=name,
    )(a, w, *[x for x, _ in extras])


def _cast_epilogue(acc):
    return acc


def _residual_epilogue(acc, x_ref, gate_ref):
    return x_ref[...] + gate_ref[...] * acc


def _swiglu_epilogue(acc, b_ref, comb_ref):
    d_exp = acc.shape[-1] // 2
    gu = acc + b_ref[...]
    gate = jnp.minimum(gu[:, :d_exp], SWIGLU_LIMIT)
    lin = jnp.clip(gu[:, d_exp:], -SWIGLU_LIMIT, SWIGLU_LIMIT)
    act = gate * _sigmoid(SWIGLU_ALPHA * gate) * (lin + 1.0)
    comb = comb_ref[...]
    lane = lax.broadcasted_iota(jnp.int32, comb.shape, 1)
    cw = jnp.sum(jnp.where(lane == pl.program_id(1), comb, 0.0), axis=-1, keepdims=True)
    return act * cw


def _moe_down_epilogue(acc, comb_ref, bd_ref, x_ref, gate_ref):
    bias = jnp.dot(comb_ref[...].astype(BF16), bd_ref[...].astype(BF16), preferred_element_type=F32)
    return x_ref[...] + gate_ref[...] * (acc + bias)


def project_in(h, w_in, layer):
    m, d = h.shape
    width = w_in.shape[-1]
    tm, tn = _tile(m, 1024), _tile(width, 1024)
    return matmul(h, w_in, pl.BlockSpec((None, d, tn), lambda i, j: (layer, 0, j)), tm, tn, width // tn, [],
                  _cast_epilogue, pl.BlockSpec((tm, tn), lambda i, j: (i, j)),
                  jax.ShapeDtypeStruct((m, width), BF16), "project_in")


def project_out(merged, w_o, layer, x, gate):
    m, d = x.shape
    tm, tn = _tile(m, 1024), _tile(d, 512)
    tile = pl.BlockSpec((tm, tn), lambda i, j: (i, j))
    return matmul(merged, w_o, pl.BlockSpec((None, merged.shape[1], tn), lambda i, j: (layer, 0, j)),
                  tm, tn, d // tn,
                  [(x, tile), (gate.reshape(1, d), pl.BlockSpec((1, tn), lambda i, j: (0, j)))],
                  _residual_epilogue, tile, jax.ShapeDtypeStruct((m, d), F32), "project_out")


def moe(h, comb, w_gu, b_gu, w_down, b_down, layer, x, gate):
    m, d = h.shape
    depth, n_exp, _, two_de = w_gu.shape
    d_exp = two_de // 2
    tm = _tile(m, 1024)
    act = matmul(
        h, w_gu, pl.BlockSpec((None, None, d, two_de), lambda i, j: (layer, j, 0, 0)), tm, two_de, n_exp,
        [(b_gu.reshape(n_exp, 1, two_de), pl.BlockSpec((None, 1, two_de), lambda i, j: (j, 0, 0))),
         (comb, pl.BlockSpec((tm, n_exp), lambda i, j: (i, 0)))],
        _swiglu_epilogue, pl.BlockSpec((tm, d_exp), lambda i, j: (i, j)),
        jax.ShapeDtypeStruct((m, n_exp * d_exp), BF16), "moe_up")
    tm, tn = _tile(m, 512), _tile(d, 512)
    tile = pl.BlockSpec((tm, tn), lambda i, j: (i, j))
    return matmul(
        act, w_down.reshape(depth, n_exp * d_exp, d),
        pl.BlockSpec((None, n_exp * d_exp, tn), lambda i, j: (layer, 0, j)), tm, tn, d // tn,
        [(comb, pl.BlockSpec((tm, n_exp), lambda i, j: (i, 0))),
         (b_down, pl.BlockSpec((n_exp, tn), lambda i, j: (0, j))),
         (x, tile), (gate.reshape(1, d), pl.BlockSpec((1, tn), lambda i, j: (0, j)))],
        _moe_down_epilogue, tile, jax.ShapeDtypeStruct((m, d), F32), "moe_down")


def _chunk_norm(x, g):
    return x * lax.rsqrt(jnp.mean(x * x, axis=-1, keepdims=True) + EPS) * g


def _qk_prep_kernel(*refs, rope):
    if rope:
        dq_ref, dk_ref, nq_ref, nk_ref, g_ref, cos_ref, sin_ref = refs[:7]
        cos, sin = cos_ref[...], sin_ref[...]
        lane = lax.broadcasted_iota(jnp.int32, cos.shape, 1)
        lower = (lane & (HEAD_DIM // 4)) == 0
    else:
        dq_ref, dk_ref, nq_ref, nk_ref, g_ref = refs[:5]
    outs = refs[-4:]
    ins = (dq_ref, dk_ref, nq_ref, nk_ref)
    for idx in range(4):
        g = g_ref[idx:idx + 1, :]
        rot = rope and idx < 2
        scale = Q_SCALE if idx in (0, 2) else 1.0
        for c in range(ins[idx].shape[-1] // HEAD_DIM):
            sl = slice(c * HEAD_DIM, (c + 1) * HEAD_DIM)
            y = _chunk_norm(ins[idx][:, sl].astype(F32), g)
            if rot:
                swapped = jnp.where(lower, pltpu.roll(y, HEAD_DIM - HEAD_DIM // 4, axis=1),
                                    pltpu.roll(y, HEAD_DIM // 4, axis=1))
                y = y * cos + swapped * sin
            outs[idx][:, sl] = (y * scale).astype(BF16)


def qk_prep(p, gains, rope, bw):
    m = p.shape[0]
    tm = _tile(m, 512)
    col = lambda c: pl.BlockSpec((tm, bw), lambda i, c=c: (i, c))
    in_specs = [col(0), col(1), col(3), col(4), pl.BlockSpec((4, HEAD_DIM), lambda i: (0, 0))]
    args = [p, p, p, p, gains]
    if rope is not None:
        in_specs += [pl.BlockSpec((tm, HEAD_DIM), lambda i: (i, 0))] * 2
        args += list(rope)
    return pl.pallas_call(
        functools.partial(_qk_prep_kernel, rope=rope is not None),
        grid=(m // tm,), in_specs=in_specs,
        out_specs=[pl.BlockSpec((tm, bw), lambda i: (i, 0))] * 4,
        out_shape=[jax.ShapeDtypeStruct((m, bw), BF16)] * 4,
        compiler_params=_cparams("arbitrary"), name="qk_prep",
    )(*args)


def rope_tables(n):
    t = jnp.arange(n, dtype=jnp.int32)
    row = (t // GRID_W).astype(F32)
    col = (t % GRID_W).astype(F32)
    n_freq = HEAD_DIM // 4
    inv = ROPE_BASE ** (-jnp.arange(n_freq, dtype=F32) / n_freq)
    ar, ac = row[:, None] * inv, col[:, None] * inv
    ang = jnp.concatenate([ar, ar, ac, ac], axis=-1)
    lane = jnp.arange(HEAD_DIM)
    sign = jnp.where((lane & n_freq) == 0, -1.0, 1.0).astype(F32)
    return jnp.cos(ang), jnp.sin(ang) * sign


def _diff_attn_kernel(lam_ref, q_ref, k_ref, v_ref, *rest, n_main, has_extra, out_scale):
    if has_extra:
        kx_ref, vx_ref, g_ref, o_ref, m_sc, l_sc, acc_sc = rest
    else:
        g_ref, o_ref, m_sc, l_sc, acc_sc = rest
    kj = pl.program_id(2)
    n_steps = n_main + (1 if has_extra else 0)

    @pl.when(kj == 0)
    def _():
        m_sc[...] = jnp.full(m_sc.shape, -jnp.inf, F32)
        l_sc[...] = jnp.zeros(l_sc.shape, F32)
        acc_sc[...] = jnp.zeros(acc_sc.shape, F32)

    def update(kr, vr):
        v = vr[...]
        for hf in range(2):
            sl = slice(hf * HEAD_DIM, (hf + 1) * HEAD_DIM)
            s = lax.dot_general(q_ref[:, sl], kr[:, sl], (((1,), (1,)), ((), ())),
                                preferred_element_type=F32)
            m_prev = m_sc[hf]
            m_new = jnp.maximum(m_prev, jnp.max(s, axis=-1, keepdims=True))
            alpha = jnp.exp2(m_prev - m_new)
            p = jnp.exp2(s - m_new)
            l_sc[hf] = alpha * l_sc[hf] + jnp.sum(p, axis=-1, keepdims=True)
            acc_sc[hf] = alpha * acc_sc[hf] + jnp.dot(p.astype(BF16), v, preferred_element_type=F32)
            m_sc[hf] = m_new

    if has_extra:
        pl.when(kj < n_main)(lambda: update(k_ref, v_ref))
        pl.when(kj == n_main)(lambda: update(kx_ref, vx_ref))
    else:
        update(k_ref, v_ref)

    @pl.when(kj == n_steps - 1)
    def _():
        _diff_finalize(lam_ref, g_ref, o_ref, acc_sc[0], acc_sc[1], l_sc[0], l_sc[1], out_scale)


def _diff_finalize(lam_ref, g_ref, o_ref, acc0, acc1, l0, l1, out_scale):
    lam = lam_ref[0, 0]
    o = acc0 * (1.0 / l0) - lam * (acc1 * (1.0 / l1))
    y = o * lax.rsqrt(jnp.mean(o * o, axis=-1, keepdims=True) + EPS)
    o_ref[...] = (y * g_ref[...] * out_scale).astype(o_ref.dtype)


def _diff_attn_bounded_kernel(lam_ref, q_ref, k_ref, v_ref, kx_ref, vx_ref, g_ref, o_ref, l_sc, acc_sc,
                              *, tk, n_main, out_scale):
    l_sc[...] = jnp.zeros(l_sc.shape, F32)
    acc_sc[...] = jnp.zeros(acc_sc.shape, F32)

    def accumulate(k, v):
        for hf in range(2):
            sl = slice(hf * HEAD_DIM, (hf + 1) * HEAD_DIM)
            s = lax.dot_general(q_ref[:, sl], k[:, sl], (((1,), (1,)), ((), ())), preferred_element_type=F32)
            p = jnp.exp2(s)
            part = p[:, 0:HEAD_DIM]
            for c in range(1, p.shape[1] // HEAD_DIM):
                part = part + p[:, c * HEAD_DIM:(c + 1) * HEAD_DIM]
            l_sc[hf] += part
            acc_sc[hf] += jnp.dot(p.astype(BF16), v, preferred_element_type=F32)

    def body(j, carry):
        off = pl.multiple_of(j * tk, tk)
        accumulate(k_ref[pl.ds(off, tk), :], v_ref[pl.ds(off, tk), :])
        return carry

    lax.fori_loop(0, n_main, body, 0)
    accumulate(kx_ref[...], vx_ref[...])
    _diff_finalize(lam_ref, g_ref, o_ref, acc_sc[0], acc_sc[1],
                   jnp.sum(l_sc[0], axis=-1, keepdims=True), jnp.sum(l_sc[1], axis=-1, keepdims=True), out_scale)


def diff_attention_bounded(q, k, v_src, v_col0, lam, sub_g, out_scale, extra):
    mq, bw = q.shape
    mk = k.shape[0]
    hw = 2 * HEAD_DIM
    tq, tk = _tile(mq, 1024), _tile(mk, 1024)
    kx, vx_src, vx_col0 = extra
    mx = kx.shape[0]
    once = pl.Buffered(1)
    return pl.pallas_call(
        functools.partial(_diff_attn_bounded_kernel, tk=tk, n_main=mk // tk, out_scale=out_scale),
        grid=(bw // hw, mq // tq),
        in_specs=[
            pl.BlockSpec(memory_space=pltpu.SMEM),
            pl.BlockSpec((tq, hw), lambda h, i: (i, h)),
            pl.BlockSpec((mk, hw), lambda h, i: (0, h), pipeline_mode=once),
            pl.BlockSpec((mk, hw), lambda h, i: (0, v_col0 + h), pipeline_mode=once),
            pl.BlockSpec((mx, hw), lambda h, i: (0, h)),
            pl.BlockSpec((mx, hw), lambda h, i: (0, vx_col0 + h)),
            pl.BlockSpec((1, hw), lambda h, i: (0, 0)),
        ],
        out_specs=pl.BlockSpec((tq, hw), lambda h, i: (i, h)),
        out_shape=jax.ShapeDtypeStruct((mq, bw), BF16),
        scratch_shapes=[pltpu.VMEM((2, tq, HEAD_DIM), F32), pltpu.VMEM((2, tq, hw), F32)],
        compiler_params=_cparams("arbitrary", "arbitrary"),
        name="diff_attention_bounded",
    )(lam.reshape(1, 1).astype(F32), q, k, v_src, kx, vx_src, sub_g.reshape(1, hw))


def score_bound_log2(g_q, g_k):
    return Q_SCALE * HEAD_DIM * jnp.max(jnp.abs(g_q)) * jnp.max(jnp.abs(g_k)) * 1.01


def diff_attention(q, k, v_src, v_col0, lam, sub_g, out_scale, extra=None):
    mq, bw = q.shape
    mk = k.shape[0]
    hw = 2 * HEAD_DIM
    heads = bw // hw
    tq, tk = _tile(mq, 1024), _tile(mk, 1024)
    n_main = mk // tk
    last = n_main - 1
    kmap = lambda h, i, j: (jnp.minimum(j, last), h)
    in_specs = [
        pl.BlockSpec(memory_space=pltpu.SMEM),
        pl.BlockSpec((tq, hw), lambda h, i, j: (i, h)),
        pl.BlockSpec((tk, hw), kmap),
        pl.BlockSpec((tk, hw), lambda h, i, j: (jnp.minimum(j, last), v_col0 + h)),
    ]
    args = [lam.reshape(1, 1).astype(F32), q, k, v_src]
    if extra is not None:
        kx, vx_src, vx_col0 = extra
        mx = kx.shape[0]
        in_specs += [pl.BlockSpec((mx, hw), lambda h, i, j: (0, h)),
                     pl.BlockSpec((mx, hw), lambda h, i, j: (0, vx_col0 + h))]
        args += [kx, vx_src]
    in_specs.append(pl.BlockSpec((1, hw), lambda h, i, j: (0, 0)))
    args.append(sub_g.reshape(1, hw))
    return pl.pallas_call(
        functools.partial(_diff_attn_kernel, n_main=n_main, has_extra=extra is not None, out_scale=out_scale),
        grid=(heads, mq // tq, n_main + (extra is not None)),
        in_specs=in_specs,
        out_specs=pl.BlockSpec((tq, hw), lambda h, i, j: (i, h)),
        out_shape=jax.ShapeDtypeStruct((mq, bw), BF16),
        scratch_shapes=[pltpu.VMEM((2, tq, 1), F32), pltpu.VMEM((2, tq, 1), F32), pltpu.VMEM((2, tq, hw), F32)],
        compiler_params=_cparams("arbitrary", "arbitrary", "arbitrary"),
        name="diff_attention",
    )(*args)


def _na_kernel(q_ref, k_ref, v_ref, kc_ref, vc_ref, bias_ref, o_ref, *, rows):
    b = pl.program_id(1)
    start_row = jnp.clip(b * NA_Q_ROWS - NA_WIN_R // 2, 0, rows - NA_K_ROWS)
    start = pl.multiple_of(start_row * GRID_W, GRID_W * (NA_WIN_R // 2))
    n_keys = NA_K_ROWS * GRID_W
    q = q_ref[...]
    k = k_ref[pl.ds(start, n_keys), :]
    v = v_ref[pl.ds(start, n_keys), :]
    nt = (((1,), (1,)), ((), ()))
    s_loc = lax.dot_general(q, k, nt, preferred_element_type=F32) + bias_ref[...]
    s_ctx = lax.dot_general(q, kc_ref[...], nt, preferred_element_type=F32)
    m = jnp.maximum(jnp.max(s_loc, axis=-1, keepdims=True), jnp.max(s_ctx, axis=-1, keepdims=True))
    p_loc = jnp.exp2(s_loc - m)
    p_ctx = jnp.exp2(s_ctx - m)
    denom = jnp.sum(p_loc, axis=-1, keepdims=True) + jnp.sum(p_ctx, axis=-1, keepdims=True)
    o = (jnp.dot(p_loc.astype(BF16), v, preferred_element_type=F32)
         + jnp.dot(p_ctx.astype(BF16), vc_ref[...], preferred_element_type=F32))
    o_ref[...] = (o * (1.0 / denom)).astype(o_ref.dtype)


def na_bias_tables(rpb, rows):
    n_blocks = rows // NA_Q_ROWS
    wr = min(NA_WIN_R, rows)
    w = jnp.arange(GRID_W)
    col_start = jnp.clip(w - NA_WIN_C // 2, 0, GRID_W - NA_WIN_C)
    kc = jnp.arange(GRID_W)
    valid_c = (kc[None, :] >= col_start[:, None]) & (kc[None, :] < col_start[:, None] + NA_WIN_C)
    rel_c = jnp.clip(kc[None, :] - w[:, None] + (NA_WIN_C - 1), 0, 2 * NA_WIN_C - 2)
    tables = []
    for blk in (0, 1, n_blocks - 1):
        r = blk * NA_Q_ROWS + jnp.arange(NA_Q_ROWS)
        start = jnp.clip(blk * NA_Q_ROWS - NA_WIN_R // 2, 0, rows - NA_K_ROWS)
        kr = start + jnp.arange(NA_K_ROWS)
        row_start = jnp.clip(r - NA_WIN_R // 2, 0, rows - wr)
        valid_r = (kr[None, :] >= row_start[:, None]) & (kr[None, :] < row_start[:, None] + wr)
        rel_r = jnp.clip(kr[None, :] - r[:, None] + (NA_WIN_R - 1), 0, 2 * NA_WIN_R - 2)
        bias = rpb[:, rel_r][:, :, :, rel_c]
        valid = valid_r[:, :, None, None] & valid_c[None, None]
        bias = jnp.where(valid[None], bias * LOG2E, MASKED)
        bias = jnp.transpose(bias, (0, 1, 3, 2, 4))
        tables.append(bias.reshape(rpb.shape[0], NA_Q_ROWS * GRID_W, NA_K_ROWS * GRID_W))
    return jnp.stack(tables, axis=1).astype(F32)


def neighbourhood_attention(q, k, v_src, v_col0, kc, vc_src, vc_col0, bias):
    s, bw = q.shape
    heads = bw // HEAD_DIM
    rows = s // GRID_W
    n_blocks = rows // NA_Q_ROWS
    assert n_blocks >= 3 and rows % NA_Q_ROWS == 0
    tq = NA_Q_ROWS * GRID_W
    lc = kc.shape[0]
    variant = lambda b: jnp.where(b == 0, 0, jnp.where(b == n_blocks - 1, 2, 1))
    return pl.pallas_call(
        functools.partial(_na_kernel, rows=rows),
        grid=(heads, n_blocks),
        in_specs=[
            pl.BlockSpec((tq, HEAD_DIM), lambda h, b: (b, h)),
            pl.BlockSpec((s, HEAD_DIM), lambda h, b: (0, h)),
            pl.BlockSpec((s, HEAD_DIM), lambda h, b: (0, v_col0 + h)),
            pl.BlockSpec((lc, HEAD_DIM), lambda h, b: (0, h)),
            pl.BlockSpec((lc, HEAD_DIM), lambda h, b: (0, vc_col0 + h)),
            pl.BlockSpec((None, None, tq, NA_K_ROWS * GRID_W), lambda h, b: (h, variant(b), 0, 0)),
        ],
        out_specs=pl.BlockSpec((tq, HEAD_DIM), lambda h, b: (b, h)),
        out_shape=jax.ShapeDtypeStruct((s, bw), BF16),
        compiler_params=_cparams("arbitrary", "arbitrary"),
        name="neighbourhood_attention",
    )(q, k, v_src, kc, vc_src, bias)


def _dense_attn_kernel(q_ref, k_ref, v_ref, o_ref):
    s = lax.dot_general(q_ref[...], k_ref[...], (((1,), (1,)), ((), ())), preferred_element_type=F32)
    p = jnp.exp2(s - jnp.max(s, axis=-1, keepdims=True))
    o = jnp.dot(p.astype(BF16), v_ref[...], preferred_element_type=F32)
    o_ref[...] = (o * (1.0 / jnp.sum(p, axis=-1, keepdims=True))).astype(o_ref.dtype)


def dense_attention(q, k, v_src, v_col0):
    m, bw = q.shape
    heads = bw // HEAD_DIM
    blk = lambda c0: pl.BlockSpec((m, HEAD_DIM), lambda h: (0, c0 + h))
    return pl.pallas_call(
        _dense_attn_kernel, grid=(heads,), in_specs=[blk(0), blk(0), blk(v_col0)], out_specs=blk(0),
        out_shape=jax.ShapeDtypeStruct((m, bw), BF16),
        compiler_params=_cparams("arbitrary"), name="dense_attention",
    )(q, k, v_src)


def _merge_up_kernel(od_ref, on_ref, cb_ref, cc_ref, cx_ref, ccp_ref, cxp_ref, ccn_ref, cxn_ref, cw_ref,
                     g0_ref, g1_ref, g2_ref, wd_ref, wn_ref, wc_ref, o_ref, oc_sc):
    i, j = pl.program_id(0), pl.program_id(1)

    @pl.when(j == 0)
    def _():
        u = cc_ref[...].astype(F32) * cx_ref[...].astype(F32)
        tm = u.shape[0]
        halo = ccp_ref.shape[0]
        prev_row = ccp_ref[halo - 1:halo, :].astype(F32) * cxp_ref[halo - 1:halo, :].astype(F32)
        next_row = ccn_ref[0:1, :].astype(F32) * cxn_ref[0:1, :].astype(F32)
        prev_row = jnp.where(i == 0, 0.0, prev_row)
        next_row = jnp.where(i == pl.num_programs(0) - 1, 0.0, next_row)
        r = lax.broadcasted_iota(jnp.int32, u.shape, 0)
        u_prev = jnp.where(r == 0, prev_row, pltpu.roll(u, 1, axis=0))
        u_next = jnp.where(r == tm - 1, next_row, pltpu.roll(u, tm - 1, axis=0))
        conv = cw_ref[0:1, :] * u_prev + cw_ref[1:2, :] * u + cw_ref[2:3, :] * u_next
        oc_sc[...] = (cb_ref[...].astype(F32) * conv).astype(BF16)

    acc = _sigmoid(g0_ref[...].astype(F32)) * jnp.dot(od_ref[...], wd_ref[...], preferred_element_type=F32)
    acc += _sigmoid(g1_ref[...].astype(F32)) * jnp.dot(on_ref[...], wn_ref[...], preferred_element_type=F32)
    acc += _sigmoid(g2_ref[...].astype(F32)) * jnp.dot(oc_sc[...], wc_ref[...], preferred_element_type=F32)
    o_ref[...] = acc.astype(o_ref.dtype)


def merge_up(o_diff, o_na, p, conv_w, w_up_diff, w_up_na, w_up_conv, layer, d):
    m, bw = o_diff.shape
    tm, tn = _tile(m, 512), math.gcd(_tile(d, 1024), bw)
    halo = 8
    n_halo = m // halo
    row_blk = lambda c: pl.BlockSpec((tm, bw), lambda i, j, c=c: (i, c))
    prev_blk = lambda c: pl.BlockSpec((halo, bw), lambda i, j, c=c: (jnp.maximum(i * (tm // halo) - 1, 0), c))
    next_blk = lambda c: pl.BlockSpec((halo, bw), lambda i, j, c=c: (jnp.minimum((i + 1) * (tm // halo), n_halo - 1), c))
    gate0 = 9 * bw // tn
    gate_blk = lambda b: pl.BlockSpec((tm, tn), lambda i, j, b=b: (i, gate0 + b * (d // tn) + j))
    w_blk = pl.BlockSpec((None, bw, tn), lambda i, j: (layer, 0, j))
    assert (9 * bw) % tn == 0
    return pl.pallas_call(
        _merge_up_kernel,
        grid=(m // tm, d // tn),
        in_specs=[row_blk(0), row_blk(0), row_blk(6), row_blk(7), row_blk(8),
                  prev_blk(7), prev_blk(8), next_blk(7), next_blk(8),
                  pl.BlockSpec((CONV_K, bw), lambda i, j: (0, 0)),
                  gate_blk(0), gate_blk(1), gate_blk(2), w_blk, w_blk, w_blk],
        out_specs=pl.BlockSpec((tm, tn), lambda i, j: (i, j)),
        out_shape=jax.ShapeDtypeStruct((m, d), BF16),
        scratch_shapes=[pltpu.VMEM((tm, bw), BF16)],
        compiler_params=_cparams("arbitrary", "arbitrary"),
        name="merge_up",
    )(o_diff, o_na, p, p, p, p, p, p, p, conv_w, p, p, p, w_up_diff, w_up_na, w_up_conv)


def kernel(x, c, ctx, c_ctx, w_mod_a, w_mod_b, b_mod, norm1_g, w_in, diff_qn_g, diff_kn_g, lambda_q1, lambda_k1, lambda_q2, lambda_k2, diff_subln_g, na_qn_g, na_kn_g, na_rpb, conv_w, w_up_diff, w_up_na, w_up_conv, w_o, norm2_g, router_w, router_b, w_gu, b_gu, w_down, b_down):
    batch, s, d = x.shape
    assert batch == 1 and ctx.shape[0] == 1
    depth = w_in.shape[0]
    bw = w_up_diff.shape[1]
    dv0 = 2 * bw // (2 * HEAD_DIM)
    nv0 = 5 * bw // HEAD_DIM
    rows = s // GRID_W

    cond = jnp.zeros((8, d), F32).at[0].set(c[0]).at[1].set(c_ctx)
    mod = modulation(cond, w_mod_a, w_mod_b, b_mod).reshape(depth, 8, N_MOD, d)
    rope = rope_tables(s)
    w_in, w_o, w_gu, w_down = (w.astype(BF16) for w in (w_in, w_o, w_gu, w_down))
    ups = tuple(w.astype(BF16) for w in (w_up_diff, w_up_na, w_up_conv))

    xl, xc = x[0], ctx[0]
    for i in range(depth):
        last = i == depth - 1
        lam_init = 0.8 - 0.6 * math.exp(-0.3 * i)
        lam = (jnp.exp(jnp.sum(lambda_q1[i] * lambda_k1[i])) - jnp.exp(jnp.sum(lambda_q2[i] * lambda_k2[i]))
               + lam_init)
        ml, mc = mod[i, 0], mod[i, 1]
        gains = jnp.stack([diff_qn_g[i], diff_kn_g[i], na_qn_g[i], na_kn_g[i]])

        p_l = project_in(prenorm(xl, norm1_g[i], ml[1], ml[0]), w_in, i)
        p_c = project_in(prenorm(xc, norm1_g[i], mc[1], mc[0]), w_in, i)
        dq_l, dk_l, nq_l, nk_l = qk_prep(p_l, gains, rope, bw)
        dq_c, dk_c, nq_c, nk_c = qk_prep(p_c, gains, None, bw)

        diff_args = (dq_l, dk_l, p_l, dv0, lam, diff_subln_g[i], 1.0 - lam_init, (dk_c, p_c, dv0))
        o_diff_l = lax.cond(score_bound_log2(diff_qn_g[i], diff_kn_g[i]) <= SCORE_BOUND_LOG2,
                            lambda: diff_attention_bounded(*diff_args), lambda: diff_attention(*diff_args))
        o_na_l = neighbourhood_attention(nq_l, nk_l, p_l, nv0, nk_c, p_c, nv0,
                                         na_bias_tables(na_rpb[i], rows))
        xl = project_out(merge_up(o_diff_l, o_na_l, p_l, conv_w[i], *ups, i, d), w_o, i, xl, ml[2])
        if not last:
            o_diff_c = diff_attention(dq_c, dk_c, p_c, dv0, lam, diff_subln_g[i], 1.0 - lam_init)
            o_na_c = dense_attention(nq_c, nk_c, p_c, nv0)
            xc = project_out(merge_up(o_diff_c, o_na_c, p_c, conv_w[i], *ups, i, d), w_o, i, xc, mc[2])

        router = (router_w[i], router_b[i])
        h_l, comb_l = prenorm(xl, norm2_g[i], ml[4], ml[3], router)
        xl = moe(h_l, comb_l, w_gu, b_gu[i], w_down, b_down[i], i, xl, ml[5])
        if not last:
            h_c, comb_c = prenorm(xc, norm2_g[i], mc[4], mc[3], router)
            xc = moe(h_c, comb_c, w_gu, b_gu[i], w_down, b_down[i], i, xc, mc[5])
    return xl[None]
```

```python
import functools
import math

import jax
import jax.numpy as jnp
from jax import lax
from jax.experimental import pallas as pl
from jax.experimental.pallas import tpu as pltpu

F32 = jnp.float32
BF16 = jnp.bfloat16
HIGHEST = lax.Precision.HIGHEST

GRID_W = 64
HEAD_DIM = 128
NA_WIN_R = 8
NA_WIN_C = 16
CONV_K = 3
N_BRANCH = 3
N_MOD = 6
TOP_K = 4
SWIGLU_LIMIT = 7.0
SWIGLU_ALPHA = 1.702
ROPE_BASE = 10000.0
EPS = 1e-6
LOG2E = math.log2(math.e)
Q_SCALE = HEAD_DIM ** -0.5 * LOG2E
MASKED = -1e30
SCORE_BOUND_LOG2 = 60.0

V7X_VMEM_LIMIT_BYTES = 56 * 1024 * 1024
NA_Q_ROWS = 8
NA_K_ROWS = NA_Q_ROWS + NA_WIN_R


def _cparams(*sem):
    return pltpu.CompilerParams(dimension_semantics=sem, vmem_limit_bytes=V7X_VMEM_LIMIT_BYTES)


def _tile(n, target):
    if n <= target:
        return n
    t = target
    while n % t:
        t -= 8
    assert t > 0, (n, target)
    return t


def _sigmoid(x):
    return 1.0 / (1.0 + jnp.exp(-x))


def _mod_kernel(c_ref, wa_ref, wb_ref, b_ref, o_ref):
    c = c_ref[...]
    s = c * _sigmoid(c)
    m1 = jnp.dot(s, wa_ref[...], precision=HIGHEST, preferred_element_type=F32)
    o_ref[...] = jnp.dot(m1, wb_ref[...], precision=HIGHEST, preferred_element_type=F32) + b_ref[...]


def modulation(cond, w_a, w_b, b):
    depth, d, r = w_a.shape
    width = w_b.shape[-1]
    rows = cond.shape[0]
    tn = _tile(width, 4096)
    return pl.pallas_call(
        _mod_kernel,
        grid=(depth, width // tn),
        in_specs=[
            pl.BlockSpec((rows, d), lambda l, j: (0, 0)),
            pl.BlockSpec((None, d, r), lambda l, j: (l, 0, 0)),
            pl.BlockSpec((None, r, tn), lambda l, j: (l, 0, j)),
            pl.BlockSpec((None, 1, tn), lambda l, j: (l, 0, j)),
        ],
        out_specs=pl.BlockSpec((None, rows, tn), lambda l, j: (l, 0, j)),
        out_shape=jax.ShapeDtypeStruct((depth, rows, width), F32),
        compiler_params=_cparams("arbitrary", "arbitrary"),
        name="modulation",
    )(cond, w_a, w_b, b.reshape(depth, 1, width))


def _modulated_norm(x_ref, g_ref, sc_ref, sh_ref):
    x = x_ref[...]
    y = x * lax.rsqrt(jnp.mean(x * x, axis=-1, keepdims=True) + EPS)
    return (y * g_ref[...]) * (1.0 + sc_ref[...]) + sh_ref[...]


def _prenorm_kernel(x_ref, g_ref, sc_ref, sh_ref, o_ref):
    o_ref[...] = _modulated_norm(x_ref, g_ref, sc_ref, sh_ref).astype(o_ref.dtype)


def _route(h, rw_ref, rb_ref):
    logits = jnp.dot(h, rw_ref[...], precision=HIGHEST, preferred_element_type=F32) + rb_ref[...]
    n_exp = logits.shape[-1]
    lane = lax.broadcasted_iota(jnp.int32, logits.shape, 1).astype(F32)
    sels, ids, vals = [], [], []
    for _ in range(TOP_K):
        m = jnp.max(logits, axis=-1, keepdims=True)
        first = jnp.min(jnp.where(logits == m, lane, float(n_exp)), axis=-1, keepdims=True)
        sel = lane == first
        sels.append(sel)
        ids.append(first)
        vals.append(m)
        logits = jnp.where(sel, -jnp.inf, logits)
    es = [jnp.exp(v - vals[0]) for v in vals]
    inv = 1.0 / functools.reduce(lambda a, b: a + b, es)
    return sels, ids, [e * inv for e in es]


def _prenorm_router_kernel(x_ref, g_ref, sc_ref, sh_ref, rw_ref, rb_ref, o_ref, comb_ref):
    h = _modulated_norm(x_ref, g_ref, sc_ref, sh_ref)
    o_ref[...] = h.astype(o_ref.dtype)
    sels, _, wts = _route(h, rw_ref, rb_ref)
    comb = jnp.zeros(comb_ref.shape, F32)
    for sel, w in zip(sels, wts):
        comb = comb + jnp.where(sel, w, 0.0)
    comb_ref[...] = comb


def _columns(cols, width):
    lane = lax.broadcasted_iota(jnp.int32, (cols[0].shape[0], width), 1)
    out = jnp.zeros(lane.shape, cols[0].dtype)
    for k, c in enumerate(cols):
        out = jnp.where(lane == k, c, out)
    return out


def _prenorm_route_kernel(x_ref, g_ref, sc_ref, sh_ref, rw_ref, rb_ref,
                          o_ref, eid_ref, ew_ref, rank_ref, cnt_ref, carry_sc):
    @pl.when(pl.program_id(0) == 0)
    def _():
        carry_sc[...] = jnp.zeros(carry_sc.shape, F32)

    h = _modulated_norm(x_ref, g_ref, sc_ref, sh_ref)
    o_ref[...] = h.astype(o_ref.dtype)
    sels, ids, wts = _route(h, rw_ref, rb_ref)
    tm = h.shape[0]
    chosen = functools.reduce(lambda a, b: a + b, [jnp.where(s, 1.0, 0.0) for s in sels])
    r = lax.broadcasted_iota(jnp.int32, (tm, tm), 0)
    c = lax.broadcasted_iota(jnp.int32, (tm, tm), 1)
    earlier = jnp.where(c < r, 1.0, 0.0).astype(BF16)
    before = jnp.dot(earlier, chosen.astype(BF16), preferred_element_type=F32) + carry_sc[...]
    ranks = [jnp.sum(jnp.where(s, before, 0.0), axis=-1, keepdims=True) for s in sels]
    carry_sc[...] += jnp.sum(chosen, axis=0, keepdims=True)
    eid_ref[...] = _columns(ids, TOP_K).astype(jnp.int32)
    ew_ref[...] = _columns(wts, TOP_K)
    rank_ref[...] = _columns(ranks, TOP_K).astype(jnp.int32)
    cnt_ref[...] = carry_sc[...].astype(jnp.int32)


def prenorm(x, g, scale, shift, router=None):
    m, d = x.shape
    tm = _tile(m, 256)
    row = lambda i: (i, 0)
    fixed = lambda i: (0, 0)
    vec = pl.BlockSpec((1, d), fixed)
    in_specs = [pl.BlockSpec((tm, d), row), vec, vec, vec]
    args = [x, g.reshape(1, d), scale.reshape(1, d), shift.reshape(1, d)]
    if router is None:
        return pl.pallas_call(
            _prenorm_kernel, grid=(m // tm,), in_specs=in_specs,
            out_specs=pl.BlockSpec((tm, d), row),
            out_shape=jax.ShapeDtypeStruct((m, d), BF16),
            compiler_params=_cparams("arbitrary"), name="prenorm",
        )(*args)
    rw, rb = router
    n_exp = rw.shape[-1]
    in_specs += [pl.BlockSpec((d, n_exp), fixed), pl.BlockSpec((1, n_exp), fixed)]
    args += [rw, rb.reshape(1, n_exp)]
    return pl.pallas_call(
        _prenorm_router_kernel, grid=(m // tm,), in_specs=in_specs,
        out_specs=[pl.BlockSpec((tm, d), row), pl.BlockSpec((tm, n_exp), row)],
        out_shape=[jax.ShapeDtypeStruct((m, d), BF16), jax.ShapeDtypeStruct((m, n_exp), F32)],
        compiler_params=_cparams("arbitrary"), name="prenorm_router",
    )(*args)


def prenorm_route(x, g, scale, shift, router):
    m, d = x.shape
    tm = _tile(m, 256)
    rw, rb = router
    n_exp = rw.shape[-1]
    row = lambda i: (i, 0)
    fixed = lambda i: (0, 0)
    vec = pl.BlockSpec((1, d), fixed)
    per_tok = pl.BlockSpec((tm, TOP_K), row)
    return pl.pallas_call(
        _prenorm_route_kernel, grid=(m // tm,),
        in_specs=[pl.BlockSpec((tm, d), row), vec, vec, vec,
                  pl.BlockSpec((d, n_exp), fixed), pl.BlockSpec((1, n_exp), fixed)],
        out_specs=[pl.BlockSpec((tm, d), row), per_tok, per_tok, per_tok, pl.BlockSpec((1, n_exp), fixed)],
        out_shape=[jax.ShapeDtypeStruct((m, d), BF16), jax.ShapeDtypeStruct((m, TOP_K), jnp.int32),
                   jax.ShapeDtypeStruct((m, TOP_K), F32), jax.ShapeDtypeStruct((m, TOP_K), jnp.int32),
                   jax.ShapeDtypeStruct((1, n_exp), jnp.int32)],
        scratch_shapes=[pltpu.VMEM((1, n_exp), F32)],
        compiler_params=_cparams("arbitrary"), name="prenorm_route",
    )(x, g.reshape(1, d), scale.reshape(1, d), shift.reshape(1, d), rw, rb.reshape(1, n_exp))


EXPERT_TILE = 256


def _expert_mlp_kernel(te_ref, nt_ref, x_ref, wgu_ref, bgu_ref, wd_ref, bd_ref, o_ref):
    used = pl.program_id(0) < nt_ref[0]

    @pl.when(jnp.logical_not(used))
    def _():
        o_ref[...] = jnp.zeros(o_ref.shape, o_ref.dtype)

    @pl.when(used)
    def _():
        d_exp = wd_ref.shape[0]
        gu = jnp.dot(x_ref[...], wgu_ref[...], preferred_element_type=F32) + bgu_ref[...]
        gate = jnp.minimum(gu[:, :d_exp], SWIGLU_LIMIT)
        lin = jnp.clip(gu[:, d_exp:], -SWIGLU_LIMIT, SWIGLU_LIMIT)
        act = gate * _sigmoid(SWIGLU_ALPHA * gate) * (lin + 1.0)
        y = jnp.dot(act.astype(BF16), wd_ref[...], preferred_element_type=F32) + bd_ref[...]
        o_ref[...] = y.astype(o_ref.dtype)


def expert_mlp(xg, tile_expert, n_tiles_used, w_gu, b_gu, w_down, b_down, layer):
    p_rows, d = xg.shape
    _, n_exp, _, two_de = w_gu.shape
    d_exp = two_de // 2
    t = EXPERT_TILE
    grid_spec = pltpu.PrefetchScalarGridSpec(
        num_scalar_prefetch=2,
        grid=(p_rows // t,),
        in_specs=[
            pl.BlockSpec((t, d), lambda j, te, nt: (j, 0)),
            pl.BlockSpec((None, None, d, two_de), lambda j, te, nt: (layer, te[j], 0, 0)),
            pl.BlockSpec((None, 1, two_de), lambda j, te, nt: (te[j], 0, 0)),
            pl.BlockSpec((None, None, d_exp, d), lambda j, te, nt: (layer, te[j], 0, 0)),
            pl.BlockSpec((None, 1, d), lambda j, te, nt: (te[j], 0, 0)),
        ],
        out_specs=pl.BlockSpec((t, d), lambda j, te, nt: (j, 0)),
    )
    return pl.pallas_call(
        _expert_mlp_kernel, grid_spec=grid_spec,
        out_shape=jax.ShapeDtypeStruct((p_rows, d), BF16),
        compiler_params=_cparams("arbitrary"), name="expert_mlp",
    )(tile_expert, n_tiles_used, xg, w_gu, b_gu.reshape(n_exp, 1, two_de), w_down, b_down.reshape(n_exp, 1, d))


def _combine_kernel(x_ref, gate_ref, ew_ref, *rest):
    y_refs, o_ref = rest[:-1], rest[-1]
    ew = ew_ref[...]
    acc = jnp.zeros(o_ref.shape, F32)
    for k, y_ref in enumerate(y_refs):
        acc = acc + ew[:, k:k + 1] * y_ref[...].astype(F32)
    o_ref[...] = x_ref[...] + gate_ref[...] * acc


def moe_combine(x, gate, ew, y4):
    m, d = x.shape
    tm, tn = _tile(m, 512), _tile(d, 1024)
    tile = pl.BlockSpec((tm, tn), lambda i, j: (i, j))
    y_blk = lambda k: pl.BlockSpec((None, tm, tn), lambda i, j, k=k: (k, i, j))
    return pl.pallas_call(
        _combine_kernel, grid=(m // tm, d // tn),
        in_specs=[tile, pl.BlockSpec((1, tn), lambda i, j: (0, j)), pl.BlockSpec((tm, TOP_K), lambda i, j: (i, 0))]
        + [y_blk(k) for k in range(TOP_K)],
        out_specs=tile, out_shape=jax.ShapeDtypeStruct((m, d), F32),
        compiler_params=_cparams("arbitrary", "arbitrary"), name="moe_combine",
    )(x, gate.reshape(1, d), ew, *([y4] * TOP_K))


def moe_routed(h, eid, ew, rank, counts, w_gu, b_gu, w_down, b_down, layer, x, gate):
    m, d = h.shape
    n_exp = counts.shape[-1]
    t = EXPERT_TILE
    p_rows = m * TOP_K + n_exp * t
    counts = counts.reshape(n_exp)
    tiles_per = (counts + t - 1) // t
    tile_end = jnp.cumsum(tiles_per)
    start = (tile_end - tiles_per) * t
    pos = start[eid] + rank
    tile_expert = jnp.minimum(jnp.searchsorted(tile_end, jnp.arange(p_rows // t), side="right"),
                              n_exp - 1).astype(jnp.int32)
    n_used = tile_end[-1:].astype(jnp.int32)
    pos_t = pos.T.reshape(-1)
    src = jnp.zeros((p_rows,), jnp.int32).at[pos_t].set(jnp.tile(jnp.arange(m, dtype=jnp.int32), TOP_K))
    xg = jnp.take(h, src, axis=0)
    yg = expert_mlp(xg, tile_expert, n_used, w_gu, b_gu, w_down, b_down, layer)
    y4 = jnp.take(yg, pos_t, axis=0).reshape(TOP_K, m, d)
    return moe_combine(x, gate, ew, y4)


def _mm_kernel(*refs, epilogue):
    a_ref, w_ref = refs[0], refs[1]
    extras, o_ref = refs[2:-1], refs[-1]
    acc = jnp.dot(a_ref[...], w_ref[...], preferred_element_type=F32)
    o_ref[...] = epilogue(acc, *extras).astype(o_ref.dtype)


def matmul(a, w, w_spec, tm, tn, n_col_tiles, extras, epilogue, out_spec, out_shape, name):
    m, k = a.shape
    return pl.pallas_call(
        functools.partial(_mm_kernel, epilogue=epilogue),
        grid=(m // tm, n_col_tiles),
        in_specs=[pl.BlockSpec((tm, k), lambda i, j: (i, 0)), w_spec] + [s for _, s in extras],
        out_specs=out_spec,
        out_shape=out_shape,
        compiler_params=_cparams("arbitrary", "arbitrary"),
        name=name,
    )(a, w, *[x for x, _ in extras])


def _cast_epilogue(acc):
    return acc


def _residual_epilogue(acc, x_ref, gate_ref):
    return x_ref[...] + gate_ref[...] * acc


def _swiglu_epilogue(acc, b_ref, comb_ref):
    d_exp = acc.shape[-1] // 2
    gu = acc + b_ref[...]
    gate = jnp.minimum(gu[:, :d_exp], SWIGLU_LIMIT)
    lin = jnp.clip(gu[:, d_exp:], -SWIGLU_LIMIT, SWIGLU_LIMIT)
    act = gate * _sigmoid(SWIGLU_ALPHA * gate) * (lin + 1.0)
    comb = comb_ref[...]
    lane = lax.broadcasted_iota(jnp.int32, comb.shape, 1)
    cw = jnp.sum(jnp.where(lane == pl.program_id(1), comb, 0.0), axis=-1, keepdims=True)
    return act * cw


def _moe_down_epilogue(acc, comb_ref, bd_ref, x_ref, gate_ref):
    bias = jnp.dot(comb_ref[...].astype(BF16), bd_ref[...].astype(BF16), preferred_element_type=F32)
    return x_ref[...] + gate_ref[...] * (acc + bias)


def project_in(h, w_in, layer):
    m, d = h.shape
    width = w_in.shape[-1]
    tm, tn = _tile(m, 1024), _tile(width, 1024)
    return matmul(h, w_in, pl.BlockSpec((None, d, tn), lambda i, j: (layer, 0, j)), tm, tn, width // tn, [],
                  _cast_epilogue, pl.BlockSpec((tm, tn), lambda i, j: (i, j)),
                  jax.ShapeDtypeStruct((m, width), BF16), "project_in")


def project_out(merged, w_o, layer, x, gate):
    m, d = x.shape
    tm, tn = _tile(m, 1024), _tile(d, 512)
    tile = pl.BlockSpec((tm, tn), lambda i, j: (i, j))
    return matmul(merged, w_o, pl.BlockSpec((None, merged.shape[1], tn), lambda i, j: (layer, 0, j)),
                  tm, tn, d // tn,
                  [(x, tile), (gate.reshape(1, d), pl.BlockSpec((1, tn), lambda i, j: (0, j)))],
                  _residual_epilogue, tile, jax.ShapeDtypeStruct((m, d), F32), "project_out")


def moe(h, comb, w_gu, b_gu, w_down, b_down, layer, x, gate):
    m, d = h.shape
    depth, n_exp, _, two_de = w_gu.shape
    d_exp = two_de // 2
    tm = _tile(m, 1024)
    act = matmul(
        h, w_gu, pl.BlockSpec((None, None, d, two_de), lambda i, j: (layer, j, 0, 0)), tm, two_de, n_exp,
        [(b_gu.reshape(n_exp, 1, two_de), pl.BlockSpec((None, 1, two_de), lambda i, j: (j, 0, 0))),
         (comb, pl.BlockSpec((tm, n_exp), lambda i, j: (i, 0)))],
        _swiglu_epilogue, pl.BlockSpec((tm, d_exp), lambda i, j: (i, j)),
        jax.ShapeDtypeStruct((m, n_exp * d_exp), BF16), "moe_up")
    tm, tn = _tile(m, 512), _tile(d, 512)
    tile = pl.BlockSpec((tm, tn), lambda i, j: (i, j))
    return matmul(
        act, w_down.reshape(depth, n_exp * d_exp, d),
        pl.BlockSpec((None, n_exp * d_exp, tn), lambda i, j: (layer, 0, j)), tm, tn, d // tn,
        [(comb, pl.BlockSpec((tm, n_exp), lambda i, j: (i, 0))),
         (b_down, pl.BlockSpec((n_exp, tn), lambda i, j: (0, j))),
         (x, tile), (gate.reshape(1, d), pl.BlockSpec((1, tn), lambda i, j: (0, j)))],
        _moe_down_epilogue, tile, jax.ShapeDtypeStruct((m, d), F32), "moe_down")


def _chunk_norm(x, g):
    return x * lax.rsqrt(jnp.mean(x * x, axis=-1, keepdims=True) + EPS) * g


def _qk_prep_kernel(*refs, rope):
    if rope:
        dq_ref, dk_ref, nq_ref, nk_ref, g_ref, cos_ref, sin_ref = refs[:7]
        cos, sin = cos_ref[...], sin_ref[...]
        lane = lax.broadcasted_iota(jnp.int32, cos.shape, 1)
        lower = (lane & (HEAD_DIM // 4)) == 0
    else:
        dq_ref, dk_ref, nq_ref, nk_ref, g_ref = refs[:5]
    outs = refs[-4:]
    ins = (dq_ref, dk_ref, nq_ref, nk_ref)
    for idx in range(4):
        g = g_ref[idx:idx + 1, :]
        rot = rope and idx < 2
        scale = Q_SCALE if idx in (0, 2) else 1.0
        for c in range(ins[idx].shape[-1] // HEAD_DIM):
            sl = slice(c * HEAD_DIM, (c + 1) * HEAD_DIM)
            y = _chunk_norm(ins[idx][:, sl].astype(F32), g)
            if rot:
                swapped = jnp.where(lower, pltpu.roll(y, HEAD_DIM - HEAD_DIM // 4, axis=1),
                                    pltpu.roll(y, HEAD_DIM // 4, axis=1))
                y = y * cos + swapped * sin
            outs[idx][:, sl] = (y * scale).astype(BF16)


def qk_prep(p, gains, rope, bw):
    m = p.shape[0]
    tm = _tile(m, 512)
    col = lambda c: pl.BlockSpec((tm, bw), lambda i, c=c: (i, c))
    in_specs = [col(0), col(1), col(3), col(4), pl.BlockSpec((4, HEAD_DIM), lambda i: (0, 0))]
    args = [p, p, p, p, gains]
    if rope is not None:
        in_specs += [pl.BlockSpec((tm, HEAD_DIM), lambda i: (i, 0))] * 2
        args += list(rope)
    return pl.pallas_call(
        functools.partial(_qk_prep_kernel, rope=rope is not None),
        grid=(m // tm,), in_specs=in_specs,
        out_specs=[pl.BlockSpec((tm, bw), lambda i: (i, 0))] * 4,
        out_shape=[jax.ShapeDtypeStruct((m, bw), BF16)] * 4,
        compiler_params=_cparams("arbitrary"), name="qk_prep",
    )(*args)


def rope_tables(n):
    t = jnp.arange(n, dtype=jnp.int32)
    row = (t // GRID_W).astype(F32)
    col = (t % GRID_W).astype(F32)
    n_freq = HEAD_DIM // 4
    inv = ROPE_BASE ** (-jnp.arange(n_freq, dtype=F32) / n_freq)
    ar, ac = row[:, None] * inv, col[:, None] * inv
    ang = jnp.concatenate([ar, ar, ac, ac], axis=-1)
    lane = jnp.arange(HEAD_DIM)
    sign = jnp.where((lane & n_freq) == 0, -1.0, 1.0).astype(F32)
    return jnp.cos(ang), jnp.sin(ang) * sign


def _diff_attn_kernel(lam_ref, q_ref, k_ref, v_ref, *rest, n_main, has_extra, out_scale):
    if has_extra:
        kx_ref, vx_ref, g_ref, o_ref, m_sc, l_sc, acc_sc = rest
    else:
        g_ref, o_ref, m_sc, l_sc, acc_sc = rest
    kj = pl.program_id(2)
    n_steps = n_main + (1 if has_extra else 0)

    @pl.when(kj == 0)
    def _():
        m_sc[...] = jnp.full(m_sc.shape, -jnp.inf, F32)
        l_sc[...] = jnp.zeros(l_sc.shape, F32)
        acc_sc[...] = jnp.zeros(acc_sc.shape, F32)

    def update(kr, vr):
        v = vr[...]
        for hf in range(2):
            sl = slice(hf * HEAD_DIM, (hf + 1) * HEAD_DIM)
            s = lax.dot_general(q_ref[:, sl], kr[:, sl], (((1,), (1,)), ((), ())),
                                preferred_element_type=F32)
            m_prev = m_sc[hf]
            m_new = jnp.maximum(m_prev, jnp.max(s, axis=-1, keepdims=True))
            alpha = jnp.exp2(m_prev - m_new)
            p = jnp.exp2(s - m_new)
            l_sc[hf] = alpha * l_sc[hf] + jnp.sum(p, axis=-1, keepdims=True)
            acc_sc[hf] = alpha * acc_sc[hf] + jnp.dot(p.astype(BF16), v, preferred_element_type=F32)
            m_sc[hf] = m_new

    if has_extra:
        pl.when(kj < n_main)(lambda: update(k_ref, v_ref))
        pl.when(kj == n_main)(lambda: update(kx_ref, vx_ref))
    else:
        update(k_ref, v_ref)

    @pl.when(kj == n_steps - 1)
    def _():
        _diff_finalize(lam_ref, g_ref, o_ref, acc_sc[0], acc_sc[1], l_sc[0], l_sc[1], out_scale)


def _diff_finalize(lam_ref, g_ref, o_ref, acc0, acc1, l0, l1, out_scale):
    lam = lam_ref[0, 0]
    o = acc0 * (1.0 / l0) - lam * (acc1 * (1.0 / l1))
    y = o * lax.rsqrt(jnp.mean(o * o, axis=-1, keepdims=True) + EPS)
    o_ref[...] = (y * g_ref[...] * out_scale).astype(o_ref.dtype)


def _diff_attn_bounded_kernel(lam_ref, q_ref, k_ref, v_ref, kx_ref, vx_ref, g_ref, o_ref, l_sc, acc_sc,
                              *, tk, n_main, out_scale):
    l_sc[...] = jnp.zeros(l_sc.shape, F32)
    acc_sc[...] = jnp.zeros(acc_sc.shape, F32)

    def accumulate(k, v):
        for hf in range(2):
            sl = slice(hf * HEAD_DIM, (hf + 1) * HEAD_DIM)
            s = lax.dot_general(q_ref[:, sl], k[:, sl], (((1,), (1,)), ((), ())), preferred_element_type=F32)
            p = jnp.exp2(s)
            part = p[:, 0:HEAD_DIM]
            for c in range(1, p.shape[1] // HEAD_DIM):
                part = part + p[:, c * HEAD_DIM:(c + 1) * HEAD_DIM]
            l_sc[hf] += part
            acc_sc[hf] += jnp.dot(p.astype(BF16), v, preferred_element_type=F32)

    def body(j, carry):
        off = pl.multiple_of(j * tk, tk)
        accumulate(k_ref[pl.ds(off, tk), :], v_ref[pl.ds(off, tk), :])
        return carry

    lax.fori_loop(0, n_main, body, 0)
    accumulate(kx_ref[...], vx_ref[...])
    _diff_finalize(lam_ref, g_ref, o_ref, acc_sc[0], acc_sc[1],
                   jnp.sum(l_sc[0], axis=-1, keepdims=True), jnp.sum(l_sc[1], axis=-1, keepdims=True), out_scale)


def diff_attention_bounded(q, k, v_src, v_col0, lam, sub_g, out_scale, extra):
    mq, bw = q.shape
    mk = k.shape[0]
    hw = 2 * HEAD_DIM
    tq, tk = _tile(mq, 1024), _tile(mk, 1024)
    kx, vx_src, vx_col0 = extra
    mx = kx.shape[0]
    once = pl.Buffered(1)
    return pl.pallas_call(
        functools.partial(_diff_attn_bounded_kernel, tk=tk, n_main=mk // tk, out_scale=out_scale),
        grid=(bw // hw, mq // tq),
        in_specs=[
            pl.BlockSpec(memory_space=pltpu.SMEM),
            pl.BlockSpec((tq, hw), lambda h, i: (i, h)),
            pl.BlockSpec((mk, hw), lambda h, i: (0, h), pipeline_mode=once),
            pl.BlockSpec((mk, hw), lambda h, i: (0, v_col0 + h), pipeline_mode=once),
            pl.BlockSpec((mx, hw), lambda h, i: (0, h)),
            pl.BlockSpec((mx, hw), lambda h, i: (0, vx_col0 + h)),
            pl.BlockSpec((1, hw), lambda h, i: (0, 0)),
        ],
        out_specs=pl.BlockSpec((tq, hw), lambda h, i: (i, h)),
        out_shape=jax.ShapeDtypeStruct((mq, bw), BF16),
        scratch_shapes=[pltpu.VMEM((2, tq, HEAD_DIM), F32), pltpu.VMEM((2, tq, hw), F32)],
        compiler_params=_cparams("arbitrary", "arbitrary"),
        name="diff_attention_bounded",
    )(lam.reshape(1, 1).astype(F32), q, k, v_src, kx, vx_src, sub_g.reshape(1, hw))


def score_bound_log2(g_q, g_k):
    return Q_SCALE * HEAD_DIM * jnp.max(jnp.abs(g_q)) * jnp.max(jnp.abs(g_k)) * 1.01


def diff_attention(q, k, v_src, v_col0, lam, sub_g, out_scale, extra=None):
    mq, bw = q.shape
    mk = k.shape[0]
    hw = 2 * HEAD_DIM
    heads = bw // hw
    tq, tk = _tile(mq, 1024), _tile(mk, 1024)
    n_main = mk // tk
    last = n_main - 1
    kmap = lambda h, i, j: (jnp.minimum(j, last), h)
    in_specs = [
        pl.BlockSpec(memory_space=pltpu.SMEM),
        pl.BlockSpec((tq, hw), lambda h, i, j: (i, h)),
        pl.BlockSpec((tk, hw), kmap),
        pl.BlockSpec((tk, hw), lambda h, i, j: (jnp.minimum(j, last), v_col0 + h)),
    ]
    args = [lam.reshape(1, 1).astype(F32), q, k, v_src]
    if extra is not None:
        kx, vx_src, vx_col0 = extra
        mx = kx.shape[0]
        in_specs += [pl.BlockSpec((mx, hw), lambda h, i, j: (0, h)),
                     pl.BlockSpec((mx, hw), lambda h, i, j: (0, vx_col0 + h))]
        args += [kx, vx_src]
    in_specs.append(pl.BlockSpec((1, hw), lambda h, i, j: (0, 0)))
    args.append(sub_g.reshape(1, hw))
    return pl.pallas_call(
        functools.partial(_diff_attn_kernel, n_main=n_main, has_extra=extra is not None, out_scale=out_scale),
        grid=(heads, mq // tq, n_main + (extra is not None)),
        in_specs=in_specs,
        out_specs=pl.BlockSpec((tq, hw), lambda h, i, j: (i, h)),
        out_shape=jax.ShapeDtypeStruct((mq, bw), BF16),
        scratch_shapes=[pltpu.VMEM((2, tq, 1), F32), pltpu.VMEM((2, tq, 1), F32), pltpu.VMEM((2, tq, hw), F32)],
        compiler_params=_cparams("arbitrary", "arbitrary", "arbitrary"),
        name="diff_attention",
    )(*args)


def _na_kernel(q_ref, k_ref, v_ref, kc_ref, vc_ref, bias_ref, o_ref, *, rows):
    b = pl.program_id(1)
    start_row = jnp.clip(b * NA_Q_ROWS - NA_WIN_R // 2, 0, rows - NA_K_ROWS)
    start = pl.multiple_of(start_row * GRID_W, GRID_W * (NA_WIN_R // 2))
    n_keys = NA_K_ROWS * GRID_W
    q = q_ref[...]
    k = k_ref[pl.ds(start, n_keys), :]
    v = v_ref[pl.ds(start, n_keys), :]
    nt = (((1,), (1,)), ((), ()))
    s_loc = lax.dot_general(q, k, nt, preferred_element_type=F32) + bias_ref[...]
    s_ctx = lax.dot_general(q, kc_ref[...], nt, preferred_element_type=F32)
    m = jnp.maximum(jnp.max(s_loc, axis=-1, keepdims=True), jnp.max(s_ctx, axis=-1, keepdims=True))
    p_loc = jnp.exp2(s_loc - m)
    p_ctx = jnp.exp2(s_ctx - m)
    denom = jnp.sum(p_loc, axis=-1, keepdims=True) + jnp.sum(p_ctx, axis=-1, keepdims=True)
    o = (jnp.dot(p_loc.astype(BF16), v, preferred_element_type=F32)
         + jnp.dot(p_ctx.astype(BF16), vc_ref[...], preferred_element_type=F32))
    o_ref[...] = (o * (1.0 / denom)).astype(o_ref.dtype)


def na_bias_tables(rpb, rows):
    n_blocks = rows // NA_Q_ROWS
    wr = min(NA_WIN_R, rows)
    w = jnp.arange(GRID_W)
    col_start = jnp.clip(w - NA_WIN_C // 2, 0, GRID_W - NA_WIN_C)
    kc = jnp.arange(GRID_W)
    valid_c = (kc[None, :] >= col_start[:, None]) & (kc[None, :] < col_start[:, None] + NA_WIN_C)
    rel_c = jnp.clip(kc[None, :] - w[:, None] + (NA_WIN_C - 1), 0, 2 * NA_WIN_C - 2)
    tables = []
    for blk in (0, 1, n_blocks - 1):
        r = blk * NA_Q_ROWS + jnp.arange(NA_Q_ROWS)
        start = jnp.clip(blk * NA_Q_ROWS - NA_WIN_R // 2, 0, rows - NA_K_ROWS)
        kr = start + jnp.arange(NA_K_ROWS)
        row_start = jnp.clip(r - NA_WIN_R // 2, 0, rows - wr)
        valid_r = (kr[None, :] >= row_start[:, None]) & (kr[None, :] < row_start[:, None] + wr)
        rel_r = jnp.clip(kr[None, :] - r[:, None] + (NA_WIN_R - 1), 0, 2 * NA_WIN_R - 2)
        bias = rpb[:, rel_r][:, :, :, rel_c]
        valid = valid_r[:, :, None, None] & valid_c[None, None]
        bias = jnp.where(valid[None], bias * LOG2E, MASKED)
        bias = jnp.transpose(bias, (0, 1, 3, 2, 4))
        tables.append(bias.reshape(rpb.shape[0], NA_Q_ROWS * GRID_W, NA_K_ROWS * GRID_W))
    return jnp.stack(tables, axis=1).astype(F32)


def neighbourhood_attention(q, k, v_src, v_col0, kc, vc_src, vc_col0, bias):
    s, bw = q.shape
    heads = bw // HEAD_DIM
    rows = s // GRID_W
    n_blocks = rows // NA_Q_ROWS
    assert n_blocks >= 3 and rows % NA_Q_ROWS == 0
    tq = NA_Q_ROWS * GRID_W
    lc = kc.shape[0]
    variant = lambda b: jnp.where(b == 0, 0, jnp.where(b == n_blocks - 1, 2, 1))
    return pl.pallas_call(
        functools.partial(_na_kernel, rows=rows),
        grid=(heads, n_blocks),
        in_specs=[
            pl.BlockSpec((tq, HEAD_DIM), lambda h, b: (b, h)),
            pl.BlockSpec((s, HEAD_DIM), lambda h, b: (0, h)),
            pl.BlockSpec((s, HEAD_DIM), lambda h, b: (0, v_col0 + h)),
            pl.BlockSpec((lc, HEAD_DIM), lambda h, b: (0, h)),
            pl.BlockSpec((lc, HEAD_DIM), lambda h, b: (0, vc_col0 + h)),
            pl.BlockSpec((None, None, tq, NA_K_ROWS * GRID_W), lambda h, b: (h, variant(b), 0, 0)),
        ],
        out_specs=pl.BlockSpec((tq, HEAD_DIM), lambda h, b: (b, h)),
        out_shape=jax.ShapeDtypeStruct((s, bw), BF16),
        compiler_params=_cparams("arbitrary", "arbitrary"),
        name="neighbourhood_attention",
    )(q, k, v_src, kc, vc_src, bias)


def _dense_attn_kernel(q_ref, k_ref, v_ref, o_ref):
    s = lax.dot_general(q_ref[...], k_ref[...], (((1,), (1,)), ((), ())), preferred_element_type=F32)
    p = jnp.exp2(s - jnp.max(s, axis=-1, keepdims=True))
    o = jnp.dot(p.astype(BF16), v_ref[...], preferred_element_type=F32)
    o_ref[...] = (o * (1.0 / jnp.sum(p, axis=-1, keepdims=True))).astype(o_ref.dtype)


def dense_attention(q, k, v_src, v_col0):
    m, bw = q.shape
    heads = bw // HEAD_DIM
    blk = lambda c0: pl.BlockSpec((m, HEAD_DIM), lambda h: (0, c0 + h))
    return pl.pallas_call(
        _dense_attn_kernel, grid=(heads,), in_specs=[blk(0), blk(0), blk(v_col0)], out_specs=blk(0),
        out_shape=jax.ShapeDtypeStruct((m, bw), BF16),
        compiler_params=_cparams("arbitrary"), name="dense_attention",
    )(q, k, v_src)


def _merge_up_kernel(od_ref, on_ref, cb_ref, cc_ref, cx_ref, ccp_ref, cxp_ref, ccn_ref, cxn_ref, cw_ref,
                     g0_ref, g1_ref, g2_ref, wd_ref, wn_ref, wc_ref, o_ref, oc_sc):
    i, j = pl.program_id(0), pl.program_id(1)

    @pl.when(j == 0)
    def _():
        u = cc_ref[...].astype(F32) * cx_ref[...].astype(F32)
        tm = u.shape[0]
        halo = ccp_ref.shape[0]
        prev_row = ccp_ref[halo - 1:halo, :].astype(F32) * cxp_ref[halo - 1:halo, :].astype(F32)
        next_row = ccn_ref[0:1, :].astype(F32) * cxn_ref[0:1, :].astype(F32)
        prev_row = jnp.where(i == 0, 0.0, prev_row)
        next_row = jnp.where(i == pl.num_programs(0) - 1, 0.0, next_row)
        r = lax.broadcasted_iota(jnp.int32, u.shape, 0)
        u_prev = jnp.where(r == 0, prev_row, pltpu.roll(u, 1, axis=0))
        u_next = jnp.where(r == tm - 1, next_row, pltpu.roll(u, tm - 1, axis=0))
        conv = cw_ref[0:1, :] * u_prev + cw_ref[1:2, :] * u + cw_ref[2:3, :] * u_next
        oc_sc[...] = (cb_ref[...].astype(F32) * conv).astype(BF16)

    acc = _sigmoid(g0_ref[...].astype(F32)) * jnp.dot(od_ref[...], wd_ref[...], preferred_element_type=F32)
    acc += _sigmoid(g1_ref[...].astype(F32)) * jnp.dot(on_ref[...], wn_ref[...], preferred_element_type=F32)
    acc += _sigmoid(g2_ref[...].astype(F32)) * jnp.dot(oc_sc[...], wc_ref[...], preferred_element_type=F32)
    o_ref[...] = acc.astype(o_ref.dtype)


def merge_up(o_diff, o_na, p, conv_w, w_up_diff, w_up_na, w_up_conv, layer, d):
    m, bw = o_diff.shape
    tm, tn = _tile(m, 512), math.gcd(_tile(d, 1024), bw)
    halo = 8
    n_halo = m // halo
    row_blk = lambda c: pl.BlockSpec((tm, bw), lambda i, j, c=c: (i, c))
    prev_blk = lambda c: pl.BlockSpec((halo, bw), lambda i, j, c=c: (jnp.maximum(i * (tm // halo) - 1, 0), c))
    next_blk = lambda c: pl.BlockSpec((halo, bw), lambda i, j, c=c: (jnp.minimum((i + 1) * (tm // halo), n_halo - 1), c))
    gate0 = 9 * bw // tn
    gate_blk = lambda b: pl.BlockSpec((tm, tn), lambda i, j, b=b: (i, gate0 + b * (d // tn) + j))
    w_blk = pl.BlockSpec((None, bw, tn), lambda i, j: (layer, 0, j))
    assert (9 * bw) % tn == 0
    return pl.pallas_call(
        _merge_up_kernel,
        grid=(m // tm, d // tn),
        in_specs=[row_blk(0), row_blk(0), row_blk(6), row_blk(7), row_blk(8),
                  prev_blk(7), prev_blk(8), next_blk(7), next_blk(8),
                  pl.BlockSpec((CONV_K, bw), lambda i, j: (0, 0)),
                  gate_blk(0), gate_blk(1), gate_blk(2), w_blk, w_blk, w_blk],
        out_specs=pl.BlockSpec((tm, tn), lambda i, j: (i, j)),
        out_shape=jax.ShapeDtypeStruct((m, d), BF16),
        scratch_shapes=[pltpu.VMEM((tm, bw), BF16)],
        compiler_params=_cparams("arbitrary", "arbitrary"),
        name="merge_up",
    )(o_diff, o_na, p, p, p, p, p, p, p, conv_w, p, p, p, w_up_diff, w_up_na, w_up_conv)


def kernel(x, c, ctx, c_ctx, w_mod_a, w_mod_b, b_mod, norm1_g, w_in, diff_qn_g, diff_kn_g, lambda_q1, lambda_k1, lambda_q2, lambda_k2, diff_subln_g, na_qn_g, na_kn_g, na_rpb, conv_w, w_up_diff, w_up_na, w_up_conv, w_o, norm2_g, router_w, router_b, w_gu, b_gu, w_down, b_down):
    batch, s, d = x.shape
    assert batch == 1 and ctx.shape[0] == 1
    depth = w_in.shape[0]
    bw = w_up_diff.shape[1]
    dv0 = 2 * bw // (2 * HEAD_DIM)
    nv0 = 5 * bw // HEAD_DIM
    rows = s // GRID_W

    cond = jnp.zeros((8, d), F32).at[0].set(c[0]).at[1].set(c_ctx)
    mod = modulation(cond, w_mod_a, w_mod_b, b_mod).reshape(depth, 8, N_MOD, d)
    rope = rope_tables(s)
    w_in, w_o, w_gu, w_down = (w.astype(BF16) for w in (w_in, w_o, w_gu, w_down))
    ups = tuple(w.astype(BF16) for w in (w_up_diff, w_up_na, w_up_conv))

    xl, xc = x[0], ctx[0]
    for i in range(depth):
        last = i == depth - 1
        lam_init = 0.8 - 0.6 * math.exp(-0.3 * i)
        lam = (jnp.exp(jnp.sum(lambda_q1[i] * lambda_k1[i])) - jnp.exp(jnp.sum(lambda_q2[i] * lambda_k2[i]))
               + lam_init)
        ml, mc = mod[i, 0], mod[i, 1]
        gains = jnp.stack([diff_qn_g[i], diff_kn_g[i], na_qn_g[i], na_kn_g[i]])

        p_l = project_in(prenorm(xl, norm1_g[i], ml[1], ml[0]), w_in, i)
        p_c = project_in(prenorm(xc, norm1_g[i], mc[1], mc[0]), w_in, i)
        dq_l, dk_l, nq_l, nk_l = qk_prep(p_l, gains, rope, bw)
        dq_c, dk_c, nq_c, nk_c = qk_prep(p_c, gains, None, bw)

        diff_args = (dq_l, dk_l, p_l, dv0, lam, diff_subln_g[i], 1.0 - lam_init, (dk_c, p_c, dv0))
        o_diff_l = lax.cond(score_bound_log2(diff_qn_g[i], diff_kn_g[i]) <= SCORE_BOUND_LOG2,
                            lambda: diff_attention_bounded(*diff_args), lambda: diff_attention(*diff_args))
        o_na_l = neighbourhood_attention(nq_l, nk_l, p_l, nv0, nk_c, p_c, nv0,
                                         na_bias_tables(na_rpb[i], rows))
        xl = project_out(merge_up(o_diff_l, o_na_l, p_l, conv_w[i], *ups, i, d), w_o, i, xl, ml[2])
        if not last:
            o_diff_c = diff_attention(dq_c, dk_c, p_c, dv0, lam, diff_subln_g[i], 1.0 - lam_init)
            o_na_c = dense_attention(nq_c, nk_c, p_c, nv0)
            xc = project_out(merge_up(o_diff_c, o_na_c, p_c, conv_w[i], *ups, i, d), w_o, i, xc, mc[2])

        router = (router_w[i], router_b[i])
        h_l, eid, ew, rank, counts = prenorm_route(xl, norm2_g[i], ml[4], ml[3], router)
        xl = moe_routed(h_l, eid, ew, rank, counts, w_gu, b_gu[i], w_down, b_down[i], i, xl, ml[5])
        if not last:
            h_c, comb_c = prenorm(xc, norm2_g[i], mc[4], mc[3], router)
            xc = moe(h_c, comb_c, w_gu, b_gu[i], w_down, b_down[i], i, xc, mc[5])
    return xl[None]
```

```python
import functools
import math

import jax
import jax.numpy as jnp
from jax import lax
from jax.experimental import pallas as pl
from jax.experimental.pallas import tpu as pltpu

F32 = jnp.float32
BF16 = jnp.bfloat16
HIGHEST = lax.Precision.HIGHEST

GRID_W = 64
HEAD_DIM = 128
NA_WIN_R = 8
NA_WIN_C = 16
CONV_K = 3
N_BRANCH = 3
N_MOD = 6
TOP_K = 4
SWIGLU_LIMIT = 7.0
SWIGLU_ALPHA = 1.702
ROPE_BASE = 10000.0
EPS = 1e-6
LOG2E = math.log2(math.e)
Q_SCALE = HEAD_DIM ** -0.5 * LOG2E
MASKED = -1e30
SCORE_BOUND_LOG2 = 60.0

V7X_VMEM_LIMIT_BYTES = 56 * 1024 * 1024
NA_Q_ROWS = 8
NA_K_ROWS = NA_Q_ROWS + NA_WIN_R


def _cparams(*sem):
    return pltpu.CompilerParams(dimension_semantics=sem, vmem_limit_bytes=V7X_VMEM_LIMIT_BYTES)


def _tile(n, target):
    if n <= target:
        return n
    t = target
    while n % t:
        t -= 8
    assert t > 0, (n, target)
    return t


def _sigmoid(x):
    return 1.0 / (1.0 + jnp.exp(-x))


def _mod_kernel(c_ref, wa_ref, wb_ref, b_ref, o_ref):
    c = c_ref[...]
    s = c * _sigmoid(c)
    m1 = jnp.dot(s, wa_ref[...], precision=HIGHEST, preferred_element_type=F32)
    o_ref[...] = jnp.dot(m1, wb_ref[...], precision=HIGHEST, preferred_element_type=F32) + b_ref[...]


def modulation(cond, w_a, w_b, b):
    depth, d, r = w_a.shape
    width = w_b.shape[-1]
    rows = cond.shape[0]
    tn = _tile(width, 4096)
    return pl.pallas_call(
        _mod_kernel,
        grid=(depth, width // tn),
        in_specs=[
            pl.BlockSpec((rows, d), lambda l, j: (0, 0)),
            pl.BlockSpec((None, d, r), lambda l, j: (l, 0, 0)),
            pl.BlockSpec((None, r, tn), lambda l, j: (l, 0, j)),
            pl.BlockSpec((None, 1, tn), lambda l, j: (l, 0, j)),
        ],
        out_specs=pl.BlockSpec((None, rows, tn), lambda l, j: (l, 0, j)),
        out_shape=jax.ShapeDtypeStruct((depth, rows, width), F32),
        compiler_params=_cparams("arbitrary", "arbitrary"),
        name="modulation",
    )(cond, w_a, w_b, b.reshape(depth, 1, width))


def _modulated_norm(x_ref, g_ref, sc_ref, sh_ref):
    x = x_ref[...]
    y = x * lax.rsqrt(jnp.mean(x * x, axis=-1, keepdims=True) + EPS)
    return (y * g_ref[...]) * (1.0 + sc_ref[...]) + sh_ref[...]


def _prenorm_kernel(x_ref, g_ref, sc_ref, sh_ref, o_ref):
    o_ref[...] = _modulated_norm(x_ref, g_ref, sc_ref, sh_ref).astype(o_ref.dtype)


def _route(h, rw_ref, rb_ref):
    logits = jnp.dot(h, rw_ref[...], precision=HIGHEST, preferred_element_type=F32) + rb_ref[...]
    n_exp = logits.shape[-1]
    lane = lax.broadcasted_iota(jnp.int32, logits.shape, 1).astype(F32)
    sels, ids, vals = [], [], []
    for _ in range(TOP_K):
        m = jnp.max(logits, axis=-1, keepdims=True)
        first = jnp.min(jnp.where(logits == m, lane, float(n_exp)), axis=-1, keepdims=True)
        sel = lane == first
        sels.append(sel)
        ids.append(first)
        vals.append(m)
        logits = jnp.where(sel, -jnp.inf, logits)
    es = [jnp.exp(v - vals[0]) for v in vals]
    inv = 1.0 / functools.reduce(lambda a, b: a + b, es)
    return sels, ids, [e * inv for e in es]


def _prenorm_router_kernel(x_ref, g_ref, sc_ref, sh_ref, rw_ref, rb_ref, o_ref, comb_ref):
    h = _modulated_norm(x_ref, g_ref, sc_ref, sh_ref)
    o_ref[...] = h.astype(o_ref.dtype)
    sels, _, wts = _route(h, rw_ref, rb_ref)
    comb = jnp.zeros(comb_ref.shape, F32)
    for sel, w in zip(sels, wts):
        comb = comb + jnp.where(sel, w, 0.0)
    comb_ref[...] = comb


def _columns(cols, width):
    lane = lax.broadcasted_iota(jnp.int32, (cols[0].shape[0], width), 1)
    out = jnp.zeros(lane.shape, cols[0].dtype)
    for k, c in enumerate(cols):
        out = jnp.where(lane == k, c, out)
    return out


def _prenorm_route_kernel(x_ref, g_ref, sc_ref, sh_ref, rw_ref, rb_ref,
                          o_ref, eid_ref, ew_ref, rank_ref, cnt_ref, carry_sc):
    @pl.when(pl.program_id(0) == 0)
    def _():
        carry_sc[...] = jnp.zeros(carry_sc.shape, F32)

    h = _modulated_norm(x_ref, g_ref, sc_ref, sh_ref)
    o_ref[...] = h.astype(o_ref.dtype)
    sels, ids, wts = _route(h, rw_ref, rb_ref)
    tm = h.shape[0]
    chosen = functools.reduce(lambda a, b: a + b, [jnp.where(s, 1.0, 0.0) for s in sels])
    r = lax.broadcasted_iota(jnp.int32, (tm, tm), 0)
    c = lax.broadcasted_iota(jnp.int32, (tm, tm), 1)
    earlier = jnp.where(c < r, 1.0, 0.0).astype(BF16)
    before = jnp.dot(earlier, chosen.astype(BF16), preferred_element_type=F32) + carry_sc[...]
    ranks = [jnp.sum(jnp.where(s, before, 0.0), axis=-1, keepdims=True) for s in sels]
    carry_sc[...] += jnp.sum(chosen, axis=0, keepdims=True)
    eid_ref[...] = _columns(ids, TOP_K).astype(jnp.int32)
    ew_ref[...] = _columns(wts, TOP_K)
    rank_ref[...] = _columns(ranks, TOP_K).astype(jnp.int32)
    cnt_ref[...] = carry_sc[...].astype(jnp.int32)


def prenorm(x, g, scale, shift, router=None):
    m, d = x.shape
    tm = _tile(m, 256)
    row = lambda i: (i, 0)
    fixed = lambda i: (0, 0)
    vec = pl.BlockSpec((1, d), fixed)
    in_specs = [pl.BlockSpec((tm, d), row), vec, vec, vec]
    args = [x, g.reshape(1, d), scale.reshape(1, d), shift.reshape(1, d)]
    if router is None:
        return pl.pallas_call(
            _prenorm_kernel, grid=(m // tm,), in_specs=in_specs,
            out_specs=pl.BlockSpec((tm, d), row),
            out_shape=jax.ShapeDtypeStruct((m, d), BF16),
            compiler_params=_cparams("arbitrary"), name="prenorm",
        )(*args)
    rw, rb = router
    n_exp = rw.shape[-1]
    in_specs += [pl.BlockSpec((d, n_exp), fixed), pl.BlockSpec((1, n_exp), fixed)]
    args += [rw, rb.reshape(1, n_exp)]
    return pl.pallas_call(
        _prenorm_router_kernel, grid=(m // tm,), in_specs=in_specs,
        out_specs=[pl.BlockSpec((tm, d), row), pl.BlockSpec((tm, n_exp), row)],
        out_shape=[jax.ShapeDtypeStruct((m, d), BF16), jax.ShapeDtypeStruct((m, n_exp), F32)],
        compiler_params=_cparams("arbitrary"), name="prenorm_router",
    )(*args)


def prenorm_route(x, g, scale, shift, router):
    m, d = x.shape
    tm = _tile(m, 256)
    rw, rb = router
    n_exp = rw.shape[-1]
    row = lambda i: (i, 0)
    fixed = lambda i: (0, 0)
    vec = pl.BlockSpec((1, d), fixed)
    per_tok = pl.BlockSpec((tm, TOP_K), row)
    return pl.pallas_call(
        _prenorm_route_kernel, grid=(m // tm,),
        in_specs=[pl.BlockSpec((tm, d), row), vec, vec, vec,
                  pl.BlockSpec((d, n_exp), fixed), pl.BlockSpec((1, n_exp), fixed)],
        out_specs=[pl.BlockSpec((tm, d), row), per_tok, per_tok, per_tok, pl.BlockSpec((1, n_exp), fixed)],
        out_shape=[jax.ShapeDtypeStruct((m, d), BF16), jax.ShapeDtypeStruct((m, TOP_K), jnp.int32),
                   jax.ShapeDtypeStruct((m, TOP_K), F32), jax.ShapeDtypeStruct((m, TOP_K), jnp.int32),
                   jax.ShapeDtypeStruct((1, n_exp), jnp.int32)],
        scratch_shapes=[pltpu.VMEM((1, n_exp), F32)],
        compiler_params=_cparams("arbitrary"), name="prenorm_route",
    )(x, g.reshape(1, d), scale.reshape(1, d), shift.reshape(1, d), rw, rb.reshape(1, n_exp))


EXPERT_TILE = 256


def _expert_mlp_kernel(te_ref, nt_ref, x_ref, wgu_ref, bgu_ref, wd_ref, bd_ref, o_ref):
    used = pl.program_id(0) < nt_ref[0]

    @pl.when(jnp.logical_not(used))
    def _():
        o_ref[...] = jnp.zeros(o_ref.shape, o_ref.dtype)

    @pl.when(used)
    def _():
        d_exp = wd_ref.shape[0]
        gu = jnp.dot(x_ref[...], wgu_ref[...], preferred_element_type=F32) + bgu_ref[...]
        gate = jnp.minimum(gu[:, :d_exp], SWIGLU_LIMIT)
        lin = jnp.clip(gu[:, d_exp:], -SWIGLU_LIMIT, SWIGLU_LIMIT)
        act = gate * _sigmoid(SWIGLU_ALPHA * gate) * (lin + 1.0)
        y = jnp.dot(act.astype(BF16), wd_ref[...], preferred_element_type=F32) + bd_ref[...]
        o_ref[...] = y.astype(o_ref.dtype)


def expert_mlp(xg, tile_expert, n_tiles_used, w_gu, b_gu, w_down, b_down, layer):
    p_rows, d = xg.shape
    _, n_exp, _, two_de = w_gu.shape
    d_exp = two_de // 2
    t = EXPERT_TILE
    grid_spec = pltpu.PrefetchScalarGridSpec(
        num_scalar_prefetch=2,
        grid=(p_rows // t,),
        in_specs=[
            pl.BlockSpec((t, d), lambda j, te, nt: (j, 0)),
            pl.BlockSpec((None, None, d, two_de), lambda j, te, nt: (layer, te[j], 0, 0)),
            pl.BlockSpec((None, 1, two_de), lambda j, te, nt: (te[j], 0, 0)),
            pl.BlockSpec((None, None, d_exp, d), lambda j, te, nt: (layer, te[j], 0, 0)),
            pl.BlockSpec((None, 1, d), lambda j, te, nt: (te[j], 0, 0)),
        ],
        out_specs=pl.BlockSpec((t, d), lambda j, te, nt: (j, 0)),
    )
    return pl.pallas_call(
        _expert_mlp_kernel, grid_spec=grid_spec,
        out_shape=jax.ShapeDtypeStruct((p_rows, d), BF16),
        compiler_params=_cparams("arbitrary"), name="expert_mlp",
    )(tile_expert, n_tiles_used, xg, w_gu, b_gu.reshape(n_exp, 1, two_de), w_down, b_down.reshape(n_exp, 1, d))


def _combine_kernel(x_ref, gate_ref, ew_ref, *rest):
    y_refs, o_ref = rest[:-1], rest[-1]
    ew = ew_ref[...]
    acc = jnp.zeros(o_ref.shape, F32)
    for k, y_ref in enumerate(y_refs):
        acc = acc + ew[:, k:k + 1] * y_ref[...].astype(F32)
    o_ref[...] = x_ref[...] + gate_ref[...] * acc


def moe_combine(x, gate, ew, y4):
    m, d = x.shape
    tm, tn = _tile(m, 512), _tile(d, 1024)
    tile = pl.BlockSpec((tm, tn), lambda i, j: (i, j))
    y_blk = lambda k: pl.BlockSpec((None, tm, tn), lambda i, j, k=k: (k, i, j))
    return pl.pallas_call(
        _combine_kernel, grid=(m // tm, d // tn),
        in_specs=[tile, pl.BlockSpec((1, tn), lambda i, j: (0, j)), pl.BlockSpec((tm, TOP_K), lambda i, j: (i, 0))]
        + [y_blk(k) for k in range(TOP_K)],
        out_specs=tile, out_shape=jax.ShapeDtypeStruct((m, d), F32),
        compiler_params=_cparams("arbitrary", "arbitrary"), name="moe_combine",
    )(x, gate.reshape(1, d), ew, *([y4] * TOP_K))


def moe_routed(h, eid, ew, rank, counts, w_gu, b_gu, w_down, b_down, layer, x, gate):
    m, d = h.shape
    n_exp = counts.shape[-1]
    t = EXPERT_TILE
    p_rows = m * TOP_K + n_exp * t
    counts = counts.reshape(n_exp)
    tiles_per = (counts + t - 1) // t
    tile_end = jnp.cumsum(tiles_per)
    start = (tile_end - tiles_per) * t
    pos = start[eid] + rank
    tile_id = jnp.arange(p_rows // t, dtype=jnp.int32)
    tile_expert = jnp.minimum(jnp.sum((tile_end[None, :] <= tile_id[:, None]).astype(jnp.int32), axis=1),
                              n_exp - 1)
    n_used = tile_end[-1:].astype(jnp.int32)
    pos_t = pos.T.reshape(-1)
    src = jnp.zeros((p_rows,), jnp.int32).at[pos_t].set(jnp.tile(jnp.arange(m, dtype=jnp.int32), TOP_K),
                                                        unique_indices=True, mode="promise_in_bounds")
    xg = h.at[src].get(mode="promise_in_bounds")
    yg = expert_mlp(xg, tile_expert, n_used, w_gu, b_gu, w_down, b_down, layer)
    y4 = yg.at[pos_t].get(mode="promise_in_bounds", unique_indices=True).reshape(TOP_K, m, d)
    return moe_combine(x, gate, ew, y4)


def _mm_kernel(*refs, epilogue):
    a_ref, w_ref = refs[0], refs[1]
    extras, o_ref = refs[2:-1], refs[-1]
    acc = jnp.dot(a_ref[...], w_ref[...], preferred_element_type=F32)
    o_ref[...] = epilogue(acc, *extras).astype(o_ref.dtype)


def matmul(a, w, w_spec, tm, tn, n_col_tiles, extras, epilogue, out_spec, out_shape, name):
    m, k = a.shape
    return pl.pallas_call(
        functools.partial(_mm_kernel, epilogue=epilogue),
        grid=(m // tm, n_col_tiles),
        in_specs=[pl.BlockSpec((tm, k), lambda i, j: (i, 0)), w_spec] + [s for _, s in extras],
        out_specs=out_spec,
        out_shape=out_shape,
        compiler_params=_cparams("arbitrary", "arbitrary"),
        name=name,
    )(a, w, *[x for x, _ in extras])


def _cast_epilogue(acc):
    return acc


def _residual_epilogue(acc, x_ref, gate_ref):
    return x_ref[...] + gate_ref[...] * acc


def _swiglu_epilogue(acc, b_ref, comb_ref):
    d_exp = acc.shape[-1] // 2
    gu = acc + b_ref[...]
    gate = jnp.minimum(gu[:, :d_exp], SWIGLU_LIMIT)
    lin = jnp.clip(gu[:, d_exp:], -SWIGLU_LIMIT, SWIGLU_LIMIT)
    act = gate * _sigmoid(SWIGLU_ALPHA * gate) * (lin + 1.0)
    comb = comb_ref[...]
    lane = lax.broadcasted_iota(jnp.int32, comb.shape, 1)
    cw = jnp.sum(jnp.where(lane == pl.program_id(1), comb, 0.0), axis=-1, keepdims=True)
    return act * cw


def _moe_down_epilogue(acc, comb_ref, bd_ref, x_ref, gate_ref):
    bias = jnp.dot(comb_ref[...].astype(BF16), bd_ref[...].astype(BF16), preferred_element_type=F32)
    return x_ref[...] + gate_ref[...] * (acc + bias)


def project_in(h, w_in, layer):
    m, d = h.shape
    width = w_in.shape[-1]
    tm, tn = _tile(m, 1024), _tile(width, 1024)
    return matmul(h, w_in, pl.BlockSpec((None, d, tn), lambda i, j: (layer, 0, j)), tm, tn, width // tn, [],
                  _cast_epilogue, pl.BlockSpec((tm, tn), lambda i, j: (i, j)),
                  jax.ShapeDtypeStruct((m, width), BF16), "project_in")


def project_out(merged, w_o, layer, x, gate):
    m, d = x.shape
    tm, tn = _tile(m, 1024), _tile(d, 512)
    tile = pl.BlockSpec((tm, tn), lambda i, j: (i, j))
    return matmul(merged, w_o, pl.BlockSpec((None, merged.shape[1], tn), lambda i, j: (layer, 0, j)),
                  tm, tn, d // tn,
                  [(x, tile), (gate.reshape(1, d), pl.BlockSpec((1, tn), lambda i, j: (0, j)))],
                  _residual_epilogue, tile, jax.ShapeDtypeStruct((m, d), F32), "project_out")


def moe(h, comb, w_gu, b_gu, w_down, b_down, layer, x, gate):
    m, d = h.shape
    depth, n_exp, _, two_de = w_gu.shape
    d_exp = two_de // 2
    tm = _tile(m, 1024)
    act = matmul(
        h, w_gu, pl.BlockSpec((None, None, d, two_de), lambda i, j: (layer, j, 0, 0)), tm, two_de, n_exp,
        [(b_gu.reshape(n_exp, 1, two_de), pl.BlockSpec((None, 1, two_de), lambda i, j: (j, 0, 0))),
         (comb, pl.BlockSpec((tm, n_exp), lambda i, j: (i, 0)))],
        _swiglu_epilogue, pl.BlockSpec((tm, d_exp), lambda i, j: (i, j)),
        jax.ShapeDtypeStruct((m, n_exp * d_exp), BF16), "moe_up")
    tm, tn = _tile(m, 512), _tile(d, 512)
    tile = pl.BlockSpec((tm, tn), lambda i, j: (i, j))
    return matmul(
        act, w_down.reshape(depth, n_exp * d_exp, d),
        pl.BlockSpec((None, n_exp * d_exp, tn), lambda i, j: (layer, 0, j)), tm, tn, d // tn,
        [(comb, pl.BlockSpec((tm, n_exp), lambda i, j: (i, 0))),
         (b_down, pl.BlockSpec((n_exp, tn), lambda i, j: (0, j))),
         (x, tile), (gate.reshape(1, d), pl.BlockSpec((1, tn), lambda i, j: (0, j)))],
        _moe_down_epilogue, tile, jax.ShapeDtypeStruct((m, d), F32), "moe_down")


def _chunk_norm(x, g):
    return x * lax.rsqrt(jnp.mean(x * x, axis=-1, keepdims=True) + EPS) * g


def _qk_prep_kernel(*refs, rope):
    if rope:
        dq_ref, dk_ref, nq_ref, nk_ref, g_ref, cos_ref, sin_ref = refs[:7]
        cos, sin = cos_ref[...], sin_ref[...]
        lane = lax.broadcasted_iota(jnp.int32, cos.shape, 1)
        lower = (lane & (HEAD_DIM // 4)) == 0
    else:
        dq_ref, dk_ref, nq_ref, nk_ref, g_ref = refs[:5]
    outs = refs[-4:]
    ins = (dq_ref, dk_ref, nq_ref, nk_ref)
    for idx in range(4):
        g = g_ref[idx:idx + 1, :]
        rot = rope and idx < 2
        scale = Q_SCALE if idx in (0, 2) else 1.0
        for c in range(ins[idx].shape[-1] // HEAD_DIM):
            sl = slice(c * HEAD_DIM, (c + 1) * HEAD_DIM)
            y = _chunk_norm(ins[idx][:, sl].astype(F32), g)
            if rot:
                swapped = jnp.where(lower, pltpu.roll(y, HEAD_DIM - HEAD_DIM // 4, axis=1),
                                    pltpu.roll(y, HEAD_DIM // 4, axis=1))
                y = y * cos + swapped * sin
            outs[idx][:, sl] = (y * scale).astype(BF16)


def qk_prep(p, gains, rope, bw):
    m = p.shape[0]
    tm = _tile(m, 512)
    col = lambda c: pl.BlockSpec((tm, bw), lambda i, c=c: (i, c))
    in_specs = [col(0), col(1), col(3), col(4), pl.BlockSpec((4, HEAD_DIM), lambda i: (0, 0))]
    args = [p, p, p, p, gains]
    if rope is not None:
        in_specs += [pl.BlockSpec((tm, HEAD_DIM), lambda i: (i, 0))] * 2
        args += list(rope)
    return pl.pallas_call(
        functools.partial(_qk_prep_kernel, rope=rope is not None),
        grid=(m // tm,), in_specs=in_specs,
        out_specs=[pl.BlockSpec((tm, bw), lambda i: (i, 0))] * 4,
        out_shape=[jax.ShapeDtypeStruct((m, bw), BF16)] * 4,
        compiler_params=_cparams("arbitrary"), name="qk_prep",
    )(*args)


def rope_tables(n):
    t = jnp.arange(n, dtype=jnp.int32)
    row = (t // GRID_W).astype(F32)
    col = (t % GRID_W).astype(F32)
    n_freq = HEAD_DIM // 4
    inv = ROPE_BASE ** (-jnp.arange(n_freq, dtype=F32) / n_freq)
    ar, ac = row[:, None] * inv, col[:, None] * inv
    ang = jnp.concatenate([ar, ar, ac, ac], axis=-1)
    lane = jnp.arange(HEAD_DIM)
    sign = jnp.where((lane & n_freq) == 0, -1.0, 1.0).astype(F32)
    return jnp.cos(ang), jnp.sin(ang) * sign


def _diff_attn_kernel(lam_ref, q_ref, k_ref, v_ref, *rest, n_main, has_extra, out_scale):
    if has_extra:
        kx_ref, vx_ref, g_ref, o_ref, m_sc, l_sc, acc_sc = rest
    else:
        g_ref, o_ref, m_sc, l_sc, acc_sc = rest
    kj = pl.program_id(2)
    n_steps = n_main + (1 if has_extra else 0)

    @pl.when(kj == 0)
    def _():
        m_sc[...] = jnp.full(m_sc.shape, -jnp.inf, F32)
        l_sc[...] = jnp.zeros(l_sc.shape, F32)
        acc_sc[...] = jnp.zeros(acc_sc.shape, F32)

    def update(kr, vr):
        v = vr[...]
        for hf in range(2):
            sl = slice(hf * HEAD_DIM, (hf + 1) * HEAD_DIM)
            s = lax.dot_general(q_ref[:, sl], kr[:, sl], (((1,), (1,)), ((), ())),
                                preferred_element_type=F32)
            m_prev = m_sc[hf]
            m_new = jnp.maximum(m_prev, jnp.max(s, axis=-1, keepdims=True))
            alpha = jnp.exp2(m_prev - m_new)
            p = jnp.exp2(s - m_new)
            l_sc[hf] = alpha * l_sc[hf] + jnp.sum(p, axis=-1, keepdims=True)
            acc_sc[hf] = alpha * acc_sc[hf] + jnp.dot(p.astype(BF16), v, preferred_element_type=F32)
            m_sc[hf] = m_new

    if has_extra:
        pl.when(kj < n_main)(lambda: update(k_ref, v_ref))
        pl.when(kj == n_main)(lambda: update(kx_ref, vx_ref))
    else:
        update(k_ref, v_ref)

    @pl.when(kj == n_steps - 1)
    def _():
        _diff_finalize(lam_ref, g_ref, o_ref, acc_sc[0], acc_sc[1], l_sc[0], l_sc[1], out_scale)


def _diff_finalize(lam_ref, g_ref, o_ref, acc0, acc1, l0, l1, out_scale):
    lam = lam_ref[0, 0]
    o = acc0 * (1.0 / l0) - lam * (acc1 * (1.0 / l1))
    y = o * lax.rsqrt(jnp.mean(o * o, axis=-1, keepdims=True) + EPS)
    o_ref[...] = (y * g_ref[...] * out_scale).astype(o_ref.dtype)


def _diff_attn_bounded_kernel(lam_ref, q_ref, k_ref, v_ref, kx_ref, vx_ref, g_ref, o_ref, l_sc, acc_sc,
                              *, tk, n_main, out_scale):
    l_sc[...] = jnp.zeros(l_sc.shape, F32)
    acc_sc[...] = jnp.zeros(acc_sc.shape, F32)

    def accumulate(k, v):
        for hf in range(2):
            sl = slice(hf * HEAD_DIM, (hf + 1) * HEAD_DIM)
            s = lax.dot_general(q_ref[:, sl], k[:, sl], (((1,), (1,)), ((), ())), preferred_element_type=F32)
            p = jnp.exp2(s)
            part = p[:, 0:HEAD_DIM]
            for c in range(1, p.shape[1] // HEAD_DIM):
                part = part + p[:, c * HEAD_DIM:(c + 1) * HEAD_DIM]
            l_sc[hf] += part
            acc_sc[hf] += jnp.dot(p.astype(BF16), v, preferred_element_type=F32)

    def body(j, carry):
        off = pl.multiple_of(j * tk, tk)
        accumulate(k_ref[pl.ds(off, tk), :], v_ref[pl.ds(off, tk), :])
        return carry

    lax.fori_loop(0, n_main, body, 0)
    accumulate(kx_ref[...], vx_ref[...])
    _diff_finalize(lam_ref, g_ref, o_ref, acc_sc[0], acc_sc[1],
                   jnp.sum(l_sc[0], axis=-1, keepdims=True), jnp.sum(l_sc[1], axis=-1, keepdims=True), out_scale)


def diff_attention_bounded(q, k, v_src, v_col0, lam, sub_g, out_scale, extra):
    mq, bw = q.shape
    mk = k.shape[0]
    hw = 2 * HEAD_DIM
    tq, tk = _tile(mq, 1024), _tile(mk, 1024)
    kx, vx_src, vx_col0 = extra
    mx = kx.shape[0]
    once = pl.Buffered(1)
    return pl.pallas_call(
        functools.partial(_diff_attn_bounded_kernel, tk=tk, n_main=mk // tk, out_scale=out_scale),
        grid=(bw // hw, mq // tq),
        in_specs=[
            pl.BlockSpec(memory_space=pltpu.SMEM),
            pl.BlockSpec((tq, hw), lambda h, i: (i, h)),
            pl.BlockSpec((mk, hw), lambda h, i: (0, h), pipeline_mode=once),
            pl.BlockSpec((mk, hw), lambda h, i: (0, v_col0 + h), pipeline_mode=once),
            pl.BlockSpec((mx, hw), lambda h, i: (0, h)),
            pl.BlockSpec((mx, hw), lambda h, i: (0, vx_col0 + h)),
            pl.BlockSpec((1, hw), lambda h, i: (0, 0)),
        ],
        out_specs=pl.BlockSpec((tq, hw), lambda h, i: (i, h)),
        out_shape=jax.ShapeDtypeStruct((mq, bw), BF16),
        scratch_shapes=[pltpu.VMEM((2, tq, HEAD_DIM), F32), pltpu.VMEM((2, tq, hw), F32)],
        compiler_params=_cparams("arbitrary", "arbitrary"),
        name="diff_attention_bounded",
    )(lam.reshape(1, 1).astype(F32), q, k, v_src, kx, vx_src, sub_g.reshape(1, hw))


def score_bound_log2(g_q, g_k):
    return Q_SCALE * HEAD_DIM * jnp.max(jnp.abs(g_q)) * jnp.max(jnp.abs(g_k)) * 1.01


def diff_attention(q, k, v_src, v_col0, lam, sub_g, out_scale, extra=None):
    mq, bw = q.shape
    mk = k.shape[0]
    hw = 2 * HEAD_DIM
    heads = bw // hw
    tq, tk = _tile(mq, 1024), _tile(mk, 1024)
    n_main = mk // tk
    last = n_main - 1
    kmap = lambda h, i, j: (jnp.minimum(j, last), h)
    in_specs = [
        pl.BlockSpec(memory_space=pltpu.SMEM),
        pl.BlockSpec((tq, hw), lambda h, i, j: (i, h)),
        pl.BlockSpec((tk, hw), kmap),
        pl.BlockSpec((tk, hw), lambda h, i, j: (jnp.minimum(j, last), v_col0 + h)),
    ]
    args = [lam.reshape(1, 1).astype(F32), q, k, v_src]
    if extra is not None:
        kx, vx_src, vx_col0 = extra
        mx = kx.shape[0]
        in_specs += [pl.BlockSpec((mx, hw), lambda h, i, j: (0, h)),
                     pl.BlockSpec((mx, hw), lambda h, i, j: (0, vx_col0 + h))]
        args += [kx, vx_src]
    in_specs.append(pl.BlockSpec((1, hw), lambda h, i, j: (0, 0)))
    args.append(sub_g.reshape(1, hw))
    return pl.pallas_call(
        functools.partial(_diff_attn_kernel, n_main=n_main, has_extra=extra is not None, out_scale=out_scale),
        grid=(heads, mq // tq, n_main + (extra is not None)),
        in_specs=in_specs,
        out_specs=pl.BlockSpec((tq, hw), lambda h, i, j: (i, h)),
        out_shape=jax.ShapeDtypeStruct((mq, bw), BF16),
        scratch_shapes=[pltpu.VMEM((2, tq, 1), F32), pltpu.VMEM((2, tq, 1), F32), pltpu.VMEM((2, tq, hw), F32)],
        compiler_params=_cparams("arbitrary", "arbitrary", "arbitrary"),
        name="diff_attention",
    )(*args)


def _na_kernel(q_ref, k_ref, v_ref, kc_ref, vc_ref, bias_ref, o_ref, *, rows):
    b = pl.program_id(1)
    start_row = jnp.clip(b * NA_Q_ROWS - NA_WIN_R // 2, 0, rows - NA_K_ROWS)
    start = pl.multiple_of(start_row * GRID_W, GRID_W * (NA_WIN_R // 2))
    n_keys = NA_K_ROWS * GRID_W
    q = q_ref[...]
    k = k_ref[pl.ds(start, n_keys), :]
    v = v_ref[pl.ds(start, n_keys), :]
    nt = (((1,), (1,)), ((), ()))
    s_loc = lax.dot_general(q, k, nt, preferred_element_type=F32) + bias_ref[...]
    s_ctx = lax.dot_general(q, kc_ref[...], nt, preferred_element_type=F32)
    m = jnp.maximum(jnp.max(s_loc, axis=-1, keepdims=True), jnp.max(s_ctx, axis=-1, keepdims=True))
    p_loc = jnp.exp2(s_loc - m)
    p_ctx = jnp.exp2(s_ctx - m)
    denom = jnp.sum(p_loc, axis=-1, keepdims=True) + jnp.sum(p_ctx, axis=-1, keepdims=True)
    o = (jnp.dot(p_loc.astype(BF16), v, preferred_element_type=F32)
         + jnp.dot(p_ctx.astype(BF16), vc_ref[...], preferred_element_type=F32))
    o_ref[...] = (o * (1.0 / denom)).astype(o_ref.dtype)


def na_bias_tables(rpb, rows):
    n_blocks = rows // NA_Q_ROWS
    wr = min(NA_WIN_R, rows)
    w = jnp.arange(GRID_W)
    col_start = jnp.clip(w - NA_WIN_C // 2, 0, GRID_W - NA_WIN_C)
    kc = jnp.arange(GRID_W)
    valid_c = (kc[None, :] >= col_start[:, None]) & (kc[None, :] < col_start[:, None] + NA_WIN_C)
    rel_c = jnp.clip(kc[None, :] - w[:, None] + (NA_WIN_C - 1), 0, 2 * NA_WIN_C - 2)
    heads, n_rel = rpb.shape[0], rpb.shape[1]
    blocks = jnp.where(valid_c[None, None], rpb[:, :, rel_c] * LOG2E, MASKED).astype(F32)
    masked = jnp.full((heads, n_rel, GRID_W, GRID_W), MASKED, F32)
    shifted = jnp.concatenate([blocks[:, 1:], masked[:, :1]], axis=1)
    pairs = jnp.concatenate([jnp.concatenate([blocks, shifted], axis=-1),
                             jnp.concatenate([masked, blocks], axis=-1),
                             jnp.concatenate([blocks, masked], axis=-1),
                             jnp.concatenate([masked[:, :1], masked[:, :1]], axis=-1)], axis=1)
    plan = []
    for blk in (0, 1, n_blocks - 1):
        start = min(max(blk * NA_Q_ROWS - NA_WIN_R // 2, 0), rows - NA_K_ROWS)
        per_row = []
        for r_i in range(NA_Q_ROWS):
            r = blk * NA_Q_ROWS + r_i
            row_start = min(max(r - NA_WIN_R // 2, 0), rows - wr)
            rel = lambda kr: kr - r + (NA_WIN_R - 1)
            inside = lambda kr: row_start <= kr < row_start + wr
            per_pair = []
            for j in range(NA_K_ROWS // 2):
                left, right = start + 2 * j, start + 2 * j + 1
                if inside(left) and inside(right):
                    per_pair.append(rel(left))
                elif inside(right):
                    per_pair.append(n_rel + rel(right))
                elif inside(left):
                    per_pair.append(2 * n_rel + rel(left))
                else:
                    per_pair.append(3 * n_rel)
            per_row.append(per_pair)
        plan.append(per_row)
    assert 2 * GRID_W == HEAD_DIM
    tq, tkeys = NA_Q_ROWS * GRID_W, NA_K_ROWS * GRID_W
    return pl.pallas_call(
        functools.partial(_na_table_kernel, plan=plan),
        grid=(heads,),
        in_specs=[pl.BlockSpec((None, 3 * n_rel + 1, GRID_W, 2 * GRID_W), lambda h: (h, 0, 0, 0))],
        out_specs=pl.BlockSpec((None, 3, tq, tkeys), lambda h: (h, 0, 0, 0)),
        out_shape=jax.ShapeDtypeStruct((heads, 3, tq, tkeys), F32),
        compiler_params=_cparams("arbitrary"), name="na_bias_tables",
    )(pairs)


def _na_table_kernel(pairs_ref, o_ref, *, plan):
    for v, per_row in enumerate(plan):
        for r_i, per_pair in enumerate(per_row):
            for j, idx in enumerate(per_pair):
                o_ref[v, r_i * GRID_W:(r_i + 1) * GRID_W, j * 2 * GRID_W:(j + 1) * 2 * GRID_W] = pairs_ref[idx]


def neighbourhood_attention(q, k, v_src, v_col0, kc, vc_src, vc_col0, bias):
    s, bw = q.shape
    heads = bw // HEAD_DIM
    rows = s // GRID_W
    n_blocks = rows // NA_Q_ROWS
    assert n_blocks >= 3 and rows % NA_Q_ROWS == 0
    tq = NA_Q_ROWS * GRID_W
    lc = kc.shape[0]
    variant = lambda b: jnp.where(b == 0, 0, jnp.where(b == n_blocks - 1, 2, 1))
    return pl.pallas_call(
        functools.partial(_na_kernel, rows=rows),
        grid=(heads, n_blocks),
        in_specs=[
            pl.BlockSpec((tq, HEAD_DIM), lambda h, b: (b, h)),
            pl.BlockSpec((s, HEAD_DIM), lambda h, b: (0, h)),
            pl.BlockSpec((s, HEAD_DIM), lambda h, b: (0, v_col0 + h)),
            pl.BlockSpec((lc, HEAD_DIM), lambda h, b: (0, h)),
            pl.BlockSpec((lc, HEAD_DIM), lambda h, b: (0, vc_col0 + h)),
            pl.BlockSpec((None, None, tq, NA_K_ROWS * GRID_W), lambda h, b: (h, variant(b), 0, 0)),
        ],
        out_specs=pl.BlockSpec((tq, HEAD_DIM), lambda h, b: (b, h)),
        out_shape=jax.ShapeDtypeStruct((s, bw), BF16),
        compiler_params=_cparams("arbitrary", "arbitrary"),
        name="neighbourhood_attention",
    )(q, k, v_src, kc, vc_src, bias)


def _dense_attn_kernel(q_ref, k_ref, v_ref, o_ref):
    s = lax.dot_general(q_ref[...], k_ref[...], (((1,), (1,)), ((), ())), preferred_element_type=F32)
    p = jnp.exp2(s - jnp.max(s, axis=-1, keepdims=True))
    o = jnp.dot(p.astype(BF16), v_ref[...], preferred_element_type=F32)
    o_ref[...] = (o * (1.0 / jnp.sum(p, axis=-1, keepdims=True))).astype(o_ref.dtype)


def dense_attention(q, k, v_src, v_col0):
    m, bw = q.shape
    heads = bw // HEAD_DIM
    blk = lambda c0: pl.BlockSpec((m, HEAD_DIM), lambda h: (0, c0 + h))
    return pl.pallas_call(
        _dense_attn_kernel, grid=(heads,), in_specs=[blk(0), blk(0), blk(v_col0)], out_specs=blk(0),
        out_shape=jax.ShapeDtypeStruct((m, bw), BF16),
        compiler_params=_cparams("arbitrary"), name="dense_attention",
    )(q, k, v_src)


def _merge_up_kernel(od_ref, on_ref, cb_ref, cc_ref, cx_ref, ccp_ref, cxp_ref, ccn_ref, cxn_ref, cw_ref,
                     g0_ref, g1_ref, g2_ref, wd_ref, wn_ref, wc_ref, o_ref, oc_sc):
    i, j = pl.program_id(0), pl.program_id(1)

    @pl.when(j == 0)
    def _():
        u = cc_ref[...].astype(F32) * cx_ref[...].astype(F32)
        tm = u.shape[0]
        halo = ccp_ref.shape[0]
        prev_row = ccp_ref[halo - 1:halo, :].astype(F32) * cxp_ref[halo - 1:halo, :].astype(F32)
        next_row = ccn_ref[0:1, :].astype(F32) * cxn_ref[0:1, :].astype(F32)
        prev_row = jnp.where(i == 0, 0.0, prev_row)
        next_row = jnp.where(i == pl.num_programs(0) - 1, 0.0, next_row)
        r = lax.broadcasted_iota(jnp.int32, u.shape, 0)
        u_prev = jnp.where(r == 0, prev_row, pltpu.roll(u, 1, axis=0))
        u_next = jnp.where(r == tm - 1, next_row, pltpu.roll(u, tm - 1, axis=0))
        conv = cw_ref[0:1, :] * u_prev + cw_ref[1:2, :] * u + cw_ref[2:3, :] * u_next
        oc_sc[...] = (cb_ref[...].astype(F32) * conv).astype(BF16)

    acc = _sigmoid(g0_ref[...].astype(F32)) * jnp.dot(od_ref[...], wd_ref[...], preferred_element_type=F32)
    acc += _sigmoid(g1_ref[...].astype(F32)) * jnp.dot(on_ref[...], wn_ref[...], preferred_element_type=F32)
    acc += _sigmoid(g2_ref[...].astype(F32)) * jnp.dot(oc_sc[...], wc_ref[...], preferred_element_type=F32)
    o_ref[...] = acc.astype(o_ref.dtype)


def merge_up(o_diff, o_na, p, conv_w, w_up_diff, w_up_na, w_up_conv, layer, d):
    m, bw = o_diff.shape
    tm, tn = _tile(m, 512), math.gcd(_tile(d, 1024), bw)
    halo = 8
    n_halo = m // halo
    row_blk = lambda c: pl.BlockSpec((tm, bw), lambda i, j, c=c: (i, c))
    prev_blk = lambda c: pl.BlockSpec((halo, bw), lambda i, j, c=c: (jnp.maximum(i * (tm // halo) - 1, 0), c))
    next_blk = lambda c: pl.BlockSpec((halo, bw), lambda i, j, c=c: (jnp.minimum((i + 1) * (tm // halo), n_halo - 1), c))
    gate0 = 9 * bw // tn
    gate_blk = lambda b: pl.BlockSpec((tm, tn), lambda i, j, b=b: (i, gate0 + b * (d // tn) + j))
    w_blk = pl.BlockSpec((None, bw, tn), lambda i, j: (layer, 0, j))
    assert (9 * bw) % tn == 0
    return pl.pallas_call(
        _merge_up_kernel,
        grid=(m // tm, d // tn),
        in_specs=[row_blk(0), row_blk(0), row_blk(6), row_blk(7), row_blk(8),
                  prev_blk(7), prev_blk(8), next_blk(7), next_blk(8),
                  pl.BlockSpec((CONV_K, bw), lambda i, j: (0, 0)),
                  gate_blk(0), gate_blk(1), gate_blk(2), w_blk, w_blk, w_blk],
        out_specs=pl.BlockSpec((tm, tn), lambda i, j: (i, j)),
        out_shape=jax.ShapeDtypeStruct((m, d), BF16),
        scratch_shapes=[pltpu.VMEM((tm, bw), BF16)],
        compiler_params=_cparams("arbitrary", "arbitrary"),
        name="merge_up",
    )(o_diff, o_na, p, p, p, p, p, p, p, conv_w, p, p, p, w_up_diff, w_up_na, w_up_conv)


def kernel(x, c, ctx, c_ctx, w_mod_a, w_mod_b, b_mod, norm1_g, w_in, diff_qn_g, diff_kn_g, lambda_q1, lambda_k1, lambda_q2, lambda_k2, diff_subln_g, na_qn_g, na_kn_g, na_rpb, conv_w, w_up_diff, w_up_na, w_up_conv, w_o, norm2_g, router_w, router_b, w_gu, b_gu, w_down, b_down):
    batch, s, d = x.shape
    assert batch == 1 and ctx.shape[0] == 1
    depth = w_in.shape[0]
    bw = w_up_diff.shape[1]
    dv0 = 2 * bw // (2 * HEAD_DIM)
    nv0 = 5 * bw // HEAD_DIM
    rows = s // GRID_W

    cond = jnp.zeros((8, d), F32).at[0].set(c[0]).at[1].set(c_ctx)
    mod = modulation(cond, w_mod_a, w_mod_b, b_mod).reshape(depth, 8, N_MOD, d)
    rope = rope_tables(s)
    w_in, w_o, w_gu, w_down = (w.astype(BF16) for w in (w_in, w_o, w_gu, w_down))
    ups = tuple(w.astype(BF16) for w in (w_up_diff, w_up_na, w_up_conv))

    xl, xc = x[0], ctx[0]
    for i in range(depth):
        last = i == depth - 1
        lam_init = 0.8 - 0.6 * math.exp(-0.3 * i)
        lam = (jnp.exp(jnp.sum(lambda_q1[i] * lambda_k1[i])) - jnp.exp(jnp.sum(lambda_q2[i] * lambda_k2[i]))
               + lam_init)
        ml, mc = mod[i, 0], mod[i, 1]
        gains = jnp.stack([diff_qn_g[i], diff_kn_g[i], na_qn_g[i], na_kn_g[i]])

        p_l = project_in(prenorm(xl, norm1_g[i], ml[1], ml[0]), w_in, i)
        p_c = project_in(prenorm(xc, norm1_g[i], mc[1], mc[0]), w_in, i)
        dq_l, dk_l, nq_l, nk_l = qk_prep(p_l, gains, rope, bw)
        dq_c, dk_c, nq_c, nk_c = qk_prep(p_c, gains, None, bw)

        diff_args = (dq_l, dk_l, p_l, dv0, lam, diff_subln_g[i], 1.0 - lam_init, (dk_c, p_c, dv0))
        o_diff_l = lax.cond(score_bound_log2(diff_qn_g[i], diff_kn_g[i]) <= SCORE_BOUND_LOG2,
                            lambda: diff_attention_bounded(*diff_args), lambda: diff_attention(*diff_args))
        o_na_l = neighbourhood_attention(nq_l, nk_l, p_l, nv0, nk_c, p_c, nv0,
                                         na_bias_tables(na_rpb[i], rows))
        xl = project_out(merge_up(o_diff_l, o_na_l, p_l, conv_w[i], *ups, i, d), w_o, i, xl, ml[2])
        if not last:
            o_diff_c = diff_attention(dq_c, dk_c, p_c, dv0, lam, diff_subln_g[i], 1.0 - lam_init)
            o_na_c = dense_attention(nq_c, nk_c, p_c, nv0)
            xc = project_out(merge_up(o_diff_c, o_na_c, p_c, conv_w[i], *ups, i, d), w_o, i, xc, mc[2])

        router = (router_w[i], router_b[i])
        h_l, eid, ew, rank, counts = prenorm_route(xl, norm2_g[i], ml[4], ml[3], router)
        xl = moe_routed(h_l, eid, ew, rank, counts, w_gu, b_gu[i], w_down, b_down[i], i, xl, ml[5])
        if not last:
            h_c, comb_c = prenorm(xc, norm2_g[i], mc[4], mc[3], router)
            xc = moe(h_c, comb_c, w_gu, b_gu[i], w_down, b_down[i], i, xc, mc[5])
    return xl[None]
```

```python
import functools
import math

import jax
import jax.numpy as jnp
from jax import lax
from jax.experimental import pallas as pl
from jax.experimental.pallas import tpu as pltpu

F32 = jnp.float32
BF16 = jnp.bfloat16
HIGHEST = lax.Precision.HIGHEST

GRID_W = 64
HEAD_DIM = 128
NA_WIN_R = 8
NA_WIN_C = 16
CONV_K = 3
N_BRANCH = 3
N_MOD = 6
TOP_K = 4
SWIGLU_LIMIT = 7.0
SWIGLU_ALPHA = 1.702
ROPE_BASE = 10000.0
EPS = 1e-6
LOG2E = math.log2(math.e)
Q_SCALE = HEAD_DIM ** -0.5 * LOG2E
MASKED = -1e30
SCORE_BOUND_LOG2 = 60.0

V7X_VMEM_LIMIT_BYTES = 56 * 1024 * 1024
NA_Q_ROWS = 8
NA_K_ROWS = NA_Q_ROWS + NA_WIN_R


def _cparams(*sem):
    return pltpu.CompilerParams(dimension_semantics=sem, vmem_limit_bytes=V7X_VMEM_LIMIT_BYTES)


def _tile(n, target):
    if n <= target:
        return n
    t = target
    while n % t:
        t -= 8
    assert t > 0, (n, target)
    return t


def _sigmoid(x):
    return 1.0 / (1.0 + jnp.exp(-x))


def _mod_kernel(c_ref, wa_ref, wb_ref, b_ref, o_ref):
    c = c_ref[...]
    s = c * _sigmoid(c)
    m1 = jnp.dot(s, wa_ref[...], precision=HIGHEST, preferred_element_type=F32)
    o_ref[...] = jnp.dot(m1, wb_ref[...], precision=HIGHEST, preferred_element_type=F32) + b_ref[...]


def modulation(cond, w_a, w_b, b):
    depth, d, r = w_a.shape
    width = w_b.shape[-1]
    rows = cond.shape[0]
    tn = _tile(width, 4096)
    return pl.pallas_call(
        _mod_kernel,
        grid=(depth, width // tn),
        in_specs=[
            pl.BlockSpec((rows, d), lambda l, j: (0, 0)),
            pl.BlockSpec((None, d, r), lambda l, j: (l, 0, 0)),
            pl.BlockSpec((None, r, tn), lambda l, j: (l, 0, j)),
            pl.BlockSpec((None, 1, tn), lambda l, j: (l, 0, j)),
        ],
        out_specs=pl.BlockSpec((None, rows, tn), lambda l, j: (l, 0, j)),
        out_shape=jax.ShapeDtypeStruct((depth, rows, width), F32),
        compiler_params=_cparams("arbitrary", "arbitrary"),
        name="modulation",
    )(cond, w_a, w_b, b.reshape(depth, 1, width))


def _modulated_norm(x_ref, g_ref, sc_ref, sh_ref):
    x = x_ref[...]
    y = x * lax.rsqrt(jnp.mean(x * x, axis=-1, keepdims=True) + EPS)
    return (y * g_ref[...]) * (1.0 + sc_ref[...]) + sh_ref[...]


def _prenorm_kernel(x_ref, g_ref, sc_ref, sh_ref, o_ref):
    o_ref[...] = _modulated_norm(x_ref, g_ref, sc_ref, sh_ref).astype(o_ref.dtype)


def _split_bf16(x):
    hi = x.astype(BF16)
    return hi, (x - hi.astype(F32)).astype(BF16)


def _route(h, rwh_ref, rwl_ref, rb_ref):
    h_hi, h_lo = _split_bf16(h)
    w_hi = rwh_ref[...]
    logits = (jnp.dot(h_hi, w_hi, preferred_element_type=F32) + jnp.dot(h_lo, w_hi, preferred_element_type=F32)
              + jnp.dot(h_hi, rwl_ref[...], preferred_element_type=F32) + rb_ref[...])
    n_exp = logits.shape[-1]
    lane = lax.broadcasted_iota(jnp.int32, logits.shape, 1).astype(F32)
    sels, ids, vals = [], [], []
    for _ in range(TOP_K):
        m = jnp.max(logits, axis=-1, keepdims=True)
        first = jnp.min(jnp.where(logits == m, lane, float(n_exp)), axis=-1, keepdims=True)
        sel = lane == first
        sels.append(sel)
        ids.append(first)
        vals.append(m)
        logits = jnp.where(sel, -jnp.inf, logits)
    es = [jnp.exp(v - vals[0]) for v in vals]
    inv = 1.0 / functools.reduce(lambda a, b: a + b, es)
    return sels, ids, [e * inv for e in es]


def _prenorm_router_kernel(x_ref, g_ref, sc_ref, sh_ref, rwh_ref, rwl_ref, rb_ref, o_ref, comb_ref):
    h = _modulated_norm(x_ref, g_ref, sc_ref, sh_ref)
    o_ref[...] = h.astype(o_ref.dtype)
    sels, _, wts = _route(h, rwh_ref, rwl_ref, rb_ref)
    comb = jnp.zeros(comb_ref.shape, F32)
    for sel, w in zip(sels, wts):
        comb = comb + jnp.where(sel, w, 0.0)
    comb_ref[...] = comb


def _columns(cols, width):
    lane = lax.broadcasted_iota(jnp.int32, (cols[0].shape[0], width), 1)
    out = jnp.zeros(lane.shape, cols[0].dtype)
    for k, c in enumerate(cols):
        out = jnp.where(lane == k, c, out)
    return out


def _prenorm_route_kernel(x_ref, g_ref, sc_ref, sh_ref, rwh_ref, rwl_ref, rb_ref,
                          o_ref, eid_ref, ew_ref, rank_ref, cnt_ref, carry_sc):
    @pl.when(pl.program_id(0) == 0)
    def _():
        carry_sc[...] = jnp.zeros(carry_sc.shape, F32)

    h = _modulated_norm(x_ref, g_ref, sc_ref, sh_ref)
    o_ref[...] = h.astype(o_ref.dtype)
    sels, ids, wts = _route(h, rwh_ref, rwl_ref, rb_ref)
    tm = h.shape[0]
    chosen = functools.reduce(lambda a, b: a + b, [jnp.where(s, 1.0, 0.0) for s in sels])
    r = lax.broadcasted_iota(jnp.int32, (tm, tm), 0)
    c = lax.broadcasted_iota(jnp.int32, (tm, tm), 1)
    earlier = jnp.where(c < r, 1.0, 0.0).astype(BF16)
    before = jnp.dot(earlier, chosen.astype(BF16), preferred_element_type=F32) + carry_sc[...]
    ranks = [jnp.sum(jnp.where(s, before, 0.0), axis=-1, keepdims=True) for s in sels]
    carry_sc[...] += jnp.sum(chosen, axis=0, keepdims=True)
    eid_ref[...] = _columns(ids, TOP_K).astype(jnp.int32)
    ew_ref[...] = _columns(wts, TOP_K)
    rank_ref[...] = _columns(ranks, TOP_K).astype(jnp.int32)
    cnt_ref[...] = carry_sc[...].astype(jnp.int32)


def prenorm(x, g, scale, shift, router=None):
    m, d = x.shape
    tm = _tile(m, 256)
    row = lambda i: (i, 0)
    fixed = lambda i: (0, 0)
    vec = pl.BlockSpec((1, d), fixed)
    in_specs = [pl.BlockSpec((tm, d), row), vec, vec, vec]
    args = [x, g.reshape(1, d), scale.reshape(1, d), shift.reshape(1, d)]
    if router is None:
        return pl.pallas_call(
            _prenorm_kernel, grid=(m // tm,), in_specs=in_specs,
            out_specs=pl.BlockSpec((tm, d), row),
            out_shape=jax.ShapeDtypeStruct((m, d), BF16),
            compiler_params=_cparams("arbitrary"), name="prenorm",
        )(*args)
    rw, rb = router
    n_exp = rw.shape[-1]
    in_specs += [pl.BlockSpec((d, n_exp), fixed)] * 2 + [pl.BlockSpec((1, n_exp), fixed)]
    args += [*_split_bf16(rw), rb.reshape(1, n_exp)]
    return pl.pallas_call(
        _prenorm_router_kernel, grid=(m // tm,), in_specs=in_specs,
        out_specs=[pl.BlockSpec((tm, d), row), pl.BlockSpec((tm, n_exp), row)],
        out_shape=[jax.ShapeDtypeStruct((m, d), BF16), jax.ShapeDtypeStruct((m, n_exp), F32)],
        compiler_params=_cparams("arbitrary"), name="prenorm_router",
    )(*args)


def prenorm_route(x, g, scale, shift, router):
    m, d = x.shape
    tm = _tile(m, 256)
    rw, rb = router
    n_exp = rw.shape[-1]
    row = lambda i: (i, 0)
    fixed = lambda i: (0, 0)
    vec = pl.BlockSpec((1, d), fixed)
    per_tok = pl.BlockSpec((tm, TOP_K), row)
    return pl.pallas_call(
        _prenorm_route_kernel, grid=(m // tm,),
        in_specs=[pl.BlockSpec((tm, d), row), vec, vec, vec,
                  pl.BlockSpec((d, n_exp), fixed), pl.BlockSpec((d, n_exp), fixed),
                  pl.BlockSpec((1, n_exp), fixed)],
        out_specs=[pl.BlockSpec((tm, d), row), per_tok, per_tok, per_tok, pl.BlockSpec((1, n_exp), fixed)],
        out_shape=[jax.ShapeDtypeStruct((m, d), BF16), jax.ShapeDtypeStruct((m, TOP_K), jnp.int32),
                   jax.ShapeDtypeStruct((m, TOP_K), F32), jax.ShapeDtypeStruct((m, TOP_K), jnp.int32),
                   jax.ShapeDtypeStruct((1, n_exp), jnp.int32)],
        scratch_shapes=[pltpu.VMEM((1, n_exp), F32)],
        compiler_params=_cparams("arbitrary"), name="prenorm_route",
    )(x, g.reshape(1, d), scale.reshape(1, d), shift.reshape(1, d), *_split_bf16(rw), rb.reshape(1, n_exp))


EXPERT_TILE = 256


def _expert_mlp_kernel(te_ref, nt_ref, x_ref, wgu_ref, bgu_ref, wd_ref, bd_ref, o_ref, wgu_sc, wd_sc):
    j = pl.program_id(0)
    used = j < nt_ref[0]

    @pl.when(jnp.logical_not(used))
    def _():
        o_ref[...] = jnp.zeros(o_ref.shape, o_ref.dtype)

    @pl.when(jnp.logical_and(used, jnp.logical_or(j == 0, te_ref[j] != te_ref[jnp.maximum(j - 1, 0)])))
    def _():
        wgu_sc[...] = wgu_ref[...].astype(BF16)
        wd_sc[...] = wd_ref[...].astype(BF16)

    @pl.when(used)
    def _():
        d_exp = wd_ref.shape[0]
        gu = jnp.dot(x_ref[...], wgu_sc[...], preferred_element_type=F32) + bgu_ref[...]
        gate = jnp.minimum(gu[:, :d_exp], SWIGLU_LIMIT)
        lin = jnp.clip(gu[:, d_exp:], -SWIGLU_LIMIT, SWIGLU_LIMIT)
        act = gate * _sigmoid(SWIGLU_ALPHA * gate) * (lin + 1.0)
        y = jnp.dot(act.astype(BF16), wd_sc[...], preferred_element_type=F32) + bd_ref[...]
        o_ref[...] = y.astype(o_ref.dtype)


def expert_mlp(xg, tile_expert, n_tiles_used, w_gu, b_gu, w_down, b_down, layer):
    p_rows, d = xg.shape
    _, n_exp, _, two_de = w_gu.shape
    d_exp = two_de // 2
    t = EXPERT_TILE
    grid_spec = pltpu.PrefetchScalarGridSpec(
        num_scalar_prefetch=2,
        grid=(p_rows // t,),
        in_specs=[
            pl.BlockSpec((t, d), lambda j, te, nt: (j, 0)),
            pl.BlockSpec((None, None, d, two_de), lambda j, te, nt: (layer, te[j], 0, 0)),
            pl.BlockSpec((None, 1, two_de), lambda j, te, nt: (te[j], 0, 0)),
            pl.BlockSpec((None, None, d_exp, d), lambda j, te, nt: (layer, te[j], 0, 0)),
            pl.BlockSpec((None, 1, d), lambda j, te, nt: (te[j], 0, 0)),
        ],
        out_specs=pl.BlockSpec((t, d), lambda j, te, nt: (j, 0)),
        scratch_shapes=[pltpu.VMEM((d, two_de), BF16), pltpu.VMEM((d_exp, d), BF16)],
    )
    return pl.pallas_call(
        _expert_mlp_kernel, grid_spec=grid_spec,
        out_shape=jax.ShapeDtypeStruct((p_rows, d), BF16),
        compiler_params=_cparams("arbitrary"), name="expert_mlp",
    )(tile_expert, n_tiles_used, xg, w_gu, b_gu.reshape(n_exp, 1, two_de), w_down, b_down.reshape(n_exp, 1, d))


def _combine_kernel(x_ref, gate_ref, ew_ref, *rest):
    y_refs, o_ref = rest[:-1], rest[-1]
    ew = ew_ref[...]
    acc = jnp.zeros(o_ref.shape, F32)
    for k, y_ref in enumerate(y_refs):
        acc = acc + ew[:, k:k + 1] * y_ref[...].astype(F32)
    o_ref[...] = x_ref[...] + gate_ref[...] * acc


def moe_combine(x, gate, ew, y4):
    m, d = x.shape
    tm, tn = _tile(m, 512), _tile(d, 1024)
    tile = pl.BlockSpec((tm, tn), lambda i, j: (i, j))
    y_blk = lambda k: pl.BlockSpec((None, tm, tn), lambda i, j, k=k: (k, i, j))
    return pl.pallas_call(
        _combine_kernel, grid=(m // tm, d // tn),
        in_specs=[tile, pl.BlockSpec((1, tn), lambda i, j: (0, j)), pl.BlockSpec((tm, TOP_K), lambda i, j: (i, 0))]
        + [y_blk(k) for k in range(TOP_K)],
        out_specs=tile, out_shape=jax.ShapeDtypeStruct((m, d), F32),
        compiler_params=_cparams("arbitrary", "arbitrary"), name="moe_combine",
    )(x, gate.reshape(1, d), ew, *([y4] * TOP_K))


def moe_routed(h, eid, ew, rank, counts, w_gu, b_gu, w_down, b_down, layer, x, gate):
    m, d = h.shape
    n_exp = counts.shape[-1]
    t = EXPERT_TILE
    p_rows = m * TOP_K + n_exp * t
    counts = counts.reshape(n_exp)
    tiles_per = (counts + t - 1) // t
    tile_end = jnp.cumsum(tiles_per)
    start = (tile_end - tiles_per) * t
    pos = start[eid] + rank
    tile_id = jnp.arange(p_rows // t, dtype=jnp.int32)
    tile_expert = jnp.minimum(jnp.sum((tile_end[None, :] <= tile_id[:, None]).astype(jnp.int32), axis=1),
                              n_exp - 1)
    n_used = tile_end[-1:].astype(jnp.int32)
    pos_t = pos.T.reshape(-1)
    src = jnp.zeros((p_rows,), jnp.int32).at[pos_t].set(jnp.tile(jnp.arange(m, dtype=jnp.int32), TOP_K),
                                                        unique_indices=True, mode="promise_in_bounds")
    xg = h.at[src].get(mode="promise_in_bounds")
    yg = expert_mlp(xg, tile_expert, n_used, w_gu, b_gu, w_down, b_down, layer)
    y4 = yg.at[pos_t].get(mode="promise_in_bounds", unique_indices=True).reshape(TOP_K, m, d)
    return moe_combine(x, gate, ew, y4)


def _mm_kernel(*refs, epilogue):
    a_ref, w_ref = refs[0], refs[1]
    extras, o_ref = refs[2:-1], refs[-1]
    acc = jnp.dot(a_ref[...], w_ref[...].astype(BF16), preferred_element_type=F32)
    o_ref[...] = epilogue(acc, *extras).astype(o_ref.dtype)


def matmul(a, w, w_spec, tm, tn, n_col_tiles, extras, epilogue, out_spec, out_shape, name):
    m, k = a.shape
    return pl.pallas_call(
        functools.partial(_mm_kernel, epilogue=epilogue),
        grid=(m // tm, n_col_tiles),
        in_specs=[pl.BlockSpec((tm, k), lambda i, j: (i, 0)), w_spec] + [s for _, s in extras],
        out_specs=out_spec,
        out_shape=out_shape,
        compiler_params=_cparams("arbitrary", "arbitrary"),
        name=name,
    )(a, w, *[x for x, _ in extras])


def _cast_epilogue(acc):
    return acc


def _residual_epilogue(acc, x_ref, gate_ref):
    return x_ref[...] + gate_ref[...] * acc


def _swiglu_epilogue(acc, b_ref, comb_ref):
    d_exp = acc.shape[-1] // 2
    gu = acc + b_ref[...]
    gate = jnp.minimum(gu[:, :d_exp], SWIGLU_LIMIT)
    lin = jnp.clip(gu[:, d_exp:], -SWIGLU_LIMIT, SWIGLU_LIMIT)
    act = gate * _sigmoid(SWIGLU_ALPHA * gate) * (lin + 1.0)
    comb = comb_ref[...]
    lane = lax.broadcasted_iota(jnp.int32, comb.shape, 1)
    cw = jnp.sum(jnp.where(lane == pl.program_id(1), comb, 0.0), axis=-1, keepdims=True)
    return act * cw


def _moe_down_epilogue(acc, comb_ref, bd_ref, x_ref, gate_ref):
    bias = jnp.dot(comb_ref[...].astype(BF16), bd_ref[...].astype(BF16), preferred_element_type=F32)
    return x_ref[...] + gate_ref[...] * (acc + bias)


def project_in(h, w_in, layer):
    m, d = h.shape
    width = w_in.shape[-1]
    tm, tn = _tile(m, 1024), _tile(width, 1024)
    return matmul(h, w_in, pl.BlockSpec((None, d, tn), lambda i, j: (layer, 0, j)), tm, tn, width // tn, [],
                  _cast_epilogue, pl.BlockSpec((tm, tn), lambda i, j: (i, j)),
                  jax.ShapeDtypeStruct((m, width), BF16), "project_in")


def project_out(merged, w_o, layer, x, gate):
    m, d = x.shape
    tm, tn = _tile(m, 1024), _tile(d, 512)
    tile = pl.BlockSpec((tm, tn), lambda i, j: (i, j))
    return matmul(merged, w_o, pl.BlockSpec((None, merged.shape[1], tn), lambda i, j: (layer, 0, j)),
                  tm, tn, d // tn,
                  [(x, tile), (gate.reshape(1, d), pl.BlockSpec((1, tn), lambda i, j: (0, j)))],
                  _residual_epilogue, tile, jax.ShapeDtypeStruct((m, d), F32), "project_out")


def moe(h, comb, w_gu, b_gu, w_down, b_down, layer, x, gate):
    m, d = h.shape
    depth, n_exp, _, two_de = w_gu.shape
    d_exp = two_de // 2
    tm = _tile(m, 1024)
    act = matmul(
        h, w_gu, pl.BlockSpec((None, None, d, two_de), lambda i, j: (layer, j, 0, 0)), tm, two_de, n_exp,
        [(b_gu.reshape(n_exp, 1, two_de), pl.BlockSpec((None, 1, two_de), lambda i, j: (j, 0, 0))),
         (comb, pl.BlockSpec((tm, n_exp), lambda i, j: (i, 0)))],
        _swiglu_epilogue, pl.BlockSpec((tm, d_exp), lambda i, j: (i, j)),
        jax.ShapeDtypeStruct((m, n_exp * d_exp), BF16), "moe_up")
    tm, tn = _tile(m, 512), _tile(d, 256)
    tile = pl.BlockSpec((tm, tn), lambda i, j: (i, j))
    return matmul(
        act, w_down.reshape(depth, n_exp * d_exp, d),
        pl.BlockSpec((None, n_exp * d_exp, tn), lambda i, j: (layer, 0, j)), tm, tn, d // tn,
        [(comb, pl.BlockSpec((tm, n_exp), lambda i, j: (i, 0))),
         (b_down, pl.BlockSpec((n_exp, tn), lambda i, j: (0, j))),
         (x, tile), (gate.reshape(1, d), pl.BlockSpec((1, tn), lambda i, j: (0, j)))],
        _moe_down_epilogue, tile, jax.ShapeDtypeStruct((m, d), F32), "moe_down")


def _chunk_norm(x, g):
    return x * lax.rsqrt(jnp.mean(x * x, axis=-1, keepdims=True) + EPS) * g


def _qk_prep_kernel(*refs, rope):
    if rope:
        dq_ref, dk_ref, nq_ref, nk_ref, g_ref, cos_ref, sin_ref = refs[:7]
        cos, sin = cos_ref[...], sin_ref[...]
        lane = lax.broadcasted_iota(jnp.int32, cos.shape, 1)
        lower = (lane & (HEAD_DIM // 4)) == 0
    else:
        dq_ref, dk_ref, nq_ref, nk_ref, g_ref = refs[:5]
    outs = refs[-4:]
    ins = (dq_ref, dk_ref, nq_ref, nk_ref)
    for idx in range(4):
        g = g_ref[idx:idx + 1, :]
        rot = rope and idx < 2
        scale = Q_SCALE if idx in (0, 2) else 1.0
        for c in range(ins[idx].shape[-1] // HEAD_DIM):
            sl = slice(c * HEAD_DIM, (c + 1) * HEAD_DIM)
            y = _chunk_norm(ins[idx][:, sl].astype(F32), g)
            if rot:
                swapped = jnp.where(lower, pltpu.roll(y, HEAD_DIM - HEAD_DIM // 4, axis=1),
                                    pltpu.roll(y, HEAD_DIM // 4, axis=1))
                y = y * cos + swapped * sin
            outs[idx][:, sl] = (y * scale).astype(BF16)


def qk_prep(p, gains, rope, bw):
    m = p.shape[0]
    tm = _tile(m, 512)
    col = lambda c: pl.BlockSpec((tm, bw), lambda i, c=c: (i, c))
    in_specs = [col(0), col(1), col(3), col(4), pl.BlockSpec((4, HEAD_DIM), lambda i: (0, 0))]
    args = [p, p, p, p, gains]
    if rope is not None:
        in_specs += [pl.BlockSpec((tm, HEAD_DIM), lambda i: (i, 0))] * 2
        args += list(rope)
    return pl.pallas_call(
        functools.partial(_qk_prep_kernel, rope=rope is not None),
        grid=(m // tm,), in_specs=in_specs,
        out_specs=[pl.BlockSpec((tm, bw), lambda i: (i, 0))] * 4,
        out_shape=[jax.ShapeDtypeStruct((m, bw), BF16)] * 4,
        compiler_params=_cparams("arbitrary"), name="qk_prep",
    )(*args)


def rope_tables(n):
    t = jnp.arange(n, dtype=jnp.int32)
    row = (t // GRID_W).astype(F32)
    col = (t % GRID_W).astype(F32)
    n_freq = HEAD_DIM // 4
    inv = ROPE_BASE ** (-jnp.arange(n_freq, dtype=F32) / n_freq)
    ar, ac = row[:, None] * inv, col[:, None] * inv
    ang = jnp.concatenate([ar, ar, ac, ac], axis=-1)
    lane = jnp.arange(HEAD_DIM)
    sign = jnp.where((lane & n_freq) == 0, -1.0, 1.0).astype(F32)
    return jnp.cos(ang), jnp.sin(ang) * sign


def _diff_attn_kernel(lam_ref, q_ref, k_ref, v_ref, *rest, n_main, has_extra, out_scale):
    if has_extra:
        kx_ref, vx_ref, g_ref, o_ref, m_sc, l_sc, acc_sc = rest
    else:
        g_ref, o_ref, m_sc, l_sc, acc_sc = rest
    kj = pl.program_id(2)
    n_steps = n_main + (1 if has_extra else 0)

    @pl.when(kj == 0)
    def _():
        m_sc[...] = jnp.full(m_sc.shape, -jnp.inf, F32)
        l_sc[...] = jnp.zeros(l_sc.shape, F32)
        acc_sc[...] = jnp.zeros(acc_sc.shape, F32)

    def update(kr, vr):
        v = vr[...]
        for hf in range(2):
            sl = slice(hf * HEAD_DIM, (hf + 1) * HEAD_DIM)
            s = lax.dot_general(q_ref[:, sl], kr[:, sl], (((1,), (1,)), ((), ())),
                                preferred_element_type=F32)
            m_prev = m_sc[hf]
            m_new = jnp.maximum(m_prev, jnp.max(s, axis=-1, keepdims=True))
            alpha = jnp.exp2(m_prev - m_new)
            p = jnp.exp2(s - m_new)
            l_sc[hf] = alpha * l_sc[hf] + jnp.sum(p, axis=-1, keepdims=True)
            acc_sc[hf] = alpha * acc_sc[hf] + jnp.dot(p.astype(BF16), v, preferred_element_type=F32)
            m_sc[hf] = m_new

    if has_extra:
        pl.when(kj < n_main)(lambda: update(k_ref, v_ref))
        pl.when(kj == n_main)(lambda: update(kx_ref, vx_ref))
    else:
        update(k_ref, v_ref)

    @pl.when(kj == n_steps - 1)
    def _():
        _diff_finalize(lam_ref, g_ref, o_ref, acc_sc[0], acc_sc[1], l_sc[0], l_sc[1], out_scale)


def _diff_finalize(lam_ref, g_ref, o_ref, acc0, acc1, l0, l1, out_scale):
    lam = lam_ref[0, 0]
    o = acc0 * (1.0 / l0) - lam * (acc1 * (1.0 / l1))
    y = o * lax.rsqrt(jnp.mean(o * o, axis=-1, keepdims=True) + EPS)
    o_ref[...] = (y * g_ref[...] * out_scale).astype(o_ref.dtype)


def _diff_attn_bounded_kernel(lam_ref, q_ref, k_ref, v_ref, kx_ref, vx_ref, g_ref, o_ref, l_sc, acc_sc,
                              *, tk, n_main, out_scale):
    l_sc[...] = jnp.zeros(l_sc.shape, F32)
    acc_sc[...] = jnp.zeros(acc_sc.shape, F32)

    def accumulate(k, v):
        for hf in range(2):
            sl = slice(hf * HEAD_DIM, (hf + 1) * HEAD_DIM)
            s = lax.dot_general(q_ref[:, sl], k[:, sl], (((1,), (1,)), ((), ())), preferred_element_type=F32)
            p = jnp.exp2(s)
            part = p[:, 0:HEAD_DIM]
            for c in range(1, p.shape[1] // HEAD_DIM):
                part = part + p[:, c * HEAD_DIM:(c + 1) * HEAD_DIM]
            l_sc[hf] += part
            acc_sc[hf] += jnp.dot(p.astype(BF16), v, preferred_element_type=F32)

    def body(j, carry):
        off = pl.multiple_of(j * tk, tk)
        accumulate(k_ref[pl.ds(off, tk), :], v_ref[pl.ds(off, tk), :])
        return carry

    lax.fori_loop(0, n_main, body, 0)
    accumulate(kx_ref[...], vx_ref[...])
    _diff_finalize(lam_ref, g_ref, o_ref, acc_sc[0], acc_sc[1],
                   jnp.sum(l_sc[0], axis=-1, keepdims=True), jnp.sum(l_sc[1], axis=-1, keepdims=True), out_scale)


def diff_attention_bounded(q, k, v_src, v_col0, lam, sub_g, out_scale, extra):
    mq, bw = q.shape
    mk = k.shape[0]
    hw = 2 * HEAD_DIM
    tq, tk = _tile(mq, 1024), _tile(mk, 1024)
    kx, vx_src, vx_col0 = extra
    mx = kx.shape[0]
    once = pl.Buffered(1)
    return pl.pallas_call(
        functools.partial(_diff_attn_bounded_kernel, tk=tk, n_main=mk // tk, out_scale=out_scale),
        grid=(bw // hw, mq // tq),
        in_specs=[
            pl.BlockSpec(memory_space=pltpu.SMEM),
            pl.BlockSpec((tq, hw), lambda h, i: (i, h)),
            pl.BlockSpec((mk, hw), lambda h, i: (0, h), pipeline_mode=once),
            pl.BlockSpec((mk, hw), lambda h, i: (0, v_col0 + h), pipeline_mode=once),
            pl.BlockSpec((mx, hw), lambda h, i: (0, h)),
            pl.BlockSpec((mx, hw), lambda h, i: (0, vx_col0 + h)),
            pl.BlockSpec((1, hw), lambda h, i: (0, 0)),
        ],
        out_specs=pl.BlockSpec((tq, hw), lambda h, i: (i, h)),
        out_shape=jax.ShapeDtypeStruct((mq, bw), BF16),
        scratch_shapes=[pltpu.VMEM((2, tq, HEAD_DIM), F32), pltpu.VMEM((2, tq, hw), F32)],
        compiler_params=_cparams("arbitrary", "arbitrary"),
        name="diff_attention_bounded",
    )(lam.reshape(1, 1).astype(F32), q, k, v_src, kx, vx_src, sub_g.reshape(1, hw))


def score_bound_log2(g_q, g_k):
    return Q_SCALE * HEAD_DIM * jnp.max(jnp.abs(g_q)) * jnp.max(jnp.abs(g_k)) * 1.01


def diff_attention(q, k, v_src, v_col0, lam, sub_g, out_scale, extra=None):
    mq, bw = q.shape
    mk = k.shape[0]
    hw = 2 * HEAD_DIM
    heads = bw // hw
    tq, tk = _tile(mq, 1024), _tile(mk, 1024)
    n_main = mk // tk
    last = n_main - 1
    kmap = lambda h, i, j: (jnp.minimum(j, last), h)
    in_specs = [
        pl.BlockSpec(memory_space=pltpu.SMEM),
        pl.BlockSpec((tq, hw), lambda h, i, j: (i, h)),
        pl.BlockSpec((tk, hw), kmap),
        pl.BlockSpec((tk, hw), lambda h, i, j: (jnp.minimum(j, last), v_col0 + h)),
    ]
    args = [lam.reshape(1, 1).astype(F32), q, k, v_src]
    if extra is not None:
        kx, vx_src, vx_col0 = extra
        mx = kx.shape[0]
        in_specs += [pl.BlockSpec((mx, hw), lambda h, i, j: (0, h)),
                     pl.BlockSpec((mx, hw), lambda h, i, j: (0, vx_col0 + h))]
        args += [kx, vx_src]
    in_specs.append(pl.BlockSpec((1, hw), lambda h, i, j: (0, 0)))
    args.append(sub_g.reshape(1, hw))
    return pl.pallas_call(
        functools.partial(_diff_attn_kernel, n_main=n_main, has_extra=extra is not None, out_scale=out_scale),
        grid=(heads, mq // tq, n_main + (extra is not None)),
        in_specs=in_specs,
        out_specs=pl.BlockSpec((tq, hw), lambda h, i, j: (i, h)),
        out_shape=jax.ShapeDtypeStruct((mq, bw), BF16),
        scratch_shapes=[pltpu.VMEM((2, tq, 1), F32), pltpu.VMEM((2, tq, 1), F32), pltpu.VMEM((2, tq, hw), F32)],
        compiler_params=_cparams("arbitrary", "arbitrary", "arbitrary"),
        name="diff_attention",
    )(*args)


def _na_kernel(q_ref, k_ref, v_ref, kc_ref, vc_ref, bias_ref, o_ref, *, rows, bounded):
    b = pl.program_id(1)
    start_row = jnp.clip(b * NA_Q_ROWS - NA_WIN_R // 2, 0, rows - NA_K_ROWS)
    start = pl.multiple_of(start_row * GRID_W, GRID_W * (NA_WIN_R // 2))
    n_keys = NA_K_ROWS * GRID_W
    q = q_ref[...]
    k = k_ref[pl.ds(start, n_keys), :]
    v = v_ref[pl.ds(start, n_keys), :]
    nt = (((1,), (1,)), ((), ()))
    s_loc = lax.dot_general(q, k, nt, preferred_element_type=F32) + bias_ref[...]
    s_ctx = lax.dot_general(q, kc_ref[...], nt, preferred_element_type=F32)
    if bounded:
        p_loc, p_ctx = jnp.exp2(s_loc), jnp.exp2(s_ctx)
    else:
        m = jnp.maximum(jnp.max(s_loc, axis=-1, keepdims=True), jnp.max(s_ctx, axis=-1, keepdims=True))
        p_loc, p_ctx = jnp.exp2(s_loc - m), jnp.exp2(s_ctx - m)
    denom = jnp.sum(p_loc, axis=-1, keepdims=True) + jnp.sum(p_ctx, axis=-1, keepdims=True)
    o = (jnp.dot(p_loc.astype(BF16), v, preferred_element_type=F32)
         + jnp.dot(p_ctx.astype(BF16), vc_ref[...], preferred_element_type=F32))
    o_ref[...] = (o * (1.0 / denom)).astype(o_ref.dtype)


def na_bias_tables(rpb, rows):
    n_blocks = rows // NA_Q_ROWS
    wr = min(NA_WIN_R, rows)
    w = jnp.arange(GRID_W)
    col_start = jnp.clip(w - NA_WIN_C // 2, 0, GRID_W - NA_WIN_C)
    kc = jnp.arange(GRID_W)
    valid_c = (kc[None, :] >= col_start[:, None]) & (kc[None, :] < col_start[:, None] + NA_WIN_C)
    rel_c = jnp.clip(kc[None, :] - w[:, None] + (NA_WIN_C - 1), 0, 2 * NA_WIN_C - 2)
    heads, n_rel = rpb.shape[0], rpb.shape[1]
    blocks = jnp.where(valid_c[None, None], rpb[:, :, rel_c] * LOG2E, MASKED).astype(F32)
    masked = jnp.full((heads, n_rel, GRID_W, GRID_W), MASKED, F32)
    shifted = jnp.concatenate([blocks[:, 1:], masked[:, :1]], axis=1)
    pairs = jnp.concatenate([jnp.concatenate([blocks, shifted], axis=-1),
                             jnp.concatenate([masked, blocks], axis=-1),
                             jnp.concatenate([blocks, masked], axis=-1),
                             jnp.concatenate([masked[:, :1], masked[:, :1]], axis=-1)], axis=1)
    plan = []
    for blk in (0, 1, n_blocks - 1):
        start = min(max(blk * NA_Q_ROWS - NA_WIN_R // 2, 0), rows - NA_K_ROWS)
        per_row = []
        for r_i in range(NA_Q_ROWS):
            r = blk * NA_Q_ROWS + r_i
            row_start = min(max(r - NA_WIN_R // 2, 0), rows - wr)
            rel = lambda kr: kr - r + (NA_WIN_R - 1)
            inside = lambda kr: row_start <= kr < row_start + wr
            per_pair = []
            for j in range(NA_K_ROWS // 2):
                left, right = start + 2 * j, start + 2 * j + 1
                if inside(left) and inside(right):
                    per_pair.append(rel(left))
                elif inside(right):
                    per_pair.append(n_rel + rel(right))
                elif inside(left):
                    per_pair.append(2 * n_rel + rel(left))
                else:
                    per_pair.append(3 * n_rel)
            per_row.append(per_pair)
        plan.append(per_row)
    assert 2 * GRID_W == HEAD_DIM
    tq, tkeys = NA_Q_ROWS * GRID_W, NA_K_ROWS * GRID_W
    return pl.pallas_call(
        functools.partial(_na_table_kernel, plan=plan),
        grid=(heads,),
        in_specs=[pl.BlockSpec((None, 3 * n_rel + 1, GRID_W, 2 * GRID_W), lambda h: (h, 0, 0, 0))],
        out_specs=pl.BlockSpec((None, 3, tq, tkeys), lambda h: (h, 0, 0, 0)),
        out_shape=jax.ShapeDtypeStruct((heads, 3, tq, tkeys), F32),
        compiler_params=_cparams("arbitrary"), name="na_bias_tables",
    )(pairs)


def _na_table_kernel(pairs_ref, o_ref, *, plan):
    for v, per_row in enumerate(plan):
        for r_i, per_pair in enumerate(per_row):
            for j, idx in enumerate(per_pair):
                o_ref[v, r_i * GRID_W:(r_i + 1) * GRID_W, j * 2 * GRID_W:(j + 1) * 2 * GRID_W] = pairs_ref[idx]


def neighbourhood_attention(q, k, v_src, v_col0, kc, vc_src, vc_col0, bias, bounded):
    s, bw = q.shape
    heads = bw // HEAD_DIM
    rows = s // GRID_W
    n_blocks = rows // NA_Q_ROWS
    assert n_blocks >= 3 and rows % NA_Q_ROWS == 0
    tq = NA_Q_ROWS * GRID_W
    lc = kc.shape[0]
    variant = lambda b: jnp.where(b == 0, 0, jnp.where(b == n_blocks - 1, 2, 1))
    return pl.pallas_call(
        functools.partial(_na_kernel, rows=rows, bounded=bounded),
        grid=(heads, n_blocks),
        in_specs=[
            pl.BlockSpec((tq, HEAD_DIM), lambda h, b: (b, h)),
            pl.BlockSpec((s, HEAD_DIM), lambda h, b: (0, h)),
            pl.BlockSpec((s, HEAD_DIM), lambda h, b: (0, v_col0 + h)),
            pl.BlockSpec((lc, HEAD_DIM), lambda h, b: (0, h)),
            pl.BlockSpec((lc, HEAD_DIM), lambda h, b: (0, vc_col0 + h)),
            pl.BlockSpec((None, None, tq, NA_K_ROWS * GRID_W), lambda h, b: (h, variant(b), 0, 0)),
        ],
        out_specs=pl.BlockSpec((tq, HEAD_DIM), lambda h, b: (b, h)),
        out_shape=jax.ShapeDtypeStruct((s, bw), BF16),
        compiler_params=_cparams("arbitrary", "arbitrary"),
        name="neighbourhood_attention",
    )(q, k, v_src, kc, vc_src, bias)


def _dense_attn_kernel(q_ref, k_ref, v_ref, o_ref):
    s = lax.dot_general(q_ref[...], k_ref[...], (((1,), (1,)), ((), ())), preferred_element_type=F32)
    p = jnp.exp2(s - jnp.max(s, axis=-1, keepdims=True))
    o = jnp.dot(p.astype(BF16), v_ref[...], preferred_element_type=F32)
    o_ref[...] = (o * (1.0 / jnp.sum(p, axis=-1, keepdims=True))).astype(o_ref.dtype)


def dense_attention(q, k, v_src, v_col0):
    m, bw = q.shape
    heads = bw // HEAD_DIM
    blk = lambda c0: pl.BlockSpec((m, HEAD_DIM), lambda h: (0, c0 + h))
    return pl.pallas_call(
        _dense_attn_kernel, grid=(heads,), in_specs=[blk(0), blk(0), blk(v_col0)], out_specs=blk(0),
        out_shape=jax.ShapeDtypeStruct((m, bw), BF16),
        compiler_params=_cparams("arbitrary"), name="dense_attention",
    )(q, k, v_src)


def _merge_up_kernel(od_ref, on_ref, cb_ref, cc_ref, cx_ref, ccp_ref, cxp_ref, ccn_ref, cxn_ref, cw_ref,
                     g0_ref, g1_ref, g2_ref, wd_ref, wn_ref, wc_ref, o_ref, oc_sc):
    i, j = pl.program_id(0), pl.program_id(1)

    @pl.when(j == 0)
    def _():
        u = cc_ref[...].astype(F32) * cx_ref[...].astype(F32)
        tm = u.shape[0]
        halo = ccp_ref.shape[0]
        prev_row = ccp_ref[halo - 1:halo, :].astype(F32) * cxp_ref[halo - 1:halo, :].astype(F32)
        next_row = ccn_ref[0:1, :].astype(F32) * cxn_ref[0:1, :].astype(F32)
        prev_row = jnp.where(i == 0, 0.0, prev_row)
        next_row = jnp.where(i == pl.num_programs(0) - 1, 0.0, next_row)
        r = lax.broadcasted_iota(jnp.int32, u.shape, 0)
        u_prev = jnp.where(r == 0, prev_row, pltpu.roll(u, 1, axis=0))
        u_next = jnp.where(r == tm - 1, next_row, pltpu.roll(u, tm - 1, axis=0))
        conv = cw_ref[0:1, :] * u_prev + cw_ref[1:2, :] * u + cw_ref[2:3, :] * u_next
        oc_sc[...] = (cb_ref[...].astype(F32) * conv).astype(BF16)

    acc = _sigmoid(g0_ref[...].astype(F32)) * jnp.dot(od_ref[...], wd_ref[...], preferred_element_type=F32)
    acc += _sigmoid(g1_ref[...].astype(F32)) * jnp.dot(on_ref[...], wn_ref[...], preferred_element_type=F32)
    acc += _sigmoid(g2_ref[...].astype(F32)) * jnp.dot(oc_sc[...], wc_ref[...], preferred_element_type=F32)
    o_ref[...] = acc.astype(o_ref.dtype)


def merge_up(o_diff, o_na, p, conv_w, w_up_diff, w_up_na, w_up_conv, layer, d):
    m, bw = o_diff.shape
    tm, tn = _tile(m, 512), math.gcd(_tile(d, 1024), bw)
    halo = 8
    n_halo = m // halo
    row_blk = lambda c: pl.BlockSpec((tm, bw), lambda i, j, c=c: (i, c))
    prev_blk = lambda c: pl.BlockSpec((halo, bw), lambda i, j, c=c: (jnp.maximum(i * (tm // halo) - 1, 0), c))
    next_blk = lambda c: pl.BlockSpec((halo, bw), lambda i, j, c=c: (jnp.minimum((i + 1) * (tm // halo), n_halo - 1), c))
    gate0 = 9 * bw // tn
    gate_blk = lambda b: pl.BlockSpec((tm, tn), lambda i, j, b=b: (i, gate0 + b * (d // tn) + j))
    w_blk = pl.BlockSpec((None, bw, tn), lambda i, j: (layer, 0, j))
    assert (9 * bw) % tn == 0
    return pl.pallas_call(
        _merge_up_kernel,
        grid=(m // tm, d // tn),
        in_specs=[row_blk(0), row_blk(0), row_blk(6), row_blk(7), row_blk(8),
                  prev_blk(7), prev_blk(8), next_blk(7), next_blk(8),
                  pl.BlockSpec((CONV_K, bw), lambda i, j: (0, 0)),
                  gate_blk(0), gate_blk(1), gate_blk(2), w_blk, w_blk, w_blk],
        out_specs=pl.BlockSpec((tm, tn), lambda i, j: (i, j)),
        out_shape=jax.ShapeDtypeStruct((m, d), BF16),
        scratch_shapes=[pltpu.VMEM((tm, bw), BF16)],
        compiler_params=_cparams("arbitrary", "arbitrary"),
        name="merge_up",
    )(o_diff, o_na, p, p, p, p, p, p, p, conv_w, p, p, p, w_up_diff, w_up_na, w_up_conv)


def kernel(x, c, ctx, c_ctx, w_mod_a, w_mod_b, b_mod, norm1_g, w_in, diff_qn_g, diff_kn_g, lambda_q1, lambda_k1, lambda_q2, lambda_k2, diff_subln_g, na_qn_g, na_kn_g, na_rpb, conv_w, w_up_diff, w_up_na, w_up_conv, w_o, norm2_g, router_w, router_b, w_gu, b_gu, w_down, b_down):
    batch, s, d = x.shape
    assert batch == 1 and ctx.shape[0] == 1
    depth = w_in.shape[0]
    bw = w_up_diff.shape[1]
    dv0 = 2 * bw // (2 * HEAD_DIM)
    nv0 = 5 * bw // HEAD_DIM
    rows = s // GRID_W

    cond = jnp.zeros((8, d), F32).at[0].set(c[0]).at[1].set(c_ctx)
    mod = modulation(cond, w_mod_a, w_mod_b, b_mod).reshape(depth, 8, N_MOD, d)
    rope = rope_tables(s)
    w_in, w_o = w_in.astype(BF16), w_o.astype(BF16)
    ups = tuple(w.astype(BF16) for w in (w_up_diff, w_up_na, w_up_conv))

    xl, xc = x[0], ctx[0]
    for i in range(depth):
        last = i == depth - 1
        lam_init = 0.8 - 0.6 * math.exp(-0.3 * i)
        lam = (jnp.exp(jnp.sum(lambda_q1[i] * lambda_k1[i])) - jnp.exp(jnp.sum(lambda_q2[i] * lambda_k2[i]))
               + lam_init)
        ml, mc = mod[i, 0], mod[i, 1]
        gains = jnp.stack([diff_qn_g[i], diff_kn_g[i], na_qn_g[i], na_kn_g[i]])

        p_l = project_in(prenorm(xl, norm1_g[i], ml[1], ml[0]), w_in, i)
        p_c = project_in(prenorm(xc, norm1_g[i], mc[1], mc[0]), w_in, i)
        dq_l, dk_l, nq_l, nk_l = qk_prep(p_l, gains, rope, bw)
        dq_c, dk_c, nq_c, nk_c = qk_prep(p_c, gains, None, bw)

        diff_args = (dq_l, dk_l, p_l, dv0, lam, diff_subln_g[i], 1.0 - lam_init, (dk_c, p_c, dv0))
        o_diff_l = lax.cond(score_bound_log2(diff_qn_g[i], diff_kn_g[i]) <= SCORE_BOUND_LOG2,
                            lambda: diff_attention_bounded(*diff_args), lambda: diff_attention(*diff_args))
        na_args = (nq_l, nk_l, p_l, nv0, nk_c, p_c, nv0, na_bias_tables(na_rpb[i], rows))
        na_bound = score_bound_log2(na_qn_g[i], na_kn_g[i]) + jnp.max(jnp.abs(na_rpb[i])) * LOG2E
        o_na_l = lax.cond(na_bound <= SCORE_BOUND_LOG2, lambda: neighbourhood_attention(*na_args, True),
                          lambda: neighbourhood_attention(*na_args, False))
        xl = project_out(merge_up(o_diff_l, o_na_l, p_l, conv_w[i], *ups, i, d), w_o, i, xl, ml[2])
        if not last:
            o_diff_c = diff_attention(dq_c, dk_c, p_c, dv0, lam, diff_subln_g[i], 1.0 - lam_init)
            o_na_c = dense_attention(nq_c, nk_c, p_c, nv0)
            xc = project_out(merge_up(o_diff_c, o_na_c, p_c, conv_w[i], *ups, i, d), w_o, i, xc, mc[2])

        router = (router_w[i], router_b[i])
        h_l, eid, ew, rank, counts = prenorm_route(xl, norm2_g[i], ml[4], ml[3], router)
        xl = moe_routed(h_l, eid, ew, rank, counts, w_gu, b_gu[i], w_down, b_down[i], i, xl, ml[5])
        if not last:
            h_c, comb_c = prenorm(xc, norm2_g[i], mc[4], mc[3], router)
            xc = moe(h_c, comb_c, w_gu, b_gu[i], w_down, b_down[i], i, xc, mc[5])
    return xl[None]
```

```python
import functools
import math

import jax
import jax.numpy as jnp
from jax import lax
from jax.experimental import pallas as pl
from jax.experimental.pallas import tpu as pltpu

F32 = jnp.float32
BF16 = jnp.bfloat16
HIGHEST = lax.Precision.HIGHEST

GRID_W = 64
HEAD_DIM = 128
NA_WIN_R = 8
NA_WIN_C = 16
CONV_K = 3
N_BRANCH = 3
N_MOD = 6
TOP_K = 4
SWIGLU_LIMIT = 7.0
SWIGLU_ALPHA = 1.702
ROPE_BASE = 10000.0
EPS = 1e-6
LOG2E = math.log2(math.e)
Q_SCALE = HEAD_DIM ** -0.5 * LOG2E
MASKED = -1e30
SCORE_BOUND_LOG2 = 60.0

V7X_VMEM_LIMIT_BYTES = 56 * 1024 * 1024
NA_Q_ROWS = 8
NA_K_ROWS = NA_Q_ROWS + NA_WIN_R


def _cparams(*sem):
    return pltpu.CompilerParams(dimension_semantics=sem, vmem_limit_bytes=V7X_VMEM_LIMIT_BYTES)


def _tile(n, target):
    if n <= target:
        return n
    t = target
    while n % t:
        t -= 8
    assert t > 0, (n, target)
    return t


def _sigmoid(x):
    return 1.0 / (1.0 + jnp.exp(-x))


def _mod_kernel(c_ref, wa_ref, wb_ref, b_ref, o_ref):
    c = c_ref[...]
    s = c * _sigmoid(c)
    m1 = jnp.dot(s, wa_ref[...], precision=HIGHEST, preferred_element_type=F32)
    o_ref[...] = jnp.dot(m1, wb_ref[...], precision=HIGHEST, preferred_element_type=F32) + b_ref[...]


def modulation(cond, w_a, w_b, b):
    depth, d, r = w_a.shape
    width = w_b.shape[-1]
    rows = cond.shape[0]
    tn = _tile(width, 4096)
    return pl.pallas_call(
        _mod_kernel,
        grid=(depth, width // tn),
        in_specs=[
            pl.BlockSpec((rows, d), lambda l, j: (0, 0)),
            pl.BlockSpec((None, d, r), lambda l, j: (l, 0, 0)),
            pl.BlockSpec((None, r, tn), lambda l, j: (l, 0, j)),
            pl.BlockSpec((None, 1, tn), lambda l, j: (l, 0, j)),
        ],
        out_specs=pl.BlockSpec((None, rows, tn), lambda l, j: (l, 0, j)),
        out_shape=jax.ShapeDtypeStruct((depth, rows, width), F32),
        compiler_params=_cparams("arbitrary", "arbitrary"),
        name="modulation",
    )(cond, w_a, w_b, b.reshape(depth, 1, width))


def _modulated_norm(x_ref, g_ref, sc_ref, sh_ref):
    x = x_ref[...]
    y = x * lax.rsqrt(jnp.mean(x * x, axis=-1, keepdims=True) + EPS)
    return (y * g_ref[...]) * (1.0 + sc_ref[...]) + sh_ref[...]


def _prenorm_kernel(x_ref, g_ref, sc_ref, sh_ref, o_ref):
    o_ref[...] = _modulated_norm(x_ref, g_ref, sc_ref, sh_ref).astype(o_ref.dtype)


def _split_bf16(x):
    hi = x.astype(BF16)
    return hi, (x - hi.astype(F32)).astype(BF16)


def _route(h, rwh_ref, rwl_ref, rb_ref):
    h_hi, h_lo = _split_bf16(h)
    w_hi = rwh_ref[...]
    logits = (jnp.dot(h_hi, w_hi, preferred_element_type=F32) + jnp.dot(h_lo, w_hi, preferred_element_type=F32)
              + jnp.dot(h_hi, rwl_ref[...], preferred_element_type=F32) + rb_ref[...])
    n_exp = logits.shape[-1]
    lane = lax.broadcasted_iota(jnp.int32, logits.shape, 1).astype(F32)
    sels, ids, vals = [], [], []
    for _ in range(TOP_K):
        m = jnp.max(logits, axis=-1, keepdims=True)
        first = jnp.min(jnp.where(logits == m, lane, float(n_exp)), axis=-1, keepdims=True)
        sel = lane == first
        sels.append(sel)
        ids.append(first)
        vals.append(m)
        logits = jnp.where(sel, -jnp.inf, logits)
    es = [jnp.exp(v - vals[0]) for v in vals]
    inv = 1.0 / functools.reduce(lambda a, b: a + b, es)
    return sels, ids, [e * inv for e in es]


def _prenorm_router_kernel(x_ref, g_ref, sc_ref, sh_ref, rwh_ref, rwl_ref, rb_ref, o_ref, comb_ref):
    h = _modulated_norm(x_ref, g_ref, sc_ref, sh_ref)
    o_ref[...] = h.astype(o_ref.dtype)
    sels, _, wts = _route(h, rwh_ref, rwl_ref, rb_ref)
    comb = jnp.zeros(comb_ref.shape, F32)
    for sel, w in zip(sels, wts):
        comb = comb + jnp.where(sel, w, 0.0)
    comb_ref[...] = comb


def _columns(cols, width):
    lane = lax.broadcasted_iota(jnp.int32, (cols[0].shape[0], width), 1)
    out = jnp.zeros(lane.shape, cols[0].dtype)
    for k, c in enumerate(cols):
        out = jnp.where(lane == k, c, out)
    return out


def _prenorm_route_kernel(x_ref, g_ref, sc_ref, sh_ref, rwh_ref, rwl_ref, rb_ref,
                          o_ref, eid_ref, ew_ref, rank_ref, cnt_ref, carry_sc):
    @pl.when(pl.program_id(0) == 0)
    def _():
        carry_sc[...] = jnp.zeros(carry_sc.shape, F32)

    h = _modulated_norm(x_ref, g_ref, sc_ref, sh_ref)
    o_ref[...] = h.astype(o_ref.dtype)
    sels, ids, wts = _route(h, rwh_ref, rwl_ref, rb_ref)
    tm = h.shape[0]
    chosen = functools.reduce(lambda a, b: a + b, [jnp.where(s, 1.0, 0.0) for s in sels])
    r = lax.broadcasted_iota(jnp.int32, (tm, tm), 0)
    c = lax.broadcasted_iota(jnp.int32, (tm, tm), 1)
    earlier = jnp.where(c < r, 1.0, 0.0).astype(BF16)
    before = jnp.dot(earlier, chosen.astype(BF16), preferred_element_type=F32) + carry_sc[...]
    ranks = [jnp.sum(jnp.where(s, before, 0.0), axis=-1, keepdims=True) for s in sels]
    carry_sc[...] += jnp.sum(chosen, axis=0, keepdims=True)
    eid_ref[...] = _columns(ids, TOP_K).astype(jnp.int32)
    ew_ref[...] = _columns(wts, TOP_K)
    rank_ref[...] = _columns(ranks, TOP_K).astype(jnp.int32)
    cnt_ref[...] = carry_sc[...].astype(jnp.int32)


def prenorm(x, g, scale, shift, router=None):
    m, d = x.shape
    tm = _tile(m, 256)
    row = lambda i: (i, 0)
    fixed = lambda i: (0, 0)
    vec = pl.BlockSpec((1, d), fixed)
    in_specs = [pl.BlockSpec((tm, d), row), vec, vec, vec]
    args = [x, g.reshape(1, d), scale.reshape(1, d), shift.reshape(1, d)]
    if router is None:
        return pl.pallas_call(
            _prenorm_kernel, grid=(m // tm,), in_specs=in_specs,
            out_specs=pl.BlockSpec((tm, d), row),
            out_shape=jax.ShapeDtypeStruct((m, d), BF16),
            compiler_params=_cparams("arbitrary"), name="prenorm",
        )(*args)
    rw, rb = router
    n_exp = rw.shape[-1]
    in_specs += [pl.BlockSpec((d, n_exp), fixed)] * 2 + [pl.BlockSpec((1, n_exp), fixed)]
    args += [*_split_bf16(rw), rb.reshape(1, n_exp)]
    return pl.pallas_call(
        _prenorm_router_kernel, grid=(m // tm,), in_specs=in_specs,
        out_specs=[pl.BlockSpec((tm, d), row), pl.BlockSpec((tm, n_exp), row)],
        out_shape=[jax.ShapeDtypeStruct((m, d), BF16), jax.ShapeDtypeStruct((m, n_exp), F32)],
        compiler_params=_cparams("arbitrary"), name="prenorm_router",
    )(*args)


def prenorm_route(x, g, scale, shift, router, row0, m):
    d = x.shape[1]
    tm = _tile(m, 256)
    assert row0 % tm == 0
    rw, rb = router
    n_exp = rw.shape[-1]
    row = lambda i: (i, 0)
    fixed = lambda i: (0, 0)
    vec = pl.BlockSpec((1, d), fixed)
    per_tok = pl.BlockSpec((tm, TOP_K), row)
    return pl.pallas_call(
        _prenorm_route_kernel, grid=(m // tm,),
        in_specs=[pl.BlockSpec((tm, d), lambda i: (i + row0 // tm, 0)), vec, vec, vec,
                  pl.BlockSpec((d, n_exp), fixed), pl.BlockSpec((d, n_exp), fixed),
                  pl.BlockSpec((1, n_exp), fixed)],
        out_specs=[pl.BlockSpec((tm, d), row), per_tok, per_tok, per_tok, pl.BlockSpec((1, n_exp), fixed)],
        out_shape=[jax.ShapeDtypeStruct((m, d), BF16), jax.ShapeDtypeStruct((m, TOP_K), jnp.int32),
                   jax.ShapeDtypeStruct((m, TOP_K), F32), jax.ShapeDtypeStruct((m, TOP_K), jnp.int32),
                   jax.ShapeDtypeStruct((1, n_exp), jnp.int32)],
        scratch_shapes=[pltpu.VMEM((1, n_exp), F32)],
        compiler_params=_cparams("arbitrary"), name="prenorm_route",
    )(x, g.reshape(1, d), scale.reshape(1, d), shift.reshape(1, d), *_split_bf16(rw), rb.reshape(1, n_exp))


EXPERT_TILE = 256
MOE_GROUPS = 2


def _expert_mlp_kernel(te_ref, nt_ref, x_ref, wgu_ref, bgu_ref, wd_ref, bd_ref, o_ref, wgu_sc, wd_sc):
    j = pl.program_id(0)
    used = j < nt_ref[0]

    @pl.when(jnp.logical_not(used))
    def _():
        o_ref[...] = jnp.zeros(o_ref.shape, o_ref.dtype)

    @pl.when(jnp.logical_and(used, jnp.logical_or(j == 0, te_ref[j] != te_ref[jnp.maximum(j - 1, 0)])))
    def _():
        wgu_sc[...] = wgu_ref[...].astype(BF16)
        wd_sc[...] = wd_ref[...].astype(BF16)

    @pl.when(used)
    def _():
        d_exp = wd_ref.shape[0]
        gu = jnp.dot(x_ref[...], wgu_sc[...], preferred_element_type=F32) + bgu_ref[...]
        gate = jnp.minimum(gu[:, :d_exp], SWIGLU_LIMIT)
        lin = jnp.clip(gu[:, d_exp:], -SWIGLU_LIMIT, SWIGLU_LIMIT)
        act = gate * _sigmoid(SWIGLU_ALPHA * gate) * (lin + 1.0)
        y = jnp.dot(act.astype(BF16), wd_sc[...], preferred_element_type=F32) + bd_ref[...]
        o_ref[...] = y.astype(o_ref.dtype)


def expert_mlp(xg, tile_expert, n_tiles_used, w_gu, b_gu, w_down, b_down, layer):
    p_rows, d = xg.shape
    _, n_exp, _, two_de = w_gu.shape
    d_exp = two_de // 2
    t = EXPERT_TILE
    grid_spec = pltpu.PrefetchScalarGridSpec(
        num_scalar_prefetch=2,
        grid=(p_rows // t,),
        in_specs=[
            pl.BlockSpec((t, d), lambda j, te, nt: (j, 0)),
            pl.BlockSpec((None, None, d, two_de), lambda j, te, nt: (layer, te[j], 0, 0)),
            pl.BlockSpec((None, 1, two_de), lambda j, te, nt: (te[j], 0, 0)),
            pl.BlockSpec((None, None, d_exp, d), lambda j, te, nt: (layer, te[j], 0, 0)),
            pl.BlockSpec((None, 1, d), lambda j, te, nt: (te[j], 0, 0)),
        ],
        out_specs=pl.BlockSpec((t, d), lambda j, te, nt: (j, 0)),
        scratch_shapes=[pltpu.VMEM((d, two_de), BF16), pltpu.VMEM((d_exp, d), BF16)],
    )
    return pl.pallas_call(
        _expert_mlp_kernel, grid_spec=grid_spec,
        out_shape=jax.ShapeDtypeStruct((p_rows, d), BF16),
        compiler_params=_cparams("arbitrary"), name="expert_mlp",
    )(tile_expert, n_tiles_used, xg, w_gu, b_gu.reshape(n_exp, 1, two_de), w_down, b_down.reshape(n_exp, 1, d))


def _combine_kernel(x_ref, gate_ref, ew_ref, *rest):
    y_refs, o_ref = rest[:-1], rest[-1]
    ew = ew_ref[...]
    acc = jnp.zeros(o_ref.shape, F32)
    for k, y_ref in enumerate(y_refs):
        acc = acc + ew[:, k:k + 1] * y_ref[...].astype(F32)
    o_ref[...] = x_ref[...] + gate_ref[...] * acc


def _combine_into_kernel(x_ref, gate_ref, ew_ref, *rest):
    _combine_kernel(x_ref, gate_ref, ew_ref, *rest[:TOP_K], rest[-1])


def moe_combine(x, gate, ew, y4, row0, prev):
    d = x.shape[1]
    m = ew.shape[0]
    tm, tn = _tile(m, 512), _tile(d, 1024)
    assert row0 % tm == 0
    tile = pl.BlockSpec((tm, tn), lambda i, j: (i + row0 // tm, j))
    y_blk = lambda k: pl.BlockSpec((None, tm, tn), lambda i, j, k=k: (k, i, j))
    in_specs = ([tile, pl.BlockSpec((1, tn), lambda i, j: (0, j)), pl.BlockSpec((tm, TOP_K), lambda i, j: (i, 0))]
                + [y_blk(k) for k in range(TOP_K)])
    args = [x, gate.reshape(1, d), ew] + [y4] * TOP_K
    if prev is not None:
        in_specs.append(pl.BlockSpec(memory_space=pl.ANY))
        args.append(prev)
    return pl.pallas_call(
        _combine_kernel if prev is None else _combine_into_kernel, grid=(m // tm, d // tn),
        in_specs=in_specs, out_specs=tile, out_shape=jax.ShapeDtypeStruct(x.shape, F32),
        input_output_aliases={} if prev is None else {len(args) - 1: 0},
        compiler_params=_cparams("arbitrary", "arbitrary"), name="moe_combine",
    )(*args)


def moe_routed(h, eid, ew, rank, counts, w_gu, b_gu, w_down, b_down, layer):
    m, d = h.shape
    n_exp = counts.shape[-1]
    t = EXPERT_TILE
    p_rows = m * TOP_K + n_exp * t
    counts = counts.reshape(n_exp)
    tiles_per = (counts + t - 1) // t
    tile_end = jnp.cumsum(tiles_per)
    start = (tile_end - tiles_per) * t
    pos = start[eid] + rank
    tile_id = jnp.arange(p_rows // t, dtype=jnp.int32)
    tile_expert = jnp.minimum(jnp.sum((tile_end[None, :] <= tile_id[:, None]).astype(jnp.int32), axis=1),
                              n_exp - 1)
    n_used = tile_end[-1:].astype(jnp.int32)
    pos_t = pos.T.reshape(-1)
    src = jnp.zeros((p_rows,), jnp.int32).at[pos_t].set(jnp.tile(jnp.arange(m, dtype=jnp.int32), TOP_K),
                                                        unique_indices=True, mode="promise_in_bounds")
    xg = h.at[src].get(mode="promise_in_bounds")
    yg = expert_mlp(xg, tile_expert, n_used, w_gu, b_gu, w_down, b_down, layer)
    return yg.at[pos_t].get(mode="promise_in_bounds", unique_indices=True).reshape(TOP_K, m, d)


def _mm_kernel(*refs, epilogue):
    a_ref, w_ref = refs[0], refs[1]
    extras, o_ref = refs[2:-1], refs[-1]
    acc = jnp.dot(a_ref[...], w_ref[...].astype(BF16), preferred_element_type=F32)
    o_ref[...] = epilogue(acc, *extras).astype(o_ref.dtype)


def matmul(a, w, w_spec, tm, tn, n_col_tiles, extras, epilogue, out_spec, out_shape, name):
    m, k = a.shape
    return pl.pallas_call(
        functools.partial(_mm_kernel, epilogue=epilogue),
        grid=(m // tm, n_col_tiles),
        in_specs=[pl.BlockSpec((tm, k), lambda i, j: (i, 0)), w_spec] + [s for _, s in extras],
        out_specs=out_spec,
        out_shape=out_shape,
        compiler_params=_cparams("arbitrary", "arbitrary"),
        name=name,
    )(a, w, *[x for x, _ in extras])


def _cast_epilogue(acc):
    return acc


def _residual_epilogue(acc, x_ref, gate_ref):
    return x_ref[...] + gate_ref[...] * acc


def _swiglu_epilogue(acc, b_ref, comb_ref):
    d_exp = acc.shape[-1] // 2
    gu = acc + b_ref[...]
    gate = jnp.minimum(gu[:, :d_exp], SWIGLU_LIMIT)
    lin = jnp.clip(gu[:, d_exp:], -SWIGLU_LIMIT, SWIGLU_LIMIT)
    act = gate * _sigmoid(SWIGLU_ALPHA * gate) * (lin + 1.0)
    comb = comb_ref[...]
    lane = lax.broadcasted_iota(jnp.int32, comb.shape, 1)
    cw = jnp.sum(jnp.where(lane == pl.program_id(1), comb, 0.0), axis=-1, keepdims=True)
    return act * cw


def _moe_down_epilogue(acc, comb_ref, bd_ref, x_ref, gate_ref):
    bias = jnp.dot(comb_ref[...].astype(BF16), bd_ref[...].astype(BF16), preferred_element_type=F32)
    return x_ref[...] + gate_ref[...] * (acc + bias)


def project_in(h, w_in, layer):
    m, d = h.shape
    width = w_in.shape[-1]
    tm, tn = _tile(m, 1024), _tile(width, 1024)
    return matmul(h, w_in, pl.BlockSpec((None, d, tn), lambda i, j: (layer, 0, j)), tm, tn, width // tn, [],
                  _cast_epilogue, pl.BlockSpec((tm, tn), lambda i, j: (i, j)),
                  jax.ShapeDtypeStruct((m, width), BF16), "project_in")


def project_out(merged, w_o, layer, x, gate):
    m, d = x.shape
    tm, tn = _tile(m, 1024), _tile(d, 512)
    tile = pl.BlockSpec((tm, tn), lambda i, j: (i, j))
    return matmul(merged, w_o, pl.BlockSpec((None, merged.shape[1], tn), lambda i, j: (layer, 0, j)),
                  tm, tn, d // tn,
                  [(x, tile), (gate.reshape(1, d), pl.BlockSpec((1, tn), lambda i, j: (0, j)))],
                  _residual_epilogue, tile, jax.ShapeDtypeStruct((m, d), F32), "project_out")


def moe(h, comb, w_gu, b_gu, w_down, b_down, layer, x, gate):
    m, d = h.shape
    depth, n_exp, _, two_de = w_gu.shape
    d_exp = two_de // 2
    tm = _tile(m, 1024)
    act = matmul(
        h, w_gu, pl.BlockSpec((None, None, d, two_de), lambda i, j: (layer, j, 0, 0)), tm, two_de, n_exp,
        [(b_gu.reshape(n_exp, 1, two_de), pl.BlockSpec((None, 1, two_de), lambda i, j: (j, 0, 0))),
         (comb, pl.BlockSpec((tm, n_exp), lambda i, j: (i, 0)))],
        _swiglu_epilogue, pl.BlockSpec((tm, d_exp), lambda i, j: (i, j)),
        jax.ShapeDtypeStruct((m, n_exp * d_exp), BF16), "moe_up")
    tm, tn = _tile(m, 512), _tile(d, 256)
    tile = pl.BlockSpec((tm, tn), lambda i, j: (i, j))
    return matmul(
        act, w_down.reshape(depth, n_exp * d_exp, d),
        pl.BlockSpec((None, n_exp * d_exp, tn), lambda i, j: (layer, 0, j)), tm, tn, d // tn,
        [(comb, pl.BlockSpec((tm, n_exp), lambda i, j: (i, 0))),
         (b_down, pl.BlockSpec((n_exp, tn), lambda i, j: (0, j))),
         (x, tile), (gate.reshape(1, d), pl.BlockSpec((1, tn), lambda i, j: (0, j)))],
        _moe_down_epilogue, tile, jax.ShapeDtypeStruct((m, d), F32), "moe_down")


def _chunk_norm(x, g):
    return x * lax.rsqrt(jnp.mean(x * x, axis=-1, keepdims=True) + EPS) * g


def _qk_prep_kernel(*refs, rope):
    if rope:
        dq_ref, dk_ref, nq_ref, nk_ref, g_ref, cos_ref, sin_ref = refs[:7]
        cos, sin = cos_ref[...], sin_ref[...]
        lane = lax.broadcasted_iota(jnp.int32, cos.shape, 1)
        lower = (lane & (HEAD_DIM // 4)) == 0
    else:
        dq_ref, dk_ref, nq_ref, nk_ref, g_ref = refs[:5]
    outs = refs[-4:]
    ins = (dq_ref, dk_ref, nq_ref, nk_ref)
    for idx in range(4):
        g = g_ref[idx:idx + 1, :]
        rot = rope and idx < 2
        scale = Q_SCALE if idx in (0, 2) else 1.0
        for c in range(ins[idx].shape[-1] // HEAD_DIM):
            sl = slice(c * HEAD_DIM, (c + 1) * HEAD_DIM)
            y = _chunk_norm(ins[idx][:, sl].astype(F32), g)
            if rot:
                swapped = jnp.where(lower, pltpu.roll(y, HEAD_DIM - HEAD_DIM // 4, axis=1),
                                    pltpu.roll(y, HEAD_DIM // 4, axis=1))
                y = y * cos + swapped * sin
            outs[idx][:, sl] = (y * scale).astype(BF16)


def qk_prep(p, gains, rope, bw):
    m = p.shape[0]
    tm = _tile(m, 512)
    col = lambda c: pl.BlockSpec((tm, bw), lambda i, c=c: (i, c))
    in_specs = [col(0), col(1), col(3), col(4), pl.BlockSpec((4, HEAD_DIM), lambda i: (0, 0))]
    args = [p, p, p, p, gains]
    if rope is not None:
        in_specs += [pl.BlockSpec((tm, HEAD_DIM), lambda i: (i, 0))] * 2
        args += list(rope)
    return pl.pallas_call(
        functools.partial(_qk_prep_kernel, rope=rope is not None),
        grid=(m // tm,), in_specs=in_specs,
        out_specs=[pl.BlockSpec((tm, bw), lambda i: (i, 0))] * 4,
        out_shape=[jax.ShapeDtypeStruct((m, bw), BF16)] * 4,
        compiler_params=_cparams("arbitrary"), name="qk_prep",
    )(*args)


def rope_tables(n):
    t = jnp.arange(n, dtype=jnp.int32)
    row = (t // GRID_W).astype(F32)
    col = (t % GRID_W).astype(F32)
    n_freq = HEAD_DIM // 4
    inv = ROPE_BASE ** (-jnp.arange(n_freq, dtype=F32) / n_freq)
    ar, ac = row[:, None] * inv, col[:, None] * inv
    ang = jnp.concatenate([ar, ar, ac, ac], axis=-1)
    lane = jnp.arange(HEAD_DIM)
    sign = jnp.where((lane & n_freq) == 0, -1.0, 1.0).astype(F32)
    return jnp.cos(ang), jnp.sin(ang) * sign


def _diff_attn_kernel(lam_ref, q_ref, k_ref, v_ref, *rest, n_main, has_extra, out_scale):
    if has_extra:
        kx_ref, vx_ref, g_ref, o_ref, m_sc, l_sc, acc_sc = rest
    else:
        g_ref, o_ref, m_sc, l_sc, acc_sc = rest
    kj = pl.program_id(2)
    n_steps = n_main + (1 if has_extra else 0)

    @pl.when(kj == 0)
    def _():
        m_sc[...] = jnp.full(m_sc.shape, -jnp.inf, F32)
        l_sc[...] = jnp.zeros(l_sc.shape, F32)
        acc_sc[...] = jnp.zeros(acc_sc.shape, F32)

    def update(kr, vr):
        v = vr[...]
        for hf in range(2):
            sl = slice(hf * HEAD_DIM, (hf + 1) * HEAD_DIM)
            s = lax.dot_general(q_ref[:, sl], kr[:, sl], (((1,), (1,)), ((), ())),
                                preferred_element_type=F32)
            m_prev = m_sc[hf]
            m_new = jnp.maximum(m_prev, jnp.max(s, axis=-1, keepdims=True))
            alpha = jnp.exp2(m_prev - m_new)
            p = jnp.exp2(s - m_new)
            l_sc[hf] = alpha * l_sc[hf] + jnp.sum(p, axis=-1, keepdims=True)
            acc_sc[hf] = alpha * acc_sc[hf] + jnp.dot(p.astype(BF16), v, preferred_element_type=F32)
            m_sc[hf] = m_new

    if has_extra:
        pl.when(kj < n_main)(lambda: update(k_ref, v_ref))
        pl.when(kj == n_main)(lambda: update(kx_ref, vx_ref))
    else:
        update(k_ref, v_ref)

    @pl.when(kj == n_steps - 1)
    def _():
        _diff_finalize(lam_ref, g_ref, o_ref, acc_sc[0], acc_sc[1], l_sc[0], l_sc[1], out_scale)


def _diff_finalize(lam_ref, g_ref, o_ref, acc0, acc1, l0, l1, out_scale):
    lam = lam_ref[0, 0]
    o = acc0 * (1.0 / l0) - lam * (acc1 * (1.0 / l1))
    y = o * lax.rsqrt(jnp.mean(o * o, axis=-1, keepdims=True) + EPS)
    o_ref[...] = (y * g_ref[...] * out_scale).astype(o_ref.dtype)


def _diff_attn_bounded_kernel(lam_ref, q_ref, k_ref, v_ref, kx_ref, vx_ref, g_ref, o_ref, l_sc, acc_sc,
                              *, tk, n_main, out_scale):
    l_sc[...] = jnp.zeros(l_sc.shape, F32)
    acc_sc[...] = jnp.zeros(acc_sc.shape, F32)

    def accumulate(k, v):
        for hf in range(2):
            sl = slice(hf * HEAD_DIM, (hf + 1) * HEAD_DIM)
            s = lax.dot_general(q_ref[:, sl], k[:, sl], (((1,), (1,)), ((), ())), preferred_element_type=F32)
            p = jnp.exp2(s)
            part = p[:, 0:HEAD_DIM]
            for c in range(1, p.shape[1] // HEAD_DIM):
                part = part + p[:, c * HEAD_DIM:(c + 1) * HEAD_DIM]
            l_sc[hf] += part
            acc_sc[hf] += jnp.dot(p.astype(BF16), v, preferred_element_type=F32)

    def body(j, carry):
        off = pl.multiple_of(j * tk, tk)
        accumulate(k_ref[pl.ds(off, tk), :], v_ref[pl.ds(off, tk), :])
        return carry

    lax.fori_loop(0, n_main, body, 0)
    accumulate(kx_ref[...], vx_ref[...])
    _diff_finalize(lam_ref, g_ref, o_ref, acc_sc[0], acc_sc[1],
                   jnp.sum(l_sc[0], axis=-1, keepdims=True), jnp.sum(l_sc[1], axis=-1, keepdims=True), out_scale)


def diff_attention_bounded(q, k, v_src, v_col0, lam, sub_g, out_scale, extra):
    mq, bw = q.shape
    mk = k.shape[0]
    hw = 2 * HEAD_DIM
    tq, tk = _tile(mq, 1024), _tile(mk, 1024)
    kx, vx_src, vx_col0 = extra
    mx = kx.shape[0]
    once = pl.Buffered(1)
    return pl.pallas_call(
        functools.partial(_diff_attn_bounded_kernel, tk=tk, n_main=mk // tk, out_scale=out_scale),
        grid=(bw // hw, mq // tq),
        in_specs=[
            pl.BlockSpec(memory_space=pltpu.SMEM),
            pl.BlockSpec((tq, hw), lambda h, i: (i, h)),
            pl.BlockSpec((mk, hw), lambda h, i: (0, h), pipeline_mode=once),
            pl.BlockSpec((mk, hw), lambda h, i: (0, v_col0 + h), pipeline_mode=once),
            pl.BlockSpec((mx, hw), lambda h, i: (0, h)),
            pl.BlockSpec((mx, hw), lambda h, i: (0, vx_col0 + h)),
            pl.BlockSpec((1, hw), lambda h, i: (0, 0)),
        ],
        out_specs=pl.BlockSpec((tq, hw), lambda h, i: (i, h)),
        out_shape=jax.ShapeDtypeStruct((mq, bw), BF16),
        scratch_shapes=[pltpu.VMEM((2, tq, HEAD_DIM), F32), pltpu.VMEM((2, tq, hw), F32)],
        compiler_params=_cparams("arbitrary", "arbitrary"),
        name="diff_attention_bounded",
    )(lam.reshape(1, 1).astype(F32), q, k, v_src, kx, vx_src, sub_g.reshape(1, hw))


def score_bound_log2(g_q, g_k):
    return Q_SCALE * HEAD_DIM * jnp.max(jnp.abs(g_q)) * jnp.max(jnp.abs(g_k)) * 1.01


def diff_attention(q, k, v_src, v_col0, lam, sub_g, out_scale, extra=None):
    mq, bw = q.shape
    mk = k.shape[0]
    hw = 2 * HEAD_DIM
    heads = bw // hw
    tq, tk = _tile(mq, 1024), _tile(mk, 1024)
    n_main = mk // tk
    last = n_main - 1
    kmap = lambda h, i, j: (jnp.minimum(j, last), h)
    in_specs = [
        pl.BlockSpec(memory_space=pltpu.SMEM),
        pl.BlockSpec((tq, hw), lambda h, i, j: (i, h)),
        pl.BlockSpec((tk, hw), kmap),
        pl.BlockSpec((tk, hw), lambda h, i, j: (jnp.minimum(j, last), v_col0 + h)),
    ]
    args = [lam.reshape(1, 1).astype(F32), q, k, v_src]
    if extra is not None:
        kx, vx_src, vx_col0 = extra
        mx = kx.shape[0]
        in_specs += [pl.BlockSpec((mx, hw), lambda h, i, j: (0, h)),
                     pl.BlockSpec((mx, hw), lambda h, i, j: (0, vx_col0 + h))]
        args += [kx, vx_src]
    in_specs.append(pl.BlockSpec((1, hw), lambda h, i, j: (0, 0)))
    args.append(sub_g.reshape(1, hw))
    return pl.pallas_call(
        functools.partial(_diff_attn_kernel, n_main=n_main, has_extra=extra is not None, out_scale=out_scale),
        grid=(heads, mq // tq, n_main + (extra is not None)),
        in_specs=in_specs,
        out_specs=pl.BlockSpec((tq, hw), lambda h, i, j: (i, h)),
        out_shape=jax.ShapeDtypeStruct((mq, bw), BF16),
        scratch_shapes=[pltpu.VMEM((2, tq, 1), F32), pltpu.VMEM((2, tq, 1), F32), pltpu.VMEM((2, tq, hw), F32)],
        compiler_params=_cparams("arbitrary", "arbitrary", "arbitrary"),
        name="diff_attention",
    )(*args)


def _na_kernel(q_ref, k_ref, v_ref, kc_ref, vc_ref, bias_ref, o_ref, *, rows, bounded):
    b = pl.program_id(1)
    start_row = jnp.clip(b * NA_Q_ROWS - NA_WIN_R // 2, 0, rows - NA_K_ROWS)
    start = pl.multiple_of(start_row * GRID_W, GRID_W * (NA_WIN_R // 2))
    n_keys = NA_K_ROWS * GRID_W
    q = q_ref[...]
    k = k_ref[pl.ds(start, n_keys), :]
    v = v_ref[pl.ds(start, n_keys), :]
    nt = (((1,), (1,)), ((), ()))
    s_loc = lax.dot_general(q, k, nt, preferred_element_type=F32) + bias_ref[...]
    s_ctx = lax.dot_general(q, kc_ref[...], nt, preferred_element_type=F32)
    if bounded:
        p_loc, p_ctx = jnp.exp2(s_loc), jnp.exp2(s_ctx)
    else:
        m = jnp.maximum(jnp.max(s_loc, axis=-1, keepdims=True), jnp.max(s_ctx, axis=-1, keepdims=True))
        p_loc, p_ctx = jnp.exp2(s_loc - m), jnp.exp2(s_ctx - m)
    denom = jnp.sum(p_loc, axis=-1, keepdims=True) + jnp.sum(p_ctx, axis=-1, keepdims=True)
    o = (jnp.dot(p_loc.astype(BF16), v, preferred_element_type=F32)
         + jnp.dot(p_ctx.astype(BF16), vc_ref[...], preferred_element_type=F32))
    o_ref[...] = (o * (1.0 / denom)).astype(o_ref.dtype)


def na_bias_tables(rpb, rows):
    n_blocks = rows // NA_Q_ROWS
    wr = min(NA_WIN_R, rows)
    w = jnp.arange(GRID_W)
    col_start = jnp.clip(w - NA_WIN_C // 2, 0, GRID_W - NA_WIN_C)
    kc = jnp.arange(GRID_W)
    valid_c = (kc[None, :] >= col_start[:, None]) & (kc[None, :] < col_start[:, None] + NA_WIN_C)
    rel_c = jnp.clip(kc[None, :] - w[:, None] + (NA_WIN_C - 1), 0, 2 * NA_WIN_C - 2)
    heads, n_rel = rpb.shape[0], rpb.shape[1]
    blocks = jnp.where(valid_c[None, None], rpb[:, :, rel_c] * LOG2E, MASKED).astype(F32)
    masked = jnp.full((heads, n_rel, GRID_W, GRID_W), MASKED, F32)
    shifted = jnp.concatenate([blocks[:, 1:], masked[:, :1]], axis=1)
    pairs = jnp.concatenate([jnp.concatenate([blocks, shifted], axis=-1),
                             jnp.concatenate([masked, blocks], axis=-1),
                             jnp.concatenate([blocks, masked], axis=-1),
                             jnp.concatenate([masked[:, :1], masked[:, :1]], axis=-1)], axis=1)
    plan = []
    for blk in (0, 1, n_blocks - 1):
        start = min(max(blk * NA_Q_ROWS - NA_WIN_R // 2, 0), rows - NA_K_ROWS)
        per_row = []
        for r_i in range(NA_Q_ROWS):
            r = blk * NA_Q_ROWS + r_i
            row_start = min(max(r - NA_WIN_R // 2, 0), rows - wr)
            rel = lambda kr: kr - r + (NA_WIN_R - 1)
            inside = lambda kr: row_start <= kr < row_start + wr
            per_pair = []
            for j in range(NA_K_ROWS // 2):
                left, right = start + 2 * j, start + 2 * j + 1
                if inside(left) and inside(right):
                    per_pair.append(rel(left))
                elif inside(right):
                    per_pair.append(n_rel + rel(right))
                elif inside(left):
                    per_pair.append(2 * n_rel + rel(left))
                else:
                    per_pair.append(3 * n_rel)
            per_row.append(per_pair)
        plan.append(per_row)
    assert 2 * GRID_W == HEAD_DIM
    tq, tkeys = NA_Q_ROWS * GRID_W, NA_K_ROWS * GRID_W
    return pl.pallas_call(
        functools.partial(_na_table_kernel, plan=plan),
        grid=(heads,),
        in_specs=[pl.BlockSpec((None, 3 * n_rel + 1, GRID_W, 2 * GRID_W), lambda h: (h, 0, 0, 0))],
        out_specs=pl.BlockSpec((None, 3, tq, tkeys), lambda h: (h, 0, 0, 0)),
        out_shape=jax.ShapeDtypeStruct((heads, 3, tq, tkeys), F32),
        compiler_params=_cparams("arbitrary"), name="na_bias_tables",
    )(pairs)


def _na_table_kernel(pairs_ref, o_ref, *, plan):
    for v, per_row in enumerate(plan):
        for r_i, per_pair in enumerate(per_row):
            for j, idx in enumerate(per_pair):
                o_ref[v, r_i * GRID_W:(r_i + 1) * GRID_W, j * 2 * GRID_W:(j + 1) * 2 * GRID_W] = pairs_ref[idx]


def neighbourhood_attention(q, k, v_src, v_col0, kc, vc_src, vc_col0, bias, bounded):
    s, bw = q.shape
    heads = bw // HEAD_DIM
    rows = s // GRID_W
    n_blocks = rows // NA_Q_ROWS
    assert n_blocks >= 3 and rows % NA_Q_ROWS == 0
    tq = NA_Q_ROWS * GRID_W
    lc = kc.shape[0]
    variant = lambda b: jnp.where(b == 0, 0, jnp.where(b == n_blocks - 1, 2, 1))
    return pl.pallas_call(
        functools.partial(_na_kernel, rows=rows, bounded=bounded),
        grid=(heads, n_blocks),
        in_specs=[
            pl.BlockSpec((tq, HEAD_DIM), lambda h, b: (b, h)),
            pl.BlockSpec((s, HEAD_DIM), lambda h, b: (0, h)),
            pl.BlockSpec((s, HEAD_DIM), lambda h, b: (0, v_col0 + h)),
            pl.BlockSpec((lc, HEAD_DIM), lambda h, b: (0, h)),
            pl.BlockSpec((lc, HEAD_DIM), lambda h, b: (0, vc_col0 + h)),
            pl.BlockSpec((None, None, tq, NA_K_ROWS * GRID_W), lambda h, b: (h, variant(b), 0, 0)),
        ],
        out_specs=pl.BlockSpec((tq, HEAD_DIM), lambda h, b: (b, h)),
        out_shape=jax.ShapeDtypeStruct((s, bw), BF16),
        compiler_params=_cparams("arbitrary", "arbitrary"),
        name="neighbourhood_attention",
    )(q, k, v_src, kc, vc_src, bias)


def _dense_attn_kernel(q_ref, k_ref, v_ref, o_ref):
    s = lax.dot_general(q_ref[...], k_ref[...], (((1,), (1,)), ((), ())), preferred_element_type=F32)
    p = jnp.exp2(s - jnp.max(s, axis=-1, keepdims=True))
    o = jnp.dot(p.astype(BF16), v_ref[...], preferred_element_type=F32)
    o_ref[...] = (o * (1.0 / jnp.sum(p, axis=-1, keepdims=True))).astype(o_ref.dtype)


def dense_attention(q, k, v_src, v_col0):
    m, bw = q.shape
    heads = bw // HEAD_DIM
    blk = lambda c0: pl.BlockSpec((m, HEAD_DIM), lambda h: (0, c0 + h))
    return pl.pallas_call(
        _dense_attn_kernel, grid=(heads,), in_specs=[blk(0), blk(0), blk(v_col0)], out_specs=blk(0),
        out_shape=jax.ShapeDtypeStruct((m, bw), BF16),
        compiler_params=_cparams("arbitrary"), name="dense_attention",
    )(q, k, v_src)


def _merge_up_kernel(od_ref, on_ref, cb_ref, cc_ref, cx_ref, ccp_ref, cxp_ref, ccn_ref, cxn_ref, cw_ref,
                     g0_ref, g1_ref, g2_ref, wd_ref, wn_ref, wc_ref, o_ref, oc_sc):
    i, j = pl.program_id(0), pl.program_id(1)

    @pl.when(j == 0)
    def _():
        u = cc_ref[...].astype(F32) * cx_ref[...].astype(F32)
        tm = u.shape[0]
        halo = ccp_ref.shape[0]
        prev_row = ccp_ref[halo - 1:halo, :].astype(F32) * cxp_ref[halo - 1:halo, :].astype(F32)
        next_row = ccn_ref[0:1, :].astype(F32) * cxn_ref[0:1, :].astype(F32)
        prev_row = jnp.where(i == 0, 0.0, prev_row)
        next_row = jnp.where(i == pl.num_programs(0) - 1, 0.0, next_row)
        r = lax.broadcasted_iota(jnp.int32, u.shape, 0)
        u_prev = jnp.where(r == 0, prev_row, pltpu.roll(u, 1, axis=0))
        u_next = jnp.where(r == tm - 1, next_row, pltpu.roll(u, tm - 1, axis=0))
        conv = cw_ref[0:1, :] * u_prev + cw_ref[1:2, :] * u + cw_ref[2:3, :] * u_next
        oc_sc[...] = (cb_ref[...].astype(F32) * conv).astype(BF16)

    acc = _sigmoid(g0_ref[...].astype(F32)) * jnp.dot(od_ref[...], wd_ref[...], preferred_element_type=F32)
    acc += _sigmoid(g1_ref[...].astype(F32)) * jnp.dot(on_ref[...], wn_ref[...], preferred_element_type=F32)
    acc += _sigmoid(g2_ref[...].astype(F32)) * jnp.dot(oc_sc[...], wc_ref[...], preferred_element_type=F32)
    o_ref[...] = acc.astype(o_ref.dtype)


def merge_up(o_diff, o_na, p, conv_w, w_up_diff, w_up_na, w_up_conv, layer, d):
    m, bw = o_diff.shape
    tm, tn = _tile(m, 512), math.gcd(_tile(d, 1024), bw)
    halo = 8
    n_halo = m // halo
    row_blk = lambda c: pl.BlockSpec((tm, bw), lambda i, j, c=c: (i, c))
    prev_blk = lambda c: pl.BlockSpec((halo, bw), lambda i, j, c=c: (jnp.maximum(i * (tm // halo) - 1, 0), c))
    next_blk = lambda c: pl.BlockSpec((halo, bw), lambda i, j, c=c: (jnp.minimum((i + 1) * (tm // halo), n_halo - 1), c))
    gate0 = 9 * bw // tn
    gate_blk = lambda b: pl.BlockSpec((tm, tn), lambda i, j, b=b: (i, gate0 + b * (d // tn) + j))
    w_blk = pl.BlockSpec((None, bw, tn), lambda i, j: (layer, 0, j))
    assert (9 * bw) % tn == 0
    return pl.pallas_call(
        _merge_up_kernel,
        grid=(m // tm, d // tn),
        in_specs=[row_blk(0), row_blk(0), row_blk(6), row_blk(7), row_blk(8),
                  prev_blk(7), prev_blk(8), next_blk(7), next_blk(8),
                  pl.BlockSpec((CONV_K, bw), lambda i, j: (0, 0)),
                  gate_blk(0), gate_blk(1), gate_blk(2), w_blk, w_blk, w_blk],
        out_specs=pl.BlockSpec((tm, tn), lambda i, j: (i, j)),
        out_shape=jax.ShapeDtypeStruct((m, d), BF16),
        scratch_shapes=[pltpu.VMEM((tm, bw), BF16)],
        compiler_params=_cparams("arbitrary", "arbitrary"),
        name="merge_up",
    )(o_diff, o_na, p, p, p, p, p, p, p, conv_w, p, p, p, w_up_diff, w_up_na, w_up_conv)


def kernel(x, c, ctx, c_ctx, w_mod_a, w_mod_b, b_mod, norm1_g, w_in, diff_qn_g, diff_kn_g, lambda_q1, lambda_k1, lambda_q2, lambda_k2, diff_subln_g, na_qn_g, na_kn_g, na_rpb, conv_w, w_up_diff, w_up_na, w_up_conv, w_o, norm2_g, router_w, router_b, w_gu, b_gu, w_down, b_down):
    batch, s, d = x.shape
    assert batch == 1 and ctx.shape[0] == 1
    depth = w_in.shape[0]
    bw = w_up_diff.shape[1]
    dv0 = 2 * bw // (2 * HEAD_DIM)
    nv0 = 5 * bw // HEAD_DIM
    rows = s // GRID_W

    cond = jnp.zeros((8, d), F32).at[0].set(c[0]).at[1].set(c_ctx)
    mod = modulation(cond, w_mod_a, w_mod_b, b_mod).reshape(depth, 8, N_MOD, d)
    rope = rope_tables(s)
    w_in, w_o = w_in.astype(BF16), w_o.astype(BF16)
    ups = tuple(w.astype(BF16) for w in (w_up_diff, w_up_na, w_up_conv))

    xl, xc = x[0], ctx[0]
    for i in range(depth):
        last = i == depth - 1
        lam_init = 0.8 - 0.6 * math.exp(-0.3 * i)
        lam = (jnp.exp(jnp.sum(lambda_q1[i] * lambda_k1[i])) - jnp.exp(jnp.sum(lambda_q2[i] * lambda_k2[i]))
               + lam_init)
        ml, mc = mod[i, 0], mod[i, 1]
        gains = jnp.stack([diff_qn_g[i], diff_kn_g[i], na_qn_g[i], na_kn_g[i]])

        p_l = project_in(prenorm(xl, norm1_g[i], ml[1], ml[0]), w_in, i)
        p_c = project_in(prenorm(xc, norm1_g[i], mc[1], mc[0]), w_in, i)
        dq_l, dk_l, nq_l, nk_l = qk_prep(p_l, gains, rope, bw)
        dq_c, dk_c, nq_c, nk_c = qk_prep(p_c, gains, None, bw)

        diff_args = (dq_l, dk_l, p_l, dv0, lam, diff_subln_g[i], 1.0 - lam_init, (dk_c, p_c, dv0))
        o_diff_l = lax.cond(score_bound_log2(diff_qn_g[i], diff_kn_g[i]) <= SCORE_BOUND_LOG2,
                            lambda: diff_attention_bounded(*diff_args), lambda: diff_attention(*diff_args))
        na_args = (nq_l, nk_l, p_l, nv0, nk_c, p_c, nv0, na_bias_tables(na_rpb[i], rows))
        na_bound = score_bound_log2(na_qn_g[i], na_kn_g[i]) + jnp.max(jnp.abs(na_rpb[i])) * LOG2E
        o_na_l = lax.cond(na_bound <= SCORE_BOUND_LOG2, lambda: neighbourhood_attention(*na_args, True),
                          lambda: neighbourhood_attention(*na_args, False))
        xl = project_out(merge_up(o_diff_l, o_na_l, p_l, conv_w[i], *ups, i, d), w_o, i, xl, ml[2])
        if not last:
            o_diff_c = diff_attention(dq_c, dk_c, p_c, dv0, lam, diff_subln_g[i], 1.0 - lam_init)
            o_na_c = dense_attention(nq_c, nk_c, p_c, nv0)
            xc = project_out(merge_up(o_diff_c, o_na_c, p_c, conv_w[i], *ups, i, d), w_o, i, xc, mc[2])

        router = (router_w[i], router_b[i])
        x_mid, xl, group = xl, None, s // MOE_GROUPS
        for gi in range(MOE_GROUPS):
            h_g, eid, ew, rank, counts = prenorm_route(x_mid, norm2_g[i], ml[4], ml[3], router, gi * group, group)
            y4 = moe_routed(h_g, eid, ew, rank, counts, w_gu, b_gu[i], w_down, b_down[i], i)
            xl = moe_combine(x_mid, ml[5], ew, y4, gi * group, xl)
        if not last:
            h_c, comb_c = prenorm(xc, norm2_g[i], mc[4], mc[3], router)
            xc = moe(h_c, comb_c, w_gu, b_gu[i], w_down, b_down[i], i, xc, mc[5])
    return xl[None]
```

```python
import functools
import math

import jax
import jax.numpy as jnp
from jax import lax
from jax.experimental import pallas as pl
from jax.experimental.pallas import tpu as pltpu

F32 = jnp.float32
BF16 = jnp.bfloat16
HIGHEST = lax.Precision.HIGHEST

GRID_W = 64
HEAD_DIM = 128
NA_WIN_R = 8
NA_WIN_C = 16
CONV_K = 3
N_BRANCH = 3
N_MOD = 6
TOP_K = 4
SWIGLU_LIMIT = 7.0
SWIGLU_ALPHA = 1.702
ROPE_BASE = 10000.0
EPS = 1e-6
LOG2E = math.log2(math.e)
Q_SCALE = HEAD_DIM ** -0.5 * LOG2E
MASKED = -1e30
SCORE_BOUND_LOG2 = 60.0

V7X_VMEM_LIMIT_BYTES = 56 * 1024 * 1024
NA_Q_ROWS = 8
NA_K_ROWS = NA_Q_ROWS + NA_WIN_R


def _cparams(*sem):
    return pltpu.CompilerParams(dimension_semantics=sem, vmem_limit_bytes=V7X_VMEM_LIMIT_BYTES)


def _tile(n, target):
    if n <= target:
        return n
    t = target
    while n % t:
        t -= 8
    assert t > 0, (n, target)
    return t


def _sigmoid(x):
    return 1.0 / (1.0 + jnp.exp(-x))


def _mod_kernel(c_ref, wa_ref, wb_ref, b_ref, o_ref):
    c = c_ref[...]
    s = c * _sigmoid(c)
    m1 = jnp.dot(s, wa_ref[...], precision=HIGHEST, preferred_element_type=F32)
    o_ref[...] = jnp.dot(m1, wb_ref[...], precision=HIGHEST, preferred_element_type=F32) + b_ref[...]


def modulation(cond, w_a, w_b, b):
    depth, d, r = w_a.shape
    width = w_b.shape[-1]
    rows = cond.shape[0]
    tn = _tile(width, 4096)
    return pl.pallas_call(
        _mod_kernel,
        grid=(depth, width // tn),
        in_specs=[
            pl.BlockSpec((rows, d), lambda l, j: (0, 0)),
            pl.BlockSpec((None, d, r), lambda l, j: (l, 0, 0)),
            pl.BlockSpec((None, r, tn), lambda l, j: (l, 0, j)),
            pl.BlockSpec((None, 1, tn), lambda l, j: (l, 0, j)),
        ],
        out_specs=pl.BlockSpec((None, rows, tn), lambda l, j: (l, 0, j)),
        out_shape=jax.ShapeDtypeStruct((depth, rows, width), F32),
        compiler_params=_cparams("arbitrary", "arbitrary"),
        name="modulation",
    )(cond, w_a, w_b, b.reshape(depth, 1, width))


def _modulated_norm(x_ref, g_ref, sc_ref, sh_ref):
    x = x_ref[...]
    y = x * lax.rsqrt(jnp.mean(x * x, axis=-1, keepdims=True) + EPS)
    return (y * g_ref[...]) * (1.0 + sc_ref[...]) + sh_ref[...]


def _prenorm_kernel(x_ref, g_ref, sc_ref, sh_ref, o_ref):
    o_ref[...] = _modulated_norm(x_ref, g_ref, sc_ref, sh_ref).astype(o_ref.dtype)


def _split_bf16(x):
    hi = x.astype(BF16)
    return hi, (x - hi.astype(F32)).astype(BF16)


def _route(h, rwh_ref, rwl_ref, rb_ref):
    h_hi, h_lo = _split_bf16(h)
    w_hi = rwh_ref[...]
    logits = (jnp.dot(h_hi, w_hi, preferred_element_type=F32) + jnp.dot(h_lo, w_hi, preferred_element_type=F32)
              + jnp.dot(h_hi, rwl_ref[...], preferred_element_type=F32) + rb_ref[...])
    n_exp = logits.shape[-1]
    lane = lax.broadcasted_iota(jnp.int32, logits.shape, 1).astype(F32)
    sels, ids, vals = [], [], []
    for _ in range(TOP_K):
        m = jnp.max(logits, axis=-1, keepdims=True)
        first = jnp.min(jnp.where(logits == m, lane, float(n_exp)), axis=-1, keepdims=True)
        sel = lane == first
        sels.append(sel)
        ids.append(first)
        vals.append(m)
        logits = jnp.where(sel, -jnp.inf, logits)
    es = [jnp.exp(v - vals[0]) for v in vals]
    inv = 1.0 / functools.reduce(lambda a, b: a + b, es)
    return sels, ids, [e * inv for e in es]


def _prenorm_router_kernel(x_ref, g_ref, sc_ref, sh_ref, rwh_ref, rwl_ref, rb_ref, o_ref, comb_ref):
    h = _modulated_norm(x_ref, g_ref, sc_ref, sh_ref)
    o_ref[...] = h.astype(o_ref.dtype)
    sels, _, wts = _route(h, rwh_ref, rwl_ref, rb_ref)
    comb = jnp.zeros(comb_ref.shape, F32)
    for sel, w in zip(sels, wts):
        comb = comb + jnp.where(sel, w, 0.0)
    comb_ref[...] = comb


def _columns(cols, width):
    lane = lax.broadcasted_iota(jnp.int32, (cols[0].shape[0], width), 1)
    out = jnp.zeros(lane.shape, cols[0].dtype)
    for k, c in enumerate(cols):
        out = jnp.where(lane == k, c, out)
    return out


def _prenorm_route_kernel(x_ref, g_ref, sc_ref, sh_ref, rwh_ref, rwl_ref, rb_ref,
                          o_ref, eid_ref, ew_ref, rank_ref, cnt_ref, carry_sc):
    @pl.when(pl.program_id(0) == 0)
    def _():
        carry_sc[...] = jnp.zeros(carry_sc.shape, F32)

    h = _modulated_norm(x_ref, g_ref, sc_ref, sh_ref)
    o_ref[...] = h.astype(o_ref.dtype)
    sels, ids, wts = _route(h, rwh_ref, rwl_ref, rb_ref)
    tm = h.shape[0]
    chosen = functools.reduce(lambda a, b: a + b, [jnp.where(s, 1.0, 0.0) for s in sels])
    r = lax.broadcasted_iota(jnp.int32, (tm, tm), 0)
    c = lax.broadcasted_iota(jnp.int32, (tm, tm), 1)
    earlier = jnp.where(c < r, 1.0, 0.0).astype(BF16)
    before = jnp.dot(earlier, chosen.astype(BF16), preferred_element_type=F32) + carry_sc[...]
    ranks = [jnp.sum(jnp.where(s, before, 0.0), axis=-1, keepdims=True) for s in sels]
    carry_sc[...] += jnp.sum(chosen, axis=0, keepdims=True)
    eid_ref[...] = _columns(ids, TOP_K).astype(jnp.int32)
    ew_ref[...] = _columns(wts, TOP_K)
    rank_ref[...] = _columns(ranks, TOP_K).astype(jnp.int32)
    cnt_ref[...] = carry_sc[...].astype(jnp.int32)


def prenorm(x, g, scale, shift, router=None):
    m, d = x.shape
    tm = _tile(m, 256)
    row = lambda i: (i, 0)
    fixed = lambda i: (0, 0)
    vec = pl.BlockSpec((1, d), fixed)
    in_specs = [pl.BlockSpec((tm, d), row), vec, vec, vec]
    args = [x, g.reshape(1, d), scale.reshape(1, d), shift.reshape(1, d)]
    if router is None:
        return pl.pallas_call(
            _prenorm_kernel, grid=(m // tm,), in_specs=in_specs,
            out_specs=pl.BlockSpec((tm, d), row),
            out_shape=jax.ShapeDtypeStruct((m, d), BF16),
            compiler_params=_cparams("arbitrary"), name="prenorm",
        )(*args)
    rw, rb = router
    n_exp = rw.shape[-1]
    in_specs += [pl.BlockSpec((d, n_exp), fixed)] * 2 + [pl.BlockSpec((1, n_exp), fixed)]
    args += [*_split_bf16(rw), rb.reshape(1, n_exp)]
    return pl.pallas_call(
        _prenorm_router_kernel, grid=(m // tm,), in_specs=in_specs,
        out_specs=[pl.BlockSpec((tm, d), row), pl.BlockSpec((tm, n_exp), row)],
        out_shape=[jax.ShapeDtypeStruct((m, d), BF16), jax.ShapeDtypeStruct((m, n_exp), F32)],
        compiler_params=_cparams("arbitrary"), name="prenorm_router",
    )(*args)


def prenorm_route(x, g, scale, shift, router, row0, m):
    d = x.shape[1]
    tm = _tile(m, 256)
    assert row0 % tm == 0
    rw, rb = router
    n_exp = rw.shape[-1]
    row = lambda i: (i, 0)
    fixed = lambda i: (0, 0)
    vec = pl.BlockSpec((1, d), fixed)
    per_tok = pl.BlockSpec((tm, TOP_K), row)
    return pl.pallas_call(
        _prenorm_route_kernel, grid=(m // tm,),
        in_specs=[pl.BlockSpec((tm, d), lambda i: (i + row0 // tm, 0)), vec, vec, vec,
                  pl.BlockSpec((d, n_exp), fixed), pl.BlockSpec((d, n_exp), fixed),
                  pl.BlockSpec((1, n_exp), fixed)],
        out_specs=[pl.BlockSpec((tm, d), row), per_tok, per_tok, per_tok, pl.BlockSpec((1, n_exp), fixed)],
        out_shape=[jax.ShapeDtypeStruct((m, d), BF16), jax.ShapeDtypeStruct((m, TOP_K), jnp.int32),
                   jax.ShapeDtypeStruct((m, TOP_K), F32), jax.ShapeDtypeStruct((m, TOP_K), jnp.int32),
                   jax.ShapeDtypeStruct((1, n_exp), jnp.int32)],
        scratch_shapes=[pltpu.VMEM((1, n_exp), F32)],
        compiler_params=_cparams("arbitrary"), name="prenorm_route",
    )(x, g.reshape(1, d), scale.reshape(1, d), shift.reshape(1, d), *_split_bf16(rw), rb.reshape(1, n_exp))


EXPERT_TILE = 256
MOE_GROUPS = 1


def _expert_mlp_kernel(te_ref, nt_ref, x_ref, wgu_ref, bgu_ref, wd_ref, bd_ref, o_ref, wgu_sc, wd_sc):
    j = pl.program_id(0)
    used = j < nt_ref[0]

    @pl.when(jnp.logical_not(used))
    def _():
        o_ref[...] = jnp.zeros(o_ref.shape, o_ref.dtype)

    @pl.when(jnp.logical_and(used, jnp.logical_or(j == 0, te_ref[j] != te_ref[jnp.maximum(j - 1, 0)])))
    def _():
        wgu_sc[...] = wgu_ref[...].astype(BF16)
        wd_sc[...] = wd_ref[...].astype(BF16)

    @pl.when(used)
    def _():
        d_exp = wd_ref.shape[0]
        gu = jnp.dot(x_ref[...], wgu_sc[...], preferred_element_type=F32) + bgu_ref[...]
        gate = jnp.minimum(gu[:, :d_exp], SWIGLU_LIMIT)
        lin = jnp.clip(gu[:, d_exp:], -SWIGLU_LIMIT, SWIGLU_LIMIT)
        act = gate * _sigmoid(SWIGLU_ALPHA * gate) * (lin + 1.0)
        y = jnp.dot(act.astype(BF16), wd_sc[...], preferred_element_type=F32) + bd_ref[...]
        o_ref[...] = y.astype(o_ref.dtype)


def expert_mlp(xg, tile_expert, n_tiles_used, w_gu, b_gu, w_down, b_down, layer):
    p_rows, d = xg.shape
    _, n_exp, _, two_de = w_gu.shape
    d_exp = two_de // 2
    t = EXPERT_TILE
    grid_spec = pltpu.PrefetchScalarGridSpec(
        num_scalar_prefetch=2,
        grid=(p_rows // t,),
        in_specs=[
            pl.BlockSpec((t, d), lambda j, te, nt: (j, 0)),
            pl.BlockSpec((None, None, d, two_de), lambda j, te, nt: (layer, te[j], 0, 0)),
            pl.BlockSpec((None, 1, two_de), lambda j, te, nt: (te[j], 0, 0)),
            pl.BlockSpec((None, None, d_exp, d), lambda j, te, nt: (layer, te[j], 0, 0)),
            pl.BlockSpec((None, 1, d), lambda j, te, nt: (te[j], 0, 0)),
        ],
        out_specs=pl.BlockSpec((t, d), lambda j, te, nt: (j, 0)),
        scratch_shapes=[pltpu.VMEM((d, two_de), BF16), pltpu.VMEM((d_exp, d), BF16)],
    )
    return pl.pallas_call(
        _expert_mlp_kernel, grid_spec=grid_spec,
        out_shape=jax.ShapeDtypeStruct((p_rows, d), BF16),
        compiler_params=_cparams("arbitrary"), name="expert_mlp",
    )(tile_expert, n_tiles_used, xg, w_gu, b_gu.reshape(n_exp, 1, two_de), w_down, b_down.reshape(n_exp, 1, d))


def _combine_kernel(x_ref, gate_ref, ew_ref, *rest):
    y_refs, o_ref = rest[:-1], rest[-1]
    ew = ew_ref[...]
    acc = jnp.zeros(o_ref.shape, F32)
    for k, y_ref in enumerate(y_refs):
        acc = acc + ew[:, k:k + 1] * y_ref[...].astype(F32)
    o_ref[...] = x_ref[...] + gate_ref[...] * acc


def _combine_into_kernel(x_ref, gate_ref, ew_ref, *rest):
    _combine_kernel(x_ref, gate_ref, ew_ref, *rest[:TOP_K], rest[-1])


def moe_combine(x, gate, ew, y4, row0, prev):
    d = x.shape[1]
    m = ew.shape[0]
    tm, tn = _tile(m, 512), _tile(d, 1024)
    assert row0 % tm == 0
    tile = pl.BlockSpec((tm, tn), lambda i, j: (i + row0 // tm, j))
    y_blk = lambda k: pl.BlockSpec((None, tm, tn), lambda i, j, k=k: (k, i, j))
    in_specs = ([tile, pl.BlockSpec((1, tn), lambda i, j: (0, j)), pl.BlockSpec((tm, TOP_K), lambda i, j: (i, 0))]
                + [y_blk(k) for k in range(TOP_K)])
    args = [x, gate.reshape(1, d), ew] + [y4] * TOP_K
    if prev is not None:
        in_specs.append(pl.BlockSpec(memory_space=pl.ANY))
        args.append(prev)
    return pl.pallas_call(
        _combine_kernel if prev is None else _combine_into_kernel, grid=(m // tm, d // tn),
        in_specs=in_specs, out_specs=tile, out_shape=jax.ShapeDtypeStruct(x.shape, F32),
        input_output_aliases={} if prev is None else {len(args) - 1: 0},
        compiler_params=_cparams("arbitrary", "arbitrary"), name="moe_combine",
    )(*args)


def moe_routed(h, eid, ew, rank, counts, w_gu, b_gu, w_down, b_down, layer):
    m, d = h.shape
    n_exp = counts.shape[-1]
    t = EXPERT_TILE
    p_rows = m * TOP_K + n_exp * t
    counts = counts.reshape(n_exp)
    tiles_per = (counts + t - 1) // t
    tile_end = jnp.cumsum(tiles_per)
    start = (tile_end - tiles_per) * t
    pos = start[eid] + rank
    tile_id = jnp.arange(p_rows // t, dtype=jnp.int32)
    tile_expert = jnp.minimum(jnp.sum((tile_end[None, :] <= tile_id[:, None]).astype(jnp.int32), axis=1),
                              n_exp - 1)
    n_used = tile_end[-1:].astype(jnp.int32)
    pos_t = pos.T.reshape(-1)
    src = (jnp.arange(p_rows, dtype=jnp.int32) % m).at[pos_t].set(
        jnp.tile(jnp.arange(m, dtype=jnp.int32), TOP_K), unique_indices=True, mode="promise_in_bounds")
    xg = h.at[src].get(mode="promise_in_bounds")
    yg = expert_mlp(xg, tile_expert, n_used, w_gu, b_gu, w_down, b_down, layer)
    return yg.at[pos_t].get(mode="promise_in_bounds", unique_indices=True).reshape(TOP_K, m, d)


def _mm_kernel(*refs, epilogue):
    a_ref, w_ref = refs[0], refs[1]
    extras, o_ref = refs[2:-1], refs[-1]
    acc = jnp.dot(a_ref[...], w_ref[...].astype(BF16), preferred_element_type=F32)
    o_ref[...] = epilogue(acc, *extras).astype(o_ref.dtype)


def matmul(a, w, w_spec, tm, tn, n_col_tiles, extras, epilogue, out_spec, out_shape, name):
    m, k = a.shape
    return pl.pallas_call(
        functools.partial(_mm_kernel, epilogue=epilogue),
        grid=(m // tm, n_col_tiles),
        in_specs=[pl.BlockSpec((tm, k), lambda i, j: (i, 0)), w_spec] + [s for _, s in extras],
        out_specs=out_spec,
        out_shape=out_shape,
        compiler_params=_cparams("arbitrary", "arbitrary"),
        name=name,
    )(a, w, *[x for x, _ in extras])


def _cast_epilogue(acc):
    return acc


def _residual_epilogue(acc, x_ref, gate_ref):
    return x_ref[...] + gate_ref[...] * acc


def _swiglu_epilogue(acc, b_ref, comb_ref):
    d_exp = acc.shape[-1] // 2
    gu = acc + b_ref[...]
    gate = jnp.minimum(gu[:, :d_exp], SWIGLU_LIMIT)
    lin = jnp.clip(gu[:, d_exp:], -SWIGLU_LIMIT, SWIGLU_LIMIT)
    act = gate * _sigmoid(SWIGLU_ALPHA * gate) * (lin + 1.0)
    comb = comb_ref[...]
    lane = lax.broadcasted_iota(jnp.int32, comb.shape, 1)
    cw = jnp.sum(jnp.where(lane == pl.program_id(1), comb, 0.0), axis=-1, keepdims=True)
    return act * cw


def _moe_down_epilogue(acc, comb_ref, bd_ref, x_ref, gate_ref):
    bias = jnp.dot(comb_ref[...].astype(BF16), bd_ref[...].astype(BF16), preferred_element_type=F32)
    return x_ref[...] + gate_ref[...] * (acc + bias)


def project_in(h, w_in, layer):
    m, d = h.shape
    width = w_in.shape[-1]
    tm, tn = _tile(m, 1024), _tile(width, 1024)
    return matmul(h, w_in, pl.BlockSpec((None, d, tn), lambda i, j: (layer, 0, j)), tm, tn, width // tn, [],
                  _cast_epilogue, pl.BlockSpec((tm, tn), lambda i, j: (i, j)),
                  jax.ShapeDtypeStruct((m, width), BF16), "project_in")


def project_out(merged, w_o, layer, x, gate):
    m, d = x.shape
    tm, tn = _tile(m, 1024), _tile(d, 512)
    tile = pl.BlockSpec((tm, tn), lambda i, j: (i, j))
    return matmul(merged, w_o, pl.BlockSpec((None, merged.shape[1], tn), lambda i, j: (layer, 0, j)),
                  tm, tn, d // tn,
                  [(x, tile), (gate.reshape(1, d), pl.BlockSpec((1, tn), lambda i, j: (0, j)))],
                  _residual_epilogue, tile, jax.ShapeDtypeStruct((m, d), F32), "project_out")


def moe(h, comb, w_gu, b_gu, w_down, b_down, layer, x, gate):
    m, d = h.shape
    depth, n_exp, _, two_de = w_gu.shape
    d_exp = two_de // 2
    tm = _tile(m, 1024)
    act = matmul(
        h, w_gu, pl.BlockSpec((None, None, d, two_de), lambda i, j: (layer, j, 0, 0)), tm, two_de, n_exp,
        [(b_gu.reshape(n_exp, 1, two_de), pl.BlockSpec((None, 1, two_de), lambda i, j: (j, 0, 0))),
         (comb, pl.BlockSpec((tm, n_exp), lambda i, j: (i, 0)))],
        _swiglu_epilogue, pl.BlockSpec((tm, d_exp), lambda i, j: (i, j)),
        jax.ShapeDtypeStruct((m, n_exp * d_exp), BF16), "moe_up")
    tm, tn = _tile(m, 512), _tile(d, 256)
    tile = pl.BlockSpec((tm, tn), lambda i, j: (i, j))
    return matmul(
        act, w_down.reshape(depth, n_exp * d_exp, d),
        pl.BlockSpec((None, n_exp * d_exp, tn), lambda i, j: (layer, 0, j)), tm, tn, d // tn,
        [(comb, pl.BlockSpec((tm, n_exp), lambda i, j: (i, 0))),
         (b_down, pl.BlockSpec((n_exp, tn), lambda i, j: (0, j))),
         (x, tile), (gate.reshape(1, d), pl.BlockSpec((1, tn), lambda i, j: (0, j)))],
        _moe_down_epilogue, tile, jax.ShapeDtypeStruct((m, d), F32), "moe_down")


def _chunk_norm(x, g):
    return x * lax.rsqrt(jnp.mean(x * x, axis=-1, keepdims=True) + EPS) * g


def _qk_prep_kernel(*refs, rope):
    if rope:
        dq_ref, dk_ref, nq_ref, nk_ref, g_ref, cos_ref, sin_ref = refs[:7]
        cos, sin = cos_ref[...], sin_ref[...]
        lane = lax.broadcasted_iota(jnp.int32, cos.shape, 1)
        lower = (lane & (HEAD_DIM // 4)) == 0
    else:
        dq_ref, dk_ref, nq_ref, nk_ref, g_ref = refs[:5]
    outs = refs[-4:]
    ins = (dq_ref, dk_ref, nq_ref, nk_ref)
    for idx in range(4):
        g = g_ref[idx:idx + 1, :]
        rot = rope and idx < 2
        scale = Q_SCALE if idx in (0, 2) else 1.0
        for c in range(ins[idx].shape[-1] // HEAD_DIM):
            sl = slice(c * HEAD_DIM, (c + 1) * HEAD_DIM)
            y = _chunk_norm(ins[idx][:, sl].astype(F32), g)
            if rot:
                swapped = jnp.where(lower, pltpu.roll(y, HEAD_DIM - HEAD_DIM // 4, axis=1),
                                    pltpu.roll(y, HEAD_DIM // 4, axis=1))
                y = y * cos + swapped * sin
            outs[idx][:, sl] = (y * scale).astype(BF16)


def qk_prep(p, gains, rope, bw):
    m = p.shape[0]
    tm = _tile(m, 512)
    col = lambda c: pl.BlockSpec((tm, bw), lambda i, c=c: (i, c))
    in_specs = [col(0), col(1), col(3), col(4), pl.BlockSpec((4, HEAD_DIM), lambda i: (0, 0))]
    args = [p, p, p, p, gains]
    if rope is not None:
        in_specs += [pl.BlockSpec((tm, HEAD_DIM), lambda i: (i, 0))] * 2
        args += list(rope)
    return pl.pallas_call(
        functools.partial(_qk_prep_kernel, rope=rope is not None),
        grid=(m // tm,), in_specs=in_specs,
        out_specs=[pl.BlockSpec((tm, bw), lambda i: (i, 0))] * 4,
        out_shape=[jax.ShapeDtypeStruct((m, bw), BF16)] * 4,
        compiler_params=_cparams("arbitrary"), name="qk_prep",
    )(*args)


def rope_tables(n):
    t = jnp.arange(n, dtype=jnp.int32)
    row = (t // GRID_W).astype(F32)
    col = (t % GRID_W).astype(F32)
    n_freq = HEAD_DIM // 4
    inv = ROPE_BASE ** (-jnp.arange(n_freq, dtype=F32) / n_freq)
    ar, ac = row[:, None] * inv, col[:, None] * inv
    ang = jnp.concatenate([ar, ar, ac, ac], axis=-1)
    lane = jnp.arange(HEAD_DIM)
    sign = jnp.where((lane & n_freq) == 0, -1.0, 1.0).astype(F32)
    return jnp.cos(ang), jnp.sin(ang) * sign


def _diff_attn_kernel(lam_ref, q_ref, k_ref, v_ref, *rest, n_main, has_extra, out_scale):
    if has_extra:
        kx_ref, vx_ref, g_ref, o_ref, m_sc, l_sc, acc_sc = rest
    else:
        g_ref, o_ref, m_sc, l_sc, acc_sc = rest
    kj = pl.program_id(2)
    n_steps = n_main + (1 if has_extra else 0)

    @pl.when(kj == 0)
    def _():
        m_sc[...] = jnp.full(m_sc.shape, -jnp.inf, F32)
        l_sc[...] = jnp.zeros(l_sc.shape, F32)
        acc_sc[...] = jnp.zeros(acc_sc.shape, F32)

    def update(kr, vr):
        v = vr[...]
        for hf in range(2):
            sl = slice(hf * HEAD_DIM, (hf + 1) * HEAD_DIM)
            s = lax.dot_general(q_ref[:, sl], kr[:, sl], (((1,), (1,)), ((), ())),
                                preferred_element_type=F32)
            m_prev = m_sc[hf]
            m_new = jnp.maximum(m_prev, jnp.max(s, axis=-1, keepdims=True))
            alpha = jnp.exp2(m_prev - m_new)
            p = jnp.exp2(s - m_new)
            l_sc[hf] = alpha * l_sc[hf] + jnp.sum(p, axis=-1, keepdims=True)
            acc_sc[hf] = alpha * acc_sc[hf] + jnp.dot(p.astype(BF16), v, preferred_element_type=F32)
            m_sc[hf] = m_new

    if has_extra:
        pl.when(kj < n_main)(lambda: update(k_ref, v_ref))
        pl.when(kj == n_main)(lambda: update(kx_ref, vx_ref))
    else:
        update(k_ref, v_ref)

    @pl.when(kj == n_steps - 1)
    def _():
        _diff_finalize(lam_ref, g_ref, o_ref, acc_sc[0], acc_sc[1], l_sc[0], l_sc[1], out_scale)


def _diff_finalize(lam_ref, g_ref, o_ref, acc0, acc1, l0, l1, out_scale):
    lam = lam_ref[0, 0]
    o = acc0 * (1.0 / l0) - lam * (acc1 * (1.0 / l1))
    y = o * lax.rsqrt(jnp.mean(o * o, axis=-1, keepdims=True) + EPS)
    o_ref[...] = (y * g_ref[...] * out_scale).astype(o_ref.dtype)


def _diff_attn_bounded_kernel(lam_ref, q_ref, k_ref, v_ref, kx_ref, vx_ref, g_ref, o_ref, l_sc, acc_sc,
                              *, tk, n_main, out_scale):
    l_sc[...] = jnp.zeros(l_sc.shape, F32)
    acc_sc[...] = jnp.zeros(acc_sc.shape, F32)

    def accumulate(k, v):
        for hf in range(2):
            sl = slice(hf * HEAD_DIM, (hf + 1) * HEAD_DIM)
            s = lax.dot_general(q_ref[:, sl], k[:, sl], (((1,), (1,)), ((), ())), preferred_element_type=F32)
            p = jnp.exp2(s)
            part = p[:, 0:HEAD_DIM]
            for c in range(1, p.shape[1] // HEAD_DIM):
                part = part + p[:, c * HEAD_DIM:(c + 1) * HEAD_DIM]
            l_sc[hf] += part
            acc_sc[hf] += jnp.dot(p.astype(BF16), v, preferred_element_type=F32)

    def body(j, carry):
        off = pl.multiple_of(j * tk, tk)
        accumulate(k_ref[pl.ds(off, tk), :], v_ref[pl.ds(off, tk), :])
        return carry

    lax.fori_loop(0, n_main, body, 0)
    accumulate(kx_ref[...], vx_ref[...])
    _diff_finalize(lam_ref, g_ref, o_ref, acc_sc[0], acc_sc[1],
                   jnp.sum(l_sc[0], axis=-1, keepdims=True), jnp.sum(l_sc[1], axis=-1, keepdims=True), out_scale)


def diff_attention_bounded(q, k, v_src, v_col0, lam, sub_g, out_scale, extra):
    mq, bw = q.shape
    mk = k.shape[0]
    hw = 2 * HEAD_DIM
    tq, tk = _tile(mq, 1024), _tile(mk, 1024)
    kx, vx_src, vx_col0 = extra
    mx = kx.shape[0]
    once = pl.Buffered(1)
    return pl.pallas_call(
        functools.partial(_diff_attn_bounded_kernel, tk=tk, n_main=mk // tk, out_scale=out_scale),
        grid=(bw // hw, mq // tq),
        in_specs=[
            pl.BlockSpec(memory_space=pltpu.SMEM),
            pl.BlockSpec((tq, hw), lambda h, i: (i, h)),
            pl.BlockSpec((mk, hw), lambda h, i: (0, h), pipeline_mode=once),
            pl.BlockSpec((mk, hw), lambda h, i: (0, v_col0 + h), pipeline_mode=once),
            pl.BlockSpec((mx, hw), lambda h, i: (0, h)),
            pl.BlockSpec((mx, hw), lambda h, i: (0, vx_col0 + h)),
            pl.BlockSpec((1, hw), lambda h, i: (0, 0)),
        ],
        out_specs=pl.BlockSpec((tq, hw), lambda h, i: (i, h)),
        out_shape=jax.ShapeDtypeStruct((mq, bw), BF16),
        scratch_shapes=[pltpu.VMEM((2, tq, HEAD_DIM), F32), pltpu.VMEM((2, tq, hw), F32)],
        compiler_params=_cparams("arbitrary", "arbitrary"),
        name="diff_attention_bounded",
    )(lam.reshape(1, 1).astype(F32), q, k, v_src, kx, vx_src, sub_g.reshape(1, hw))


def score_bound_log2(g_q, g_k):
    return Q_SCALE * HEAD_DIM * jnp.max(jnp.abs(g_q)) * jnp.max(jnp.abs(g_k)) * 1.01


def diff_attention(q, k, v_src, v_col0, lam, sub_g, out_scale, extra=None):
    mq, bw = q.shape
    mk = k.shape[0]
    hw = 2 * HEAD_DIM
    heads = bw // hw
    tq, tk = _tile(mq, 1024), _tile(mk, 1024)
    n_main = mk // tk
    last = n_main - 1
    kmap = lambda h, i, j: (jnp.minimum(j, last), h)
    in_specs = [
        pl.BlockSpec(memory_space=pltpu.SMEM),
        pl.BlockSpec((tq, hw), lambda h, i, j: (i, h)),
        pl.BlockSpec((tk, hw), kmap),
        pl.BlockSpec((tk, hw), lambda h, i, j: (jnp.minimum(j, last), v_col0 + h)),
    ]
    args = [lam.reshape(1, 1).astype(F32), q, k, v_src]
    if extra is not None:
        kx, vx_src, vx_col0 = extra
        mx = kx.shape[0]
        in_specs += [pl.BlockSpec((mx, hw), lambda h, i, j: (0, h)),
                     pl.BlockSpec((mx, hw), lambda h, i, j: (0, vx_col0 + h))]
        args += [kx, vx_src]
    in_specs.append(pl.BlockSpec((1, hw), lambda h, i, j: (0, 0)))
    args.append(sub_g.reshape(1, hw))
    return pl.pallas_call(
        functools.partial(_diff_attn_kernel, n_main=n_main, has_extra=extra is not None, out_scale=out_scale),
        grid=(heads, mq // tq, n_main + (extra is not None)),
        in_specs=in_specs,
        out_specs=pl.BlockSpec((tq, hw), lambda h, i, j: (i, h)),
        out_shape=jax.ShapeDtypeStruct((mq, bw), BF16),
        scratch_shapes=[pltpu.VMEM((2, tq, 1), F32), pltpu.VMEM((2, tq, 1), F32), pltpu.VMEM((2, tq, hw), F32)],
        compiler_params=_cparams("arbitrary", "arbitrary", "arbitrary"),
        name="diff_attention",
    )(*args)


def _na_kernel(q_ref, k_ref, v_ref, kc_ref, vc_ref, bias_ref, o_ref, *, rows, bounded):
    b = pl.program_id(1)
    start_row = jnp.clip(b * NA_Q_ROWS - NA_WIN_R // 2, 0, rows - NA_K_ROWS)
    start = pl.multiple_of(start_row * GRID_W, GRID_W * (NA_WIN_R // 2))
    n_keys = NA_K_ROWS * GRID_W
    q = q_ref[...]
    k = k_ref[pl.ds(start, n_keys), :]
    v = v_ref[pl.ds(start, n_keys), :]
    nt = (((1,), (1,)), ((), ()))
    s_loc = lax.dot_general(q, k, nt, preferred_element_type=F32) + bias_ref[...]
    s_ctx = lax.dot_general(q, kc_ref[...], nt, preferred_element_type=F32)
    if bounded:
        p_loc, p_ctx = jnp.exp2(s_loc), jnp.exp2(s_ctx)
    else:
        m = jnp.maximum(jnp.max(s_loc, axis=-1, keepdims=True), jnp.max(s_ctx, axis=-1, keepdims=True))
        p_loc, p_ctx = jnp.exp2(s_loc - m), jnp.exp2(s_ctx - m)
    denom = jnp.sum(p_loc, axis=-1, keepdims=True) + jnp.sum(p_ctx, axis=-1, keepdims=True)
    o = (jnp.dot(p_loc.astype(BF16), v, preferred_element_type=F32)
         + jnp.dot(p_ctx.astype(BF16), vc_ref[...], preferred_element_type=F32))
    o_ref[...] = (o * (1.0 / denom)).astype(o_ref.dtype)


def na_bias_tables(rpb, rows):
    n_blocks = rows // NA_Q_ROWS
    wr = min(NA_WIN_R, rows)
    w = jnp.arange(GRID_W)
    col_start = jnp.clip(w - NA_WIN_C // 2, 0, GRID_W - NA_WIN_C)
    kc = jnp.arange(GRID_W)
    valid_c = (kc[None, :] >= col_start[:, None]) & (kc[None, :] < col_start[:, None] + NA_WIN_C)
    rel_c = jnp.clip(kc[None, :] - w[:, None] + (NA_WIN_C - 1), 0, 2 * NA_WIN_C - 2)
    heads, n_rel = rpb.shape[0], rpb.shape[1]
    blocks = jnp.where(valid_c[None, None], rpb[:, :, rel_c] * LOG2E, MASKED).astype(F32)
    masked = jnp.full((heads, n_rel, GRID_W, GRID_W), MASKED, F32)
    shifted = jnp.concatenate([blocks[:, 1:], masked[:, :1]], axis=1)
    pairs = jnp.concatenate([jnp.concatenate([blocks, shifted], axis=-1),
                             jnp.concatenate([masked, blocks], axis=-1),
                             jnp.concatenate([blocks, masked], axis=-1),
                             jnp.concatenate([masked[:, :1], masked[:, :1]], axis=-1)], axis=1)
    plan = []
    for blk in (0, 1, n_blocks - 1):
        start = min(max(blk * NA_Q_ROWS - NA_WIN_R // 2, 0), rows - NA_K_ROWS)
        per_row = []
        for r_i in range(NA_Q_ROWS):
            r = blk * NA_Q_ROWS + r_i
            row_start = min(max(r - NA_WIN_R // 2, 0), rows - wr)
            rel = lambda kr: kr - r + (NA_WIN_R - 1)
            inside = lambda kr: row_start <= kr < row_start + wr
            per_pair = []
            for j in range(NA_K_ROWS // 2):
                left, right = start + 2 * j, start + 2 * j + 1
                if inside(left) and inside(right):
                    per_pair.append(rel(left))
                elif inside(right):
                    per_pair.append(n_rel + rel(right))
                elif inside(left):
                    per_pair.append(2 * n_rel + rel(left))
                else:
                    per_pair.append(3 * n_rel)
            per_row.append(per_pair)
        plan.append(per_row)
    assert 2 * GRID_W == HEAD_DIM
    tq, tkeys = NA_Q_ROWS * GRID_W, NA_K_ROWS * GRID_W
    return pl.pallas_call(
        functools.partial(_na_table_kernel, plan=plan),
        grid=(heads,),
        in_specs=[pl.BlockSpec((None, 3 * n_rel + 1, GRID_W, 2 * GRID_W), lambda h: (h, 0, 0, 0))],
        out_specs=pl.BlockSpec((None, 3, tq, tkeys), lambda h: (h, 0, 0, 0)),
        out_shape=jax.ShapeDtypeStruct((heads, 3, tq, tkeys), F32),
        compiler_params=_cparams("arbitrary"), name="na_bias_tables",
    )(pairs)


def _na_table_kernel(pairs_ref, o_ref, *, plan):
    for v, per_row in enumerate(plan):
        for r_i, per_pair in enumerate(per_row):
            for j, idx in enumerate(per_pair):
                o_ref[v, r_i * GRID_W:(r_i + 1) * GRID_W, j * 2 * GRID_W:(j + 1) * 2 * GRID_W] = pairs_ref[idx]


def neighbourhood_attention(q, k, v_src, v_col0, kc, vc_src, vc_col0, bias, bounded):
    s, bw = q.shape
    heads = bw // HEAD_DIM
    rows = s // GRID_W
    n_blocks = rows // NA_Q_ROWS
    assert n_blocks >= 3 and rows % NA_Q_ROWS == 0
    tq = NA_Q_ROWS * GRID_W
    lc = kc.shape[0]
    variant = lambda b: jnp.where(b == 0, 0, jnp.where(b == n_blocks - 1, 2, 1))
    return pl.pallas_call(
        functools.partial(_na_kernel, rows=rows, bounded=bounded),
        grid=(heads, n_blocks),
        in_specs=[
            pl.BlockSpec((tq, HEAD_DIM), lambda h, b: (b, h)),
            pl.BlockSpec((s, HEAD_DIM), lambda h, b: (0, h)),
            pl.BlockSpec((s, HEAD_DIM), lambda h, b: (0, v_col0 + h)),
            pl.BlockSpec((lc, HEAD_DIM), lambda h, b: (0, h)),
            pl.BlockSpec((lc, HEAD_DIM), lambda h, b: (0, vc_col0 + h)),
            pl.BlockSpec((None, None, tq, NA_K_ROWS * GRID_W), lambda h, b: (h, variant(b), 0, 0)),
        ],
        out_specs=pl.BlockSpec((tq, HEAD_DIM), lambda h, b: (b, h)),
        out_shape=jax.ShapeDtypeStruct((s, bw), BF16),
        compiler_params=_cparams("arbitrary", "arbitrary"),
        name="neighbourhood_attention",
    )(q, k, v_src, kc, vc_src, bias)


def _dense_attn_kernel(q_ref, k_ref, v_ref, o_ref):
    s = lax.dot_general(q_ref[...], k_ref[...], (((1,), (1,)), ((), ())), preferred_element_type=F32)
    p = jnp.exp2(s - jnp.max(s, axis=-1, keepdims=True))
    o = jnp.dot(p.astype(BF16), v_ref[...], preferred_element_type=F32)
    o_ref[...] = (o * (1.0 / jnp.sum(p, axis=-1, keepdims=True))).astype(o_ref.dtype)


def dense_attention(q, k, v_src, v_col0):
    m, bw = q.shape
    heads = bw // HEAD_DIM
    blk = lambda c0: pl.BlockSpec((m, HEAD_DIM), lambda h: (0, c0 + h))
    return pl.pallas_call(
        _dense_attn_kernel, grid=(heads,), in_specs=[blk(0), blk(0), blk(v_col0)], out_specs=blk(0),
        out_shape=jax.ShapeDtypeStruct((m, bw), BF16),
        compiler_params=_cparams("arbitrary"), name="dense_attention",
    )(q, k, v_src)


def _merge_up_kernel(od_ref, on_ref, cb_ref, cc_ref, cx_ref, ccp_ref, cxp_ref, ccn_ref, cxn_ref, cw_ref,
                     g0_ref, g1_ref, g2_ref, wd_ref, wn_ref, wc_ref, o_ref, oc_sc):
    i, j = pl.program_id(0), pl.program_id(1)

    @pl.when(j == 0)
    def _():
        u = cc_ref[...].astype(F32) * cx_ref[...].astype(F32)
        tm = u.shape[0]
        halo = ccp_ref.shape[0]
        prev_row = ccp_ref[halo - 1:halo, :].astype(F32) * cxp_ref[halo - 1:halo, :].astype(F32)
        next_row = ccn_ref[0:1, :].astype(F32) * cxn_ref[0:1, :].astype(F32)
        prev_row = jnp.where(i == 0, 0.0, prev_row)
        next_row = jnp.where(i == pl.num_programs(0) - 1, 0.0, next_row)
        r = lax.broadcasted_iota(jnp.int32, u.shape, 0)
        u_prev = jnp.where(r == 0, prev_row, pltpu.roll(u, 1, axis=0))
        u_next = jnp.where(r == tm - 1, next_row, pltpu.roll(u, tm - 1, axis=0))
        conv = cw_ref[0:1, :] * u_prev + cw_ref[1:2, :] * u + cw_ref[2:3, :] * u_next
        oc_sc[...] = (cb_ref[...].astype(F32) * conv).astype(BF16)

    acc = _sigmoid(g0_ref[...].astype(F32)) * jnp.dot(od_ref[...], wd_ref[...], preferred_element_type=F32)
    acc += _sigmoid(g1_ref[...].astype(F32)) * jnp.dot(on_ref[...], wn_ref[...], preferred_element_type=F32)
    acc += _sigmoid(g2_ref[...].astype(F32)) * jnp.dot(oc_sc[...], wc_ref[...], preferred_element_type=F32)
    o_ref[...] = acc.astype(o_ref.dtype)


def merge_up(o_diff, o_na, p, conv_w, w_up_diff, w_up_na, w_up_conv, layer, d):
    m, bw = o_diff.shape
    tm, tn = _tile(m, 512), math.gcd(_tile(d, 1024), bw)
    halo = 8
    n_halo = m // halo
    row_blk = lambda c: pl.BlockSpec((tm, bw), lambda i, j, c=c: (i, c))
    prev_blk = lambda c: pl.BlockSpec((halo, bw), lambda i, j, c=c: (jnp.maximum(i * (tm // halo) - 1, 0), c))
    next_blk = lambda c: pl.BlockSpec((halo, bw), lambda i, j, c=c: (jnp.minimum((i + 1) * (tm // halo), n_halo - 1), c))
    gate0 = 9 * bw // tn
    gate_blk = lambda b: pl.BlockSpec((tm, tn), lambda i, j, b=b: (i, gate0 + b * (d // tn) + j))
    w_blk = pl.BlockSpec((None, bw, tn), lambda i, j: (layer, 0, j))
    assert (9 * bw) % tn == 0
    return pl.pallas_call(
        _merge_up_kernel,
        grid=(m // tm, d // tn),
        in_specs=[row_blk(0), row_blk(0), row_blk(6), row_blk(7), row_blk(8),
                  prev_blk(7), prev_blk(8), next_blk(7), next_blk(8),
                  pl.BlockSpec((CONV_K, bw), lambda i, j: (0, 0)),
                  gate_blk(0), gate_blk(1), gate_blk(2), w_blk, w_blk, w_blk],
        out_specs=pl.BlockSpec((tm, tn), lambda i, j: (i, j)),
        out_shape=jax.ShapeDtypeStruct((m, d), BF16),
        scratch_shapes=[pltpu.VMEM((tm, bw), BF16)],
        compiler_params=_cparams("arbitrary", "arbitrary"),
        name="merge_up",
    )(o_diff, o_na, p, p, p, p, p, p, p, conv_w, p, p, p, w_up_diff, w_up_na, w_up_conv)


def kernel(x, c, ctx, c_ctx, w_mod_a, w_mod_b, b_mod, norm1_g, w_in, diff_qn_g, diff_kn_g, lambda_q1, lambda_k1, lambda_q2, lambda_k2, diff_subln_g, na_qn_g, na_kn_g, na_rpb, conv_w, w_up_diff, w_up_na, w_up_conv, w_o, norm2_g, router_w, router_b, w_gu, b_gu, w_down, b_down):
    batch, s, d = x.shape
    assert batch == 1 and ctx.shape[0] == 1
    depth = w_in.shape[0]
    bw = w_up_diff.shape[1]
    dv0 = 2 * bw // (2 * HEAD_DIM)
    nv0 = 5 * bw // HEAD_DIM
    rows = s // GRID_W

    cond = jnp.zeros((8, d), F32).at[0].set(c[0]).at[1].set(c_ctx)
    mod = modulation(cond, w_mod_a, w_mod_b, b_mod).reshape(depth, 8, N_MOD, d)
    rope = rope_tables(s)
    w_in, w_o = w_in.astype(BF16), w_o.astype(BF16)
    ups = tuple(w.astype(BF16) for w in (w_up_diff, w_up_na, w_up_conv))

    xl, xc = x[0], ctx[0]
    for i in range(depth):
        last = i == depth - 1
        lam_init = 0.8 - 0.6 * math.exp(-0.3 * i)
        lam = (jnp.exp(jnp.sum(lambda_q1[i] * lambda_k1[i])) - jnp.exp(jnp.sum(lambda_q2[i] * lambda_k2[i]))
               + lam_init)
        ml, mc = mod[i, 0], mod[i, 1]
        gains = jnp.stack([diff_qn_g[i], diff_kn_g[i], na_qn_g[i], na_kn_g[i]])

        p_l = project_in(prenorm(xl, norm1_g[i], ml[1], ml[0]), w_in, i)
        p_c = project_in(prenorm(xc, norm1_g[i], mc[1], mc[0]), w_in, i)
        dq_l, dk_l, nq_l, nk_l = qk_prep(p_l, gains, rope, bw)
        dq_c, dk_c, nq_c, nk_c = qk_prep(p_c, gains, None, bw)

        diff_args = (dq_l, dk_l, p_l, dv0, lam, diff_subln_g[i], 1.0 - lam_init, (dk_c, p_c, dv0))
        o_diff_l = lax.cond(score_bound_log2(diff_qn_g[i], diff_kn_g[i]) <= SCORE_BOUND_LOG2,
                            lambda: diff_attention_bounded(*diff_args), lambda: diff_attention(*diff_args))
        na_args = (nq_l, nk_l, p_l, nv0, nk_c, p_c, nv0, na_bias_tables(na_rpb[i], rows))
        na_bound = score_bound_log2(na_qn_g[i], na_kn_g[i]) + jnp.max(jnp.abs(na_rpb[i])) * LOG2E
        o_na_l = lax.cond(na_bound <= SCORE_BOUND_LOG2, lambda: neighbourhood_attention(*na_args, True),
                          lambda: neighbourhood_attention(*na_args, False))
        xl = project_out(merge_up(o_diff_l, o_na_l, p_l, conv_w[i], *ups, i, d), w_o, i, xl, ml[2])
        if not last:
            o_diff_c = diff_attention(dq_c, dk_c, p_c, dv0, lam, diff_subln_g[i], 1.0 - lam_init)
            o_na_c = dense_attention(nq_c, nk_c, p_c, nv0)
            xc = project_out(merge_up(o_diff_c, o_na_c, p_c, conv_w[i], *ups, i, d), w_o, i, xc, mc[2])

        router = (router_w[i], router_b[i])
        x_mid, xl, group = xl, None, s // MOE_GROUPS
        for gi in range(MOE_GROUPS):
            h_g, eid, ew, rank, counts = prenorm_route(x_mid, norm2_g[i], ml[4], ml[3], router, gi * group, group)
            y4 = moe_routed(h_g, eid, ew, rank, counts, w_gu, b_gu[i], w_down, b_down[i], i)
            xl = moe_combine(x_mid, ml[5], ew, y4, gi * group, xl)
        if not last:
            h_c, comb_c = prenorm(xc, norm2_g[i], mc[4], mc[3], router)
            xc = moe(h_c, comb_c, w_gu, b_gu[i], w_down, b_down[i], i, xc, mc[5])
    return xl[None]
```

```python
import functools
import math

import jax
import jax.numpy as jnp
from jax import lax
from jax.experimental import pallas as pl
from jax.experimental.pallas import tpu as pltpu

F32 = jnp.float32
BF16 = jnp.bfloat16
HIGHEST = lax.Precision.HIGHEST

GRID_W = 64
HEAD_DIM = 128
NA_WIN_R = 8
NA_WIN_C = 16
CONV_K = 3
N_BRANCH = 3
N_MOD = 6
TOP_K = 4
SWIGLU_LIMIT = 7.0
SWIGLU_ALPHA = 1.702
ROPE_BASE = 10000.0
EPS = 1e-6
LOG2E = math.log2(math.e)
Q_SCALE = HEAD_DIM ** -0.5 * LOG2E
MASKED = -1e30
SCORE_BOUND_LOG2 = 60.0

V7X_VMEM_LIMIT_BYTES = 56 * 1024 * 1024
NA_Q_ROWS = 8
NA_K_ROWS = NA_Q_ROWS + NA_WIN_R


def _cparams(*sem):
    return pltpu.CompilerParams(dimension_semantics=sem, vmem_limit_bytes=V7X_VMEM_LIMIT_BYTES)


def _tile(n, target):
    if n <= target:
        return n
    t = target
    while n % t:
        t -= 8
    assert t > 0, (n, target)
    return t


def _sigmoid(x):
    return 1.0 / (1.0 + jnp.exp(-x))


def _mod_kernel(c_ref, wa_ref, wb_ref, b_ref, o_ref):
    c = c_ref[...]
    s = c * _sigmoid(c)
    m1 = jnp.dot(s, wa_ref[...], precision=HIGHEST, preferred_element_type=F32)
    o_ref[...] = jnp.dot(m1, wb_ref[...], precision=HIGHEST, preferred_element_type=F32) + b_ref[...]


def modulation(cond, w_a, w_b, b):
    depth, d, r = w_a.shape
    width = w_b.shape[-1]
    rows = cond.shape[0]
    tn = _tile(width, 4096)
    return pl.pallas_call(
        _mod_kernel,
        grid=(depth, width // tn),
        in_specs=[
            pl.BlockSpec((rows, d), lambda l, j: (0, 0)),
            pl.BlockSpec((None, d, r), lambda l, j: (l, 0, 0)),
            pl.BlockSpec((None, r, tn), lambda l, j: (l, 0, j)),
            pl.BlockSpec((None, 1, tn), lambda l, j: (l, 0, j)),
        ],
        out_specs=pl.BlockSpec((None, rows, tn), lambda l, j: (l, 0, j)),
        out_shape=jax.ShapeDtypeStruct((depth, rows, width), F32),
        compiler_params=_cparams("arbitrary", "arbitrary"),
        name="modulation",
    )(cond, w_a, w_b, b.reshape(depth, 1, width))


def _modulated_norm(x_ref, g_ref, sc_ref, sh_ref):
    x = x_ref[...]
    y = x * lax.rsqrt(jnp.mean(x * x, axis=-1, keepdims=True) + EPS)
    return (y * g_ref[...]) * (1.0 + sc_ref[...]) + sh_ref[...]


def _prenorm_kernel(x_ref, g_ref, sc_ref, sh_ref, o_ref):
    o_ref[...] = _modulated_norm(x_ref, g_ref, sc_ref, sh_ref).astype(o_ref.dtype)


def _split_bf16(x):
    hi = x.astype(BF16)
    return hi, (x - hi.astype(F32)).astype(BF16)


def _route(h, rwh_ref, rwl_ref, rb_ref):
    h_hi, h_lo = _split_bf16(h)
    w_hi = rwh_ref[...]
    logits = (jnp.dot(h_hi, w_hi, preferred_element_type=F32) + jnp.dot(h_lo, w_hi, preferred_element_type=F32)
              + jnp.dot(h_hi, rwl_ref[...], preferred_element_type=F32) + rb_ref[...])
    n_exp = logits.shape[-1]
    lane = lax.broadcasted_iota(jnp.int32, logits.shape, 1).astype(F32)
    sels, ids, vals = [], [], []
    for _ in range(TOP_K):
        m = jnp.max(logits, axis=-1, keepdims=True)
        first = jnp.min(jnp.where(logits == m, lane, float(n_exp)), axis=-1, keepdims=True)
        sel = lane == first
        sels.append(sel)
        ids.append(first)
        vals.append(m)
        logits = jnp.where(sel, -jnp.inf, logits)
    es = [jnp.exp(v - vals[0]) for v in vals]
    inv = 1.0 / functools.reduce(lambda a, b: a + b, es)
    return sels, ids, [e * inv for e in es]


def _prenorm_router_kernel(x_ref, g_ref, sc_ref, sh_ref, rwh_ref, rwl_ref, rb_ref, o_ref, comb_ref):
    h = _modulated_norm(x_ref, g_ref, sc_ref, sh_ref)
    o_ref[...] = h.astype(o_ref.dtype)
    sels, _, wts = _route(h, rwh_ref, rwl_ref, rb_ref)
    comb = jnp.zeros(comb_ref.shape, F32)
    for sel, w in zip(sels, wts):
        comb = comb + jnp.where(sel, w, 0.0)
    comb_ref[...] = comb


def _columns(cols, width):
    lane = lax.broadcasted_iota(jnp.int32, (cols[0].shape[0], width), 1)
    out = jnp.zeros(lane.shape, cols[0].dtype)
    for k, c in enumerate(cols):
        out = jnp.where(lane == k, c, out)
    return out


def _prenorm_route_kernel(x_ref, g_ref, sc_ref, sh_ref, rwh_ref, rwl_ref, rb_ref,
                          o_ref, eid_ref, ew_ref, rank_ref, cnt_ref, carry_sc):
    @pl.when(pl.program_id(0) == 0)
    def _():
        carry_sc[...] = jnp.zeros(carry_sc.shape, F32)

    h = _modulated_norm(x_ref, g_ref, sc_ref, sh_ref)
    o_ref[...] = h.astype(o_ref.dtype)
    sels, ids, wts = _route(h, rwh_ref, rwl_ref, rb_ref)
    tm = h.shape[0]
    chosen = functools.reduce(lambda a, b: a + b, [jnp.where(s, 1.0, 0.0) for s in sels])
    r = lax.broadcasted_iota(jnp.int32, (tm, tm), 0)
    c = lax.broadcasted_iota(jnp.int32, (tm, tm), 1)
    earlier = jnp.where(c < r, 1.0, 0.0).astype(BF16)
    before = jnp.dot(earlier, chosen.astype(BF16), preferred_element_type=F32) + carry_sc[...]
    ranks = [jnp.sum(jnp.where(s, before, 0.0), axis=-1, keepdims=True) for s in sels]
    carry_sc[...] += jnp.sum(chosen, axis=0, keepdims=True)
    eid_ref[...] = _columns(ids, TOP_K).astype(jnp.int32)
    ew_ref[...] = _columns(wts, TOP_K)
    rank_ref[...] = _columns(ranks, TOP_K).astype(jnp.int32)
    cnt_ref[...] = carry_sc[...].astype(jnp.int32)


def prenorm(x, g, scale, shift, router=None):
    m, d = x.shape
    tm = _tile(m, 256)
    row = lambda i: (i, 0)
    fixed = lambda i: (0, 0)
    vec = pl.BlockSpec((1, d), fixed)
    in_specs = [pl.BlockSpec((tm, d), row), vec, vec, vec]
    args = [x, g.reshape(1, d), scale.reshape(1, d), shift.reshape(1, d)]
    if router is None:
        return pl.pallas_call(
            _prenorm_kernel, grid=(m // tm,), in_specs=in_specs,
            out_specs=pl.BlockSpec((tm, d), row),
            out_shape=jax.ShapeDtypeStruct((m, d), BF16),
            compiler_params=_cparams("arbitrary"), name="prenorm",
        )(*args)
    rw, rb = router
    n_exp = rw.shape[-1]
    in_specs += [pl.BlockSpec((d, n_exp), fixed)] * 2 + [pl.BlockSpec((1, n_exp), fixed)]
    args += [*_split_bf16(rw), rb.reshape(1, n_exp)]
    return pl.pallas_call(
        _prenorm_router_kernel, grid=(m // tm,), in_specs=in_specs,
        out_specs=[pl.BlockSpec((tm, d), row), pl.BlockSpec((tm, n_exp), row)],
        out_shape=[jax.ShapeDtypeStruct((m, d), BF16), jax.ShapeDtypeStruct((m, n_exp), F32)],
        compiler_params=_cparams("arbitrary"), name="prenorm_router",
    )(*args)


def prenorm_route(x, g, scale, shift, router, row0, m):
    d = x.shape[1]
    tm = _tile(m, 256)
    assert row0 % tm == 0
    rw, rb = router
    n_exp = rw.shape[-1]
    row = lambda i: (i, 0)
    fixed = lambda i: (0, 0)
    vec = pl.BlockSpec((1, d), fixed)
    per_tok = pl.BlockSpec((tm, TOP_K), row)
    return pl.pallas_call(
        _prenorm_route_kernel, grid=(m // tm,),
        in_specs=[pl.BlockSpec((tm, d), lambda i: (i + row0 // tm, 0)), vec, vec, vec,
                  pl.BlockSpec((d, n_exp), fixed), pl.BlockSpec((d, n_exp), fixed),
                  pl.BlockSpec((1, n_exp), fixed)],
        out_specs=[pl.BlockSpec((tm, d), row), per_tok, per_tok, per_tok, pl.BlockSpec((1, n_exp), fixed)],
        out_shape=[jax.ShapeDtypeStruct((m, d), BF16), jax.ShapeDtypeStruct((m, TOP_K), jnp.int32),
                   jax.ShapeDtypeStruct((m, TOP_K), F32), jax.ShapeDtypeStruct((m, TOP_K), jnp.int32),
                   jax.ShapeDtypeStruct((1, n_exp), jnp.int32)],
        scratch_shapes=[pltpu.VMEM((1, n_exp), F32)],
        compiler_params=_cparams("arbitrary"), name="prenorm_route",
    )(x, g.reshape(1, d), scale.reshape(1, d), shift.reshape(1, d), *_split_bf16(rw), rb.reshape(1, n_exp))


EXPERT_TILE = 256
MOE_GROUPS = 2


def _expert_mlp_kernel(te_ref, nt_ref, x_ref, wgu_ref, bgu_ref, wd_ref, bd_ref, o_ref, wgu_sc, wd_sc):
    j = pl.program_id(0)
    used = j < nt_ref[0]

    @pl.when(jnp.logical_not(used))
    def _():
        o_ref[...] = jnp.zeros(o_ref.shape, o_ref.dtype)

    @pl.when(jnp.logical_and(used, jnp.logical_or(j == 0, te_ref[j] != te_ref[jnp.maximum(j - 1, 0)])))
    def _():
        wgu_sc[...] = wgu_ref[...].astype(BF16)
        wd_sc[...] = wd_ref[...].astype(BF16)

    @pl.when(used)
    def _():
        d_exp = wd_ref.shape[0]
        gu = jnp.dot(x_ref[...], wgu_sc[...], preferred_element_type=F32) + bgu_ref[...]
        gate = jnp.minimum(gu[:, :d_exp], SWIGLU_LIMIT)
        lin = jnp.clip(gu[:, d_exp:], -SWIGLU_LIMIT, SWIGLU_LIMIT)
        act = gate * _sigmoid(SWIGLU_ALPHA * gate) * (lin + 1.0)
        y = jnp.dot(act.astype(BF16), wd_sc[...], preferred_element_type=F32) + bd_ref[...]
        o_ref[...] = y.astype(o_ref.dtype)


def expert_mlp(xg, tile_expert, n_tiles_used, w_gu, b_gu, w_down, b_down, layer):
    p_rows, d = xg.shape
    _, n_exp, _, two_de = w_gu.shape
    d_exp = two_de // 2
    t = EXPERT_TILE
    grid_spec = pltpu.PrefetchScalarGridSpec(
        num_scalar_prefetch=2,
        grid=(p_rows // t,),
        in_specs=[
            pl.BlockSpec((t, d), lambda j, te, nt: (j, 0)),
            pl.BlockSpec((None, None, d, two_de), lambda j, te, nt: (layer, te[j], 0, 0)),
            pl.BlockSpec((None, 1, two_de), lambda j, te, nt: (te[j], 0, 0)),
            pl.BlockSpec((None, None, d_exp, d), lambda j, te, nt: (layer, te[j], 0, 0)),
            pl.BlockSpec((None, 1, d), lambda j, te, nt: (te[j], 0, 0)),
        ],
        out_specs=pl.BlockSpec((t, d), lambda j, te, nt: (j, 0)),
        scratch_shapes=[pltpu.VMEM((d, two_de), BF16), pltpu.VMEM((d_exp, d), BF16)],
    )
    return pl.pallas_call(
        _expert_mlp_kernel, grid_spec=grid_spec,
        out_shape=jax.ShapeDtypeStruct((p_rows, d), BF16),
        compiler_params=_cparams("arbitrary"), name="expert_mlp",
    )(tile_expert, n_tiles_used, xg, w_gu, b_gu.reshape(n_exp, 1, two_de), w_down, b_down.reshape(n_exp, 1, d))


def _combine_kernel(x_ref, gate_ref, ew_ref, *rest):
    y_refs, o_ref = rest[:-1], rest[-1]
    ew = ew_ref[...]
    acc = jnp.zeros(o_ref.shape, F32)
    for k, y_ref in enumerate(y_refs):
        acc = acc + ew[:, k:k + 1] * y_ref[...].astype(F32)
    o_ref[...] = x_ref[...] + gate_ref[...] * acc


def _combine_into_kernel(x_ref, gate_ref, ew_ref, *rest):
    _combine_kernel(x_ref, gate_ref, ew_ref, *rest[:TOP_K], rest[-1])


def moe_combine(x, gate, ew, y4, row0, prev):
    d = x.shape[1]
    m = ew.shape[0]
    tm, tn = _tile(m, 512), _tile(d, 1024)
    assert row0 % tm == 0
    tile = pl.BlockSpec((tm, tn), lambda i, j: (i + row0 // tm, j))
    y_blk = lambda k: pl.BlockSpec((None, tm, tn), lambda i, j, k=k: (k, i, j))
    in_specs = ([tile, pl.BlockSpec((1, tn), lambda i, j: (0, j)), pl.BlockSpec((tm, TOP_K), lambda i, j: (i, 0))]
                + [y_blk(k) for k in range(TOP_K)])
    args = [x, gate.reshape(1, d), ew] + [y4] * TOP_K
    if prev is not None:
        in_specs.append(pl.BlockSpec(memory_space=pl.ANY))
        args.append(prev)
    return pl.pallas_call(
        _combine_kernel if prev is None else _combine_into_kernel, grid=(m // tm, d // tn),
        in_specs=in_specs, out_specs=tile, out_shape=jax.ShapeDtypeStruct(x.shape, F32),
        input_output_aliases={} if prev is None else {len(args) - 1: 0},
        compiler_params=_cparams("arbitrary", "arbitrary"), name="moe_combine",
    )(*args)


def moe_routed(h, eid, ew, rank, counts, w_gu, b_gu, w_down, b_down, layer):
    m, d = h.shape
    n_exp = counts.shape[-1]
    t = EXPERT_TILE
    p_rows = m * TOP_K + n_exp * t
    counts = counts.reshape(n_exp)
    tiles_per = (counts + t - 1) // t
    tile_end = jnp.cumsum(tiles_per)
    start = (tile_end - tiles_per) * t
    pos = start[eid] + rank
    tile_id = jnp.arange(p_rows // t, dtype=jnp.int32)
    tile_expert = jnp.minimum(jnp.sum((tile_end[None, :] <= tile_id[:, None]).astype(jnp.int32), axis=1),
                              n_exp - 1)
    n_used = tile_end[-1:].astype(jnp.int32)
    pos_t = pos.T.reshape(-1)
    src = (jnp.arange(p_rows, dtype=jnp.int32) % m).at[pos_t].set(
        jnp.tile(jnp.arange(m, dtype=jnp.int32), TOP_K), unique_indices=True, mode="promise_in_bounds")
    xg = h.at[src].get(mode="promise_in_bounds")
    yg = expert_mlp(xg, tile_expert, n_used, w_gu, b_gu, w_down, b_down, layer)
    return yg.at[pos_t].get(mode="promise_in_bounds", unique_indices=True).reshape(TOP_K, m, d)


def _mm_kernel(*refs, epilogue):
    a_ref, w_ref = refs[0], refs[1]
    extras, o_ref = refs[2:-1], refs[-1]
    acc = jnp.dot(a_ref[...], w_ref[...].astype(BF16), preferred_element_type=F32)
    o_ref[...] = epilogue(acc, *extras).astype(o_ref.dtype)


def matmul(a, w, w_spec, tm, tn, n_col_tiles, extras, epilogue, out_spec, out_shape, name):
    m, k = a.shape
    return pl.pallas_call(
        functools.partial(_mm_kernel, epilogue=epilogue),
        grid=(m // tm, n_col_tiles),
        in_specs=[pl.BlockSpec((tm, k), lambda i, j: (i, 0)), w_spec] + [s for _, s in extras],
        out_specs=out_spec,
        out_shape=out_shape,
        compiler_params=_cparams("arbitrary", "arbitrary"),
        name=name,
    )(a, w, *[x for x, _ in extras])


def _cast_epilogue(acc):
    return acc


def _residual_epilogue(acc, x_ref, gate_ref):
    return x_ref[...] + gate_ref[...] * acc


def _swiglu_epilogue(acc, b_ref, comb_ref):
    d_exp = acc.shape[-1] // 2
    gu = acc + b_ref[...]
    gate = jnp.minimum(gu[:, :d_exp], SWIGLU_LIMIT)
    lin = jnp.clip(gu[:, d_exp:], -SWIGLU_LIMIT, SWIGLU_LIMIT)
    act = gate * _sigmoid(SWIGLU_ALPHA * gate) * (lin + 1.0)
    comb = comb_ref[...]
    lane = lax.broadcasted_iota(jnp.int32, comb.shape, 1)
    cw = jnp.sum(jnp.where(lane == pl.program_id(1), comb, 0.0), axis=-1, keepdims=True)
    return act * cw


def _moe_down_epilogue(acc, comb_ref, bd_ref, x_ref, gate_ref):
    bias = jnp.dot(comb_ref[...].astype(BF16), bd_ref[...].astype(BF16), preferred_element_type=F32)
    return x_ref[...] + gate_ref[...] * (acc + bias)


def project_in(h, w_in, layer):
    m, d = h.shape
    width = w_in.shape[-1]
    tm, tn = _tile(m, 1024), _tile(width, 1024)
    return matmul(h, w_in, pl.BlockSpec((None, d, tn), lambda i, j: (layer, 0, j)), tm, tn, width // tn, [],
                  _cast_epilogue, pl.BlockSpec((tm, tn), lambda i, j: (i, j)),
                  jax.ShapeDtypeStruct((m, width), BF16), "project_in")


def project_out(merged, w_o, layer, x, gate):
    m, d = x.shape
    tm, tn = _tile(m, 1024), _tile(d, 512)
    tile = pl.BlockSpec((tm, tn), lambda i, j: (i, j))
    return matmul(merged, w_o, pl.BlockSpec((None, merged.shape[1], tn), lambda i, j: (layer, 0, j)),
                  tm, tn, d // tn,
                  [(x, tile), (gate.reshape(1, d), pl.BlockSpec((1, tn), lambda i, j: (0, j)))],
                  _residual_epilogue, tile, jax.ShapeDtypeStruct((m, d), F32), "project_out")


def moe(h, comb, w_gu, b_gu, w_down, b_down, layer, x, gate):
    m, d = h.shape
    depth, n_exp, _, two_de = w_gu.shape
    d_exp = two_de // 2
    tm = _tile(m, 1024)
    act = matmul(
        h, w_gu, pl.BlockSpec((None, None, d, two_de), lambda i, j: (layer, j, 0, 0)), tm, two_de, n_exp,
        [(b_gu.reshape(n_exp, 1, two_de), pl.BlockSpec((None, 1, two_de), lambda i, j: (j, 0, 0))),
         (comb, pl.BlockSpec((tm, n_exp), lambda i, j: (i, 0)))],
        _swiglu_epilogue, pl.BlockSpec((tm, d_exp), lambda i, j: (i, j)),
        jax.ShapeDtypeStruct((m, n_exp * d_exp), BF16), "moe_up")
    tm, tn = _tile(m, 512), _tile(d, 256)
    tile = pl.BlockSpec((tm, tn), lambda i, j: (i, j))
    return matmul(
        act, w_down.reshape(depth, n_exp * d_exp, d),
        pl.BlockSpec((None, n_exp * d_exp, tn), lambda i, j: (layer, 0, j)), tm, tn, d // tn,
        [(comb, pl.BlockSpec((tm, n_exp), lambda i, j: (i, 0))),
         (b_down, pl.BlockSpec((n_exp, tn), lambda i, j: (0, j))),
         (x, tile), (gate.reshape(1, d), pl.BlockSpec((1, tn), lambda i, j: (0, j)))],
        _moe_down_epilogue, tile, jax.ShapeDtypeStruct((m, d), F32), "moe_down")


def _chunk_norm(x, g):
    return x * lax.rsqrt(jnp.mean(x * x, axis=-1, keepdims=True) + EPS) * g


def _qk_prep_kernel(*refs, rope):
    if rope:
        dq_ref, dk_ref, nq_ref, nk_ref, g_ref, cos_ref, sin_ref = refs[:7]
        cos, sin = cos_ref[...], sin_ref[...]
        lane = lax.broadcasted_iota(jnp.int32, cos.shape, 1)
        lower = (lane & (HEAD_DIM // 4)) == 0
    else:
        dq_ref, dk_ref, nq_ref, nk_ref, g_ref = refs[:5]
    outs = refs[-4:]
    ins = (dq_ref, dk_ref, nq_ref, nk_ref)
    for idx in range(4):
        g = g_ref[idx:idx + 1, :]
        rot = rope and idx < 2
        scale = Q_SCALE if idx in (0, 2) else 1.0
        for c in range(ins[idx].shape[-1] // HEAD_DIM):
            sl = slice(c * HEAD_DIM, (c + 1) * HEAD_DIM)
            y = _chunk_norm(ins[idx][:, sl].astype(F32), g)
            if rot:
                swapped = jnp.where(lower, pltpu.roll(y, HEAD_DIM - HEAD_DIM // 4, axis=1),
                                    pltpu.roll(y, HEAD_DIM // 4, axis=1))
                y = y * cos + swapped * sin
            outs[idx][:, sl] = (y * scale).astype(BF16)


def qk_prep(p, gains, rope, bw):
    m = p.shape[0]
    tm = _tile(m, 512)
    col = lambda c: pl.BlockSpec((tm, bw), lambda i, c=c: (i, c))
    in_specs = [col(0), col(1), col(3), col(4), pl.BlockSpec((4, HEAD_DIM), lambda i: (0, 0))]
    args = [p, p, p, p, gains]
    if rope is not None:
        in_specs += [pl.BlockSpec((tm, HEAD_DIM), lambda i: (i, 0))] * 2
        args += list(rope)
    return pl.pallas_call(
        functools.partial(_qk_prep_kernel, rope=rope is not None),
        grid=(m // tm,), in_specs=in_specs,
        out_specs=[pl.BlockSpec((tm, bw), lambda i: (i, 0))] * 4,
        out_shape=[jax.ShapeDtypeStruct((m, bw), BF16)] * 4,
        compiler_params=_cparams("arbitrary"), name="qk_prep",
    )(*args)


def rope_tables(n):
    t = jnp.arange(n, dtype=jnp.int32)
    row = (t // GRID_W).astype(F32)
    col = (t % GRID_W).astype(F32)
    n_freq = HEAD_DIM // 4
    inv = ROPE_BASE ** (-jnp.arange(n_freq, dtype=F32) / n_freq)
    ar, ac = row[:, None] * inv, col[:, None] * inv
    ang = jnp.concatenate([ar, ar, ac, ac], axis=-1)
    lane = jnp.arange(HEAD_DIM)
    sign = jnp.where((lane & n_freq) == 0, -1.0, 1.0).astype(F32)
    return jnp.cos(ang), jnp.sin(ang) * sign


def _diff_attn_kernel(lam_ref, q_ref, k_ref, v_ref, *rest, n_main, has_extra, out_scale):
    if has_extra:
        kx_ref, vx_ref, g_ref, o_ref, m_sc, l_sc, acc_sc = rest
    else:
        g_ref, o_ref, m_sc, l_sc, acc_sc = rest
    kj = pl.program_id(2)
    n_steps = n_main + (1 if has_extra else 0)

    @pl.when(kj == 0)
    def _():
        m_sc[...] = jnp.full(m_sc.shape, -jnp.inf, F32)
        l_sc[...] = jnp.zeros(l_sc.shape, F32)
        acc_sc[...] = jnp.zeros(acc_sc.shape, F32)

    def update(kr, vr):
        v = vr[...]
        for hf in range(2):
            sl = slice(hf * HEAD_DIM, (hf + 1) * HEAD_DIM)
            s = lax.dot_general(q_ref[:, sl], kr[:, sl], (((1,), (1,)), ((), ())),
                                preferred_element_type=F32)
            m_prev = m_sc[hf]
            m_new = jnp.maximum(m_prev, jnp.max(s, axis=-1, keepdims=True))
            alpha = jnp.exp2(m_prev - m_new)
            p = jnp.exp2(s - m_new)
            l_sc[hf] = alpha * l_sc[hf] + jnp.sum(p, axis=-1, keepdims=True)
            acc_sc[hf] = alpha * acc_sc[hf] + jnp.dot(p.astype(BF16), v, preferred_element_type=F32)
            m_sc[hf] = m_new

    if has_extra:
        pl.when(kj < n_main)(lambda: update(k_ref, v_ref))
        pl.when(kj == n_main)(lambda: update(kx_ref, vx_ref))
    else:
        update(k_ref, v_ref)

    @pl.when(kj == n_steps - 1)
    def _():
        _diff_finalize(lam_ref, g_ref, o_ref, acc_sc[0], acc_sc[1], l_sc[0], l_sc[1], out_scale)


def _diff_finalize(lam_ref, g_ref, o_ref, acc0, acc1, l0, l1, out_scale):
    lam = lam_ref[0, 0]
    o = acc0 * (1.0 / l0) - lam * (acc1 * (1.0 / l1))
    y = o * lax.rsqrt(jnp.mean(o * o, axis=-1, keepdims=True) + EPS)
    o_ref[...] = (y * g_ref[...] * out_scale).astype(o_ref.dtype)


def _diff_attn_bounded_kernel(lam_ref, q_ref, k_ref, v_ref, kx_ref, vx_ref, g_ref, o_ref, l_sc, acc_sc,
                              *, tk, n_main, out_scale):
    l_sc[...] = jnp.zeros(l_sc.shape, F32)
    acc_sc[...] = jnp.zeros(acc_sc.shape, F32)

    def accumulate(k, v):
        for hf in range(2):
            sl = slice(hf * HEAD_DIM, (hf + 1) * HEAD_DIM)
            s = lax.dot_general(q_ref[:, sl], k[:, sl], (((1,), (1,)), ((), ())), preferred_element_type=F32)
            p = jnp.exp2(s)
            part = p[:, 0:HEAD_DIM]
            for c in range(1, p.shape[1] // HEAD_DIM):
                part = part + p[:, c * HEAD_DIM:(c + 1) * HEAD_DIM]
            l_sc[hf] += part
            acc_sc[hf] += jnp.dot(p.astype(BF16), v, preferred_element_type=F32)

    def body(j, carry):
        off = pl.multiple_of(j * tk, tk)
        accumulate(k_ref[pl.ds(off, tk), :], v_ref[pl.ds(off, tk), :])
        return carry

    lax.fori_loop(0, n_main, body, 0)
    accumulate(kx_ref[...], vx_ref[...])
    _diff_finalize(lam_ref, g_ref, o_ref, acc_sc[0], acc_sc[1],
                   jnp.sum(l_sc[0], axis=-1, keepdims=True), jnp.sum(l_sc[1], axis=-1, keepdims=True), out_scale)


def diff_attention_bounded(q, k, v_src, v_col0, lam, sub_g, out_scale, extra):
    mq, bw = q.shape
    mk = k.shape[0]
    hw = 2 * HEAD_DIM
    tq, tk = _tile(mq, 1024), _tile(mk, 1024)
    kx, vx_src, vx_col0 = extra
    mx = kx.shape[0]
    once = pl.Buffered(1)
    return pl.pallas_call(
        functools.partial(_diff_attn_bounded_kernel, tk=tk, n_main=mk // tk, out_scale=out_scale),
        grid=(bw // hw, mq // tq),
        in_specs=[
            pl.BlockSpec(memory_space=pltpu.SMEM),
            pl.BlockSpec((tq, hw), lambda h, i: (i, h)),
            pl.BlockSpec((mk, hw), lambda h, i: (0, h), pipeline_mode=once),
            pl.BlockSpec((mk, hw), lambda h, i: (0, v_col0 + h), pipeline_mode=once),
            pl.BlockSpec((mx, hw), lambda h, i: (0, h)),
            pl.BlockSpec((mx, hw), lambda h, i: (0, vx_col0 + h)),
            pl.BlockSpec((1, hw), lambda h, i: (0, 0)),
        ],
        out_specs=pl.BlockSpec((tq, hw), lambda h, i: (i, h)),
        out_shape=jax.ShapeDtypeStruct((mq, bw), BF16),
        scratch_shapes=[pltpu.VMEM((2, tq, HEAD_DIM), F32), pltpu.VMEM((2, tq, hw), F32)],
        compiler_params=_cparams("arbitrary", "arbitrary"),
        name="diff_attention_bounded",
    )(lam.reshape(1, 1).astype(F32), q, k, v_src, kx, vx_src, sub_g.reshape(1, hw))


def score_bound_log2(g_q, g_k):
    return Q_SCALE * HEAD_DIM * jnp.max(jnp.abs(g_q)) * jnp.max(jnp.abs(g_k)) * 1.01


def diff_attention(q, k, v_src, v_col0, lam, sub_g, out_scale, extra=None):
    mq, bw = q.shape
    mk = k.shape[0]
    hw = 2 * HEAD_DIM
    heads = bw // hw
    tq, tk = _tile(mq, 1024), _tile(mk, 1024)
    n_main = mk // tk
    last = n_main - 1
    kmap = lambda h, i, j: (jnp.minimum(j, last), h)
    in_specs = [
        pl.BlockSpec(memory_space=pltpu.SMEM),
        pl.BlockSpec((tq, hw), lambda h, i, j: (i, h)),
        pl.BlockSpec((tk, hw), kmap),
        pl.BlockSpec((tk, hw), lambda h, i, j: (jnp.minimum(j, last), v_col0 + h)),
    ]
    args = [lam.reshape(1, 1).astype(F32), q, k, v_src]
    if extra is not None:
        kx, vx_src, vx_col0 = extra
        mx = kx.shape[0]
        in_specs += [pl.BlockSpec((mx, hw), lambda h, i, j: (0, h)),
                     pl.BlockSpec((mx, hw), lambda h, i, j: (0, vx_col0 + h))]
        args += [kx, vx_src]
    in_specs.append(pl.BlockSpec((1, hw), lambda h, i, j: (0, 0)))
    args.append(sub_g.reshape(1, hw))
    return pl.pallas_call(
        functools.partial(_diff_attn_kernel, n_main=n_main, has_extra=extra is not None, out_scale=out_scale),
        grid=(heads, mq // tq, n_main + (extra is not None)),
        in_specs=in_specs,
        out_specs=pl.BlockSpec((tq, hw), lambda h, i, j: (i, h)),
        out_shape=jax.ShapeDtypeStruct((mq, bw), BF16),
        scratch_shapes=[pltpu.VMEM((2, tq, 1), F32), pltpu.VMEM((2, tq, 1), F32), pltpu.VMEM((2, tq, hw), F32)],
        compiler_params=_cparams("arbitrary", "arbitrary", "arbitrary"),
        name="diff_attention",
    )(*args)


def _na_kernel(q_ref, k_ref, v_ref, kc_ref, vc_ref, bias_ref, o_ref, *, rows, bounded):
    b = pl.program_id(1)
    start_row = jnp.clip(b * NA_Q_ROWS - NA_WIN_R // 2, 0, rows - NA_K_ROWS)
    start = pl.multiple_of(start_row * GRID_W, GRID_W * (NA_WIN_R // 2))
    n_keys = NA_K_ROWS * GRID_W
    q = q_ref[...]
    k = k_ref[pl.ds(start, n_keys), :]
    v = v_ref[pl.ds(start, n_keys), :]
    nt = (((1,), (1,)), ((), ()))
    s_loc = lax.dot_general(q, k, nt, preferred_element_type=F32) + bias_ref[...]
    s_ctx = lax.dot_general(q, kc_ref[...], nt, preferred_element_type=F32)
    if bounded:
        p_loc, p_ctx = jnp.exp2(s_loc), jnp.exp2(s_ctx)
    else:
        m = jnp.maximum(jnp.max(s_loc, axis=-1, keepdims=True), jnp.max(s_ctx, axis=-1, keepdims=True))
        p_loc, p_ctx = jnp.exp2(s_loc - m), jnp.exp2(s_ctx - m)
    denom = jnp.sum(p_loc, axis=-1, keepdims=True) + jnp.sum(p_ctx, axis=-1, keepdims=True)
    o = (jnp.dot(p_loc.astype(BF16), v, preferred_element_type=F32)
         + jnp.dot(p_ctx.astype(BF16), vc_ref[...], preferred_element_type=F32))
    o_ref[...] = (o * (1.0 / denom)).astype(o_ref.dtype)


def na_bias_tables(rpb, rows):
    n_blocks = rows // NA_Q_ROWS
    wr = min(NA_WIN_R, rows)
    w = jnp.arange(GRID_W)
    col_start = jnp.clip(w - NA_WIN_C // 2, 0, GRID_W - NA_WIN_C)
    kc = jnp.arange(GRID_W)
    valid_c = (kc[None, :] >= col_start[:, None]) & (kc[None, :] < col_start[:, None] + NA_WIN_C)
    rel_c = jnp.clip(kc[None, :] - w[:, None] + (NA_WIN_C - 1), 0, 2 * NA_WIN_C - 2)
    heads, n_rel = rpb.shape[0], rpb.shape[1]
    blocks = jnp.where(valid_c[None, None], rpb[:, :, rel_c] * LOG2E, MASKED).astype(F32)
    masked = jnp.full((heads, n_rel, GRID_W, GRID_W), MASKED, F32)
    shifted = jnp.concatenate([blocks[:, 1:], masked[:, :1]], axis=1)
    pairs = jnp.concatenate([jnp.concatenate([blocks, shifted], axis=-1),
                             jnp.concatenate([masked, blocks], axis=-1),
                             jnp.concatenate([blocks, masked], axis=-1),
                             jnp.concatenate([masked[:, :1], masked[:, :1]], axis=-1)], axis=1)
    plan = []
    for blk in (0, 1, n_blocks - 1):
        start = min(max(blk * NA_Q_ROWS - NA_WIN_R // 2, 0), rows - NA_K_ROWS)
        per_row = []
        for r_i in range(NA_Q_ROWS):
            r = blk * NA_Q_ROWS + r_i
            row_start = min(max(r - NA_WIN_R // 2, 0), rows - wr)
            rel = lambda kr: kr - r + (NA_WIN_R - 1)
            inside = lambda kr: row_start <= kr < row_start + wr
            per_pair = []
            for j in range(NA_K_ROWS // 2):
                left, right = start + 2 * j, start + 2 * j + 1
                if inside(left) and inside(right):
                    per_pair.append(rel(left))
                elif inside(right):
                    per_pair.append(n_rel + rel(right))
                elif inside(left):
                    per_pair.append(2 * n_rel + rel(left))
                else:
                    per_pair.append(3 * n_rel)
            per_row.append(per_pair)
        plan.append(per_row)
    assert 2 * GRID_W == HEAD_DIM
    tq, tkeys = NA_Q_ROWS * GRID_W, NA_K_ROWS * GRID_W
    return pl.pallas_call(
        functools.partial(_na_table_kernel, plan=plan),
        grid=(heads,),
        in_specs=[pl.BlockSpec((None, 3 * n_rel + 1, GRID_W, 2 * GRID_W), lambda h: (h, 0, 0, 0))],
        out_specs=pl.BlockSpec((None, 3, tq, tkeys), lambda h: (h, 0, 0, 0)),
        out_shape=jax.ShapeDtypeStruct((heads, 3, tq, tkeys), F32),
        compiler_params=_cparams("arbitrary"), name="na_bias_tables",
    )(pairs)


def _na_table_kernel(pairs_ref, o_ref, *, plan):
    for v, per_row in enumerate(plan):
        for r_i, per_pair in enumerate(per_row):
            for j, idx in enumerate(per_pair):
                o_ref[v, r_i * GRID_W:(r_i + 1) * GRID_W, j * 2 * GRID_W:(j + 1) * 2 * GRID_W] = pairs_ref[idx]


def neighbourhood_attention(q, k, v_src, v_col0, kc, vc_src, vc_col0, bias, bounded):
    s, bw = q.shape
    heads = bw // HEAD_DIM
    rows = s // GRID_W
    n_blocks = rows // NA_Q_ROWS
    assert n_blocks >= 3 and rows % NA_Q_ROWS == 0
    tq = NA_Q_ROWS * GRID_W
    lc = kc.shape[0]
    variant = lambda b: jnp.where(b == 0, 0, jnp.where(b == n_blocks - 1, 2, 1))
    return pl.pallas_call(
        functools.partial(_na_kernel, rows=rows, bounded=bounded),
        grid=(heads, n_blocks),
        in_specs=[
            pl.BlockSpec((tq, HEAD_DIM), lambda h, b: (b, h)),
            pl.BlockSpec((s, HEAD_DIM), lambda h, b: (0, h)),
            pl.BlockSpec((s, HEAD_DIM), lambda h, b: (0, v_col0 + h)),
            pl.BlockSpec((lc, HEAD_DIM), lambda h, b: (0, h)),
            pl.BlockSpec((lc, HEAD_DIM), lambda h, b: (0, vc_col0 + h)),
            pl.BlockSpec((None, None, tq, NA_K_ROWS * GRID_W), lambda h, b: (h, variant(b), 0, 0)),
        ],
        out_specs=pl.BlockSpec((tq, HEAD_DIM), lambda h, b: (b, h)),
        out_shape=jax.ShapeDtypeStruct((s, bw), BF16),
        compiler_params=_cparams("arbitrary", "arbitrary"),
        name="neighbourhood_attention",
    )(q, k, v_src, kc, vc_src, bias)


def _dense_attn_kernel(q_ref, k_ref, v_ref, o_ref):
    s = lax.dot_general(q_ref[...], k_ref[...], (((1,), (1,)), ((), ())), preferred_element_type=F32)
    p = jnp.exp2(s - jnp.max(s, axis=-1, keepdims=True))
    o = jnp.dot(p.astype(BF16), v_ref[...], preferred_element_type=F32)
    o_ref[...] = (o * (1.0 / jnp.sum(p, axis=-1, keepdims=True))).astype(o_ref.dtype)


def dense_attention(q, k, v_src, v_col0):
    m, bw = q.shape
    heads = bw // HEAD_DIM
    blk = lambda c0: pl.BlockSpec((m, HEAD_DIM), lambda h: (0, c0 + h))
    return pl.pallas_call(
        _dense_attn_kernel, grid=(heads,), in_specs=[blk(0), blk(0), blk(v_col0)], out_specs=blk(0),
        out_shape=jax.ShapeDtypeStruct((m, bw), BF16),
        compiler_params=_cparams("arbitrary"), name="dense_attention",
    )(q, k, v_src)


def _merge_up_kernel(od_ref, on_ref, cb_ref, cc_ref, cx_ref, ccp_ref, cxp_ref, ccn_ref, cxn_ref, cw_ref,
                     g0_ref, g1_ref, g2_ref, wd_ref, wn_ref, wc_ref, o_ref, oc_sc):
    i, j = pl.program_id(0), pl.program_id(1)

    @pl.when(j == 0)
    def _():
        u = cc_ref[...].astype(F32) * cx_ref[...].astype(F32)
        tm = u.shape[0]
        halo = ccp_ref.shape[0]
        prev_row = ccp_ref[halo - 1:halo, :].astype(F32) * cxp_ref[halo - 1:halo, :].astype(F32)
        next_row = ccn_ref[0:1, :].astype(F32) * cxn_ref[0:1, :].astype(F32)
        prev_row = jnp.where(i == 0, 0.0, prev_row)
        next_row = jnp.where(i == pl.num_programs(0) - 1, 0.0, next_row)
        r = lax.broadcasted_iota(jnp.int32, u.shape, 0)
        u_prev = jnp.where(r == 0, prev_row, pltpu.roll(u, 1, axis=0))
        u_next = jnp.where(r == tm - 1, next_row, pltpu.roll(u, tm - 1, axis=0))
        conv = cw_ref[0:1, :] * u_prev + cw_ref[1:2, :] * u + cw_ref[2:3, :] * u_next
        oc_sc[...] = (cb_ref[...].astype(F32) * conv).astype(BF16)

    acc = _sigmoid(g0_ref[...].astype(F32)) * jnp.dot(od_ref[...], wd_ref[...], preferred_element_type=F32)
    acc += _sigmoid(g1_ref[...].astype(F32)) * jnp.dot(on_ref[...], wn_ref[...], preferred_element_type=F32)
    acc += _sigmoid(g2_ref[...].astype(F32)) * jnp.dot(oc_sc[...], wc_ref[...], preferred_element_type=F32)
    o_ref[...] = acc.astype(o_ref.dtype)


def merge_up(o_diff, o_na, p, conv_w, w_up_diff, w_up_na, w_up_conv, layer, d):
    m, bw = o_diff.shape
    tm, tn = _tile(m, 512), math.gcd(_tile(d, 1024), bw)
    halo = 8
    n_halo = m // halo
    row_blk = lambda c: pl.BlockSpec((tm, bw), lambda i, j, c=c: (i, c))
    prev_blk = lambda c: pl.BlockSpec((halo, bw), lambda i, j, c=c: (jnp.maximum(i * (tm // halo) - 1, 0), c))
    next_blk = lambda c: pl.BlockSpec((halo, bw), lambda i, j, c=c: (jnp.minimum((i + 1) * (tm // halo), n_halo - 1), c))
    gate0 = 9 * bw // tn
    gate_blk = lambda b: pl.BlockSpec((tm, tn), lambda i, j, b=b: (i, gate0 + b * (d // tn) + j))
    w_blk = pl.BlockSpec((None, bw, tn), lambda i, j: (layer, 0, j))
    assert (9 * bw) % tn == 0
    return pl.pallas_call(
        _merge_up_kernel,
        grid=(m // tm, d // tn),
        in_specs=[row_blk(0), row_blk(0), row_blk(6), row_blk(7), row_blk(8),
                  prev_blk(7), prev_blk(8), next_blk(7), next_blk(8),
                  pl.BlockSpec((CONV_K, bw), lambda i, j: (0, 0)),
                  gate_blk(0), gate_blk(1), gate_blk(2), w_blk, w_blk, w_blk],
        out_specs=pl.BlockSpec((tm, tn), lambda i, j: (i, j)),
        out_shape=jax.ShapeDtypeStruct((m, d), BF16),
        scratch_shapes=[pltpu.VMEM((tm, bw), BF16)],
        compiler_params=_cparams("arbitrary", "arbitrary"),
        name="merge_up",
    )(o_diff, o_na, p, p, p, p, p, p, p, conv_w, p, p, p, w_up_diff, w_up_na, w_up_conv)


def kernel(x, c, ctx, c_ctx, w_mod_a, w_mod_b, b_mod, norm1_g, w_in, diff_qn_g, diff_kn_g, lambda_q1, lambda_k1, lambda_q2, lambda_k2, diff_subln_g, na_qn_g, na_kn_g, na_rpb, conv_w, w_up_diff, w_up_na, w_up_conv, w_o, norm2_g, router_w, router_b, w_gu, b_gu, w_down, b_down):
    batch, s, d = x.shape
    assert batch == 1 and ctx.shape[0] == 1
    depth = w_in.shape[0]
    bw = w_up_diff.shape[1]
    dv0 = 2 * bw // (2 * HEAD_DIM)
    nv0 = 5 * bw // HEAD_DIM
    rows = s // GRID_W

    cond = jnp.zeros((8, d), F32).at[0].set(c[0]).at[1].set(c_ctx)
    mod = modulation(cond, w_mod_a, w_mod_b, b_mod).reshape(depth, 8, N_MOD, d)
    rope = rope_tables(s)
    w_in, w_o = w_in.astype(BF16), w_o.astype(BF16)
    ups = tuple(w.astype(BF16) for w in (w_up_diff, w_up_na, w_up_conv))

    xl, xc = x[0], ctx[0]
    for i in range(depth):
        last = i == depth - 1
        lam_init = 0.8 - 0.6 * math.exp(-0.3 * i)
        lam = (jnp.exp(jnp.sum(lambda_q1[i] * lambda_k1[i])) - jnp.exp(jnp.sum(lambda_q2[i] * lambda_k2[i]))
               + lam_init)
        ml, mc = mod[i, 0], mod[i, 1]
        gains = jnp.stack([diff_qn_g[i], diff_kn_g[i], na_qn_g[i], na_kn_g[i]])

        p_l = project_in(prenorm(xl, norm1_g[i], ml[1], ml[0]), w_in, i)
        p_c = project_in(prenorm(xc, norm1_g[i], mc[1], mc[0]), w_in, i)
        dq_l, dk_l, nq_l, nk_l = qk_prep(p_l, gains, rope, bw)
        dq_c, dk_c, nq_c, nk_c = qk_prep(p_c, gains, None, bw)

        diff_args = (dq_l, dk_l, p_l, dv0, lam, diff_subln_g[i], 1.0 - lam_init, (dk_c, p_c, dv0))
        o_diff_l = lax.cond(score_bound_log2(diff_qn_g[i], diff_kn_g[i]) <= SCORE_BOUND_LOG2,
                            lambda: diff_attention_bounded(*diff_args), lambda: diff_attention(*diff_args))
        na_args = (nq_l, nk_l, p_l, nv0, nk_c, p_c, nv0, na_bias_tables(na_rpb[i], rows))
        na_bound = score_bound_log2(na_qn_g[i], na_kn_g[i]) + jnp.max(jnp.abs(na_rpb[i])) * LOG2E
        o_na_l = lax.cond(na_bound <= SCORE_BOUND_LOG2, lambda: neighbourhood_attention(*na_args, True),
                          lambda: neighbourhood_attention(*na_args, False))
        xl = project_out(merge_up(o_diff_l, o_na_l, p_l, conv_w[i], *ups, i, d), w_o, i, xl, ml[2])
        if not last:
            o_diff_c = diff_attention(dq_c, dk_c, p_c, dv0, lam, diff_subln_g[i], 1.0 - lam_init)
            o_na_c = dense_attention(nq_c, nk_c, p_c, nv0)
            xc = project_out(merge_up(o_diff_c, o_na_c, p_c, conv_w[i], *ups, i, d), w_o, i, xc, mc[2])

        router = (router_w[i], router_b[i])
        x_mid, xl, group = xl, None, s // MOE_GROUPS
        for gi in range(MOE_GROUPS):
            h_g, eid, ew, rank, counts = prenorm_route(x_mid, norm2_g[i], ml[4], ml[3], router, gi * group, group)
            y4 = moe_routed(h_g, eid, ew, rank, counts, w_gu, b_gu[i], w_down, b_down[i], i)
            xl = moe_combine(x_mid, ml[5], ew, y4, gi * group, xl)
        if not last:
            h_c, comb_c = prenorm(xc, norm2_g[i], mc[4], mc[3], router)
            xc = moe(h_c, comb_c, w_gu, b_gu[i], w_down, b_down[i], i, xc, mc[5])
    return xl[None]
```

```python
import functools
import math

import jax
import jax.numpy as jnp
from jax import lax
from jax.experimental import pallas as pl
from jax.experimental.pallas import tpu as pltpu

F32 = jnp.float32
BF16 = jnp.bfloat16
HIGHEST = lax.Precision.HIGHEST

GRID_W = 64
HEAD_DIM = 128
NA_WIN_R = 8
NA_WIN_C = 16
CONV_K = 3
N_BRANCH = 3
N_MOD = 6
TOP_K = 4
SWIGLU_LIMIT = 7.0
SWIGLU_ALPHA = 1.702
ROPE_BASE = 10000.0
EPS = 1e-6
LOG2E = math.log2(math.e)
Q_SCALE = HEAD_DIM ** -0.5 * LOG2E
MASKED = -1e30
SCORE_BOUND_LOG2 = 60.0

V7X_VMEM_LIMIT_BYTES = 56 * 1024 * 1024
NA_Q_ROWS = 8
NA_K_ROWS = NA_Q_ROWS + NA_WIN_R


def _cparams(*sem):
    return pltpu.CompilerParams(dimension_semantics=sem, vmem_limit_bytes=V7X_VMEM_LIMIT_BYTES)


def _tile(n, target):
    if n <= target:
        return n
    t = target
    while n % t:
        t -= 8
    assert t > 0, (n, target)
    return t


def _sigmoid(x):
    return 1.0 / (1.0 + jnp.exp(-x))


def _mod_kernel(c_ref, wa_ref, wb_ref, b_ref, o_ref):
    c = c_ref[...]
    s = c * _sigmoid(c)
    m1 = jnp.dot(s, wa_ref[...], precision=HIGHEST, preferred_element_type=F32)
    o_ref[...] = jnp.dot(m1, wb_ref[...], precision=HIGHEST, preferred_element_type=F32) + b_ref[...]


def modulation(cond, w_a, w_b, b):
    depth, d, r = w_a.shape
    width = w_b.shape[-1]
    rows = cond.shape[0]
    tn = _tile(width, 4096)
    return pl.pallas_call(
        _mod_kernel,
        grid=(depth, width // tn),
        in_specs=[
            pl.BlockSpec((rows, d), lambda l, j: (0, 0)),
            pl.BlockSpec((None, d, r), lambda l, j: (l, 0, 0)),
            pl.BlockSpec((None, r, tn), lambda l, j: (l, 0, j)),
            pl.BlockSpec((None, 1, tn), lambda l, j: (l, 0, j)),
        ],
        out_specs=pl.BlockSpec((None, rows, tn), lambda l, j: (l, 0, j)),
        out_shape=jax.ShapeDtypeStruct((depth, rows, width), F32),
        compiler_params=_cparams("arbitrary", "arbitrary"),
        name="modulation",
    )(cond, w_a, w_b, b.reshape(depth, 1, width))


def _modulated_norm(x_ref, g_ref, sc_ref, sh_ref):
    x = x_ref[...]
    y = x * lax.rsqrt(jnp.mean(x * x, axis=-1, keepdims=True) + EPS)
    return (y * g_ref[...]) * (1.0 + sc_ref[...]) + sh_ref[...]


def _prenorm_kernel(x_ref, g_ref, sc_ref, sh_ref, o_ref):
    o_ref[...] = _modulated_norm(x_ref, g_ref, sc_ref, sh_ref).astype(o_ref.dtype)


def _split_bf16(x):
    hi = x.astype(BF16)
    return hi, (x - hi.astype(F32)).astype(BF16)


def _route(h, rwh_ref, rwl_ref, rb_ref):
    h_hi, h_lo = _split_bf16(h)
    w_hi = rwh_ref[...]
    logits = (jnp.dot(h_hi, w_hi, preferred_element_type=F32) + jnp.dot(h_lo, w_hi, preferred_element_type=F32)
              + jnp.dot(h_hi, rwl_ref[...], preferred_element_type=F32) + rb_ref[...])
    n_exp = logits.shape[-1]
    lane = lax.broadcasted_iota(jnp.int32, logits.shape, 1).astype(F32)
    sels, ids, vals = [], [], []
    for _ in range(TOP_K):
        m = jnp.max(logits, axis=-1, keepdims=True)
        first = jnp.min(jnp.where(logits == m, lane, float(n_exp)), axis=-1, keepdims=True)
        sel = lane == first
        sels.append(sel)
        ids.append(first)
        vals.append(m)
        logits = jnp.where(sel, -jnp.inf, logits)
    es = [jnp.exp(v - vals[0]) for v in vals]
    inv = 1.0 / functools.reduce(lambda a, b: a + b, es)
    return sels, ids, [e * inv for e in es]


def _prenorm_router_kernel(x_ref, g_ref, sc_ref, sh_ref, rwh_ref, rwl_ref, rb_ref, o_ref, comb_ref):
    h = _modulated_norm(x_ref, g_ref, sc_ref, sh_ref)
    o_ref[...] = h.astype(o_ref.dtype)
    sels, _, wts = _route(h, rwh_ref, rwl_ref, rb_ref)
    comb = jnp.zeros(comb_ref.shape, F32)
    for sel, w in zip(sels, wts):
        comb = comb + jnp.where(sel, w, 0.0)
    comb_ref[...] = comb


def _columns(cols, width):
    lane = lax.broadcasted_iota(jnp.int32, (cols[0].shape[0], width), 1)
    out = jnp.zeros(lane.shape, cols[0].dtype)
    for k, c in enumerate(cols):
        out = jnp.where(lane == k, c, out)
    return out


def _prenorm_route_kernel(x_ref, g_ref, sc_ref, sh_ref, rwh_ref, rwl_ref, rb_ref,
                          o_ref, eid_ref, ew_ref, rank_ref, cnt_ref, carry_sc):
    @pl.when(pl.program_id(0) == 0)
    def _():
        carry_sc[...] = jnp.zeros(carry_sc.shape, F32)

    h = _modulated_norm(x_ref, g_ref, sc_ref, sh_ref)
    o_ref[...] = h.astype(o_ref.dtype)
    sels, ids, wts = _route(h, rwh_ref, rwl_ref, rb_ref)
    tm = h.shape[0]
    chosen = functools.reduce(lambda a, b: a + b, [jnp.where(s, 1.0, 0.0) for s in sels])
    r = lax.broadcasted_iota(jnp.int32, (tm, tm), 0)
    c = lax.broadcasted_iota(jnp.int32, (tm, tm), 1)
    earlier = jnp.where(c < r, 1.0, 0.0).astype(BF16)
    before = jnp.dot(earlier, chosen.astype(BF16), preferred_element_type=F32) + carry_sc[...]
    ranks = [jnp.sum(jnp.where(s, before, 0.0), axis=-1, keepdims=True) for s in sels]
    carry_sc[...] += jnp.sum(chosen, axis=0, keepdims=True)
    eid_ref[...] = _columns(ids, TOP_K).astype(jnp.int32)
    ew_ref[...] = _columns(wts, TOP_K)
    rank_ref[...] = _columns(ranks, TOP_K).astype(jnp.int32)
    cnt_ref[...] = carry_sc[...].astype(jnp.int32)


def prenorm(x, g, scale, shift, router=None):
    m, d = x.shape
    tm = _tile(m, 256)
    row = lambda i: (i, 0)
    fixed = lambda i: (0, 0)
    vec = pl.BlockSpec((1, d), fixed)
    in_specs = [pl.BlockSpec((tm, d), row), vec, vec, vec]
    args = [x, g.reshape(1, d), scale.reshape(1, d), shift.reshape(1, d)]
    if router is None:
        return pl.pallas_call(
            _prenorm_kernel, grid=(m // tm,), in_specs=in_specs,
            out_specs=pl.BlockSpec((tm, d), row),
            out_shape=jax.ShapeDtypeStruct((m, d), BF16),
            compiler_params=_cparams("arbitrary"), name="prenorm",
        )(*args)
    rw, rb = router
    n_exp = rw.shape[-1]
    in_specs += [pl.BlockSpec((d, n_exp), fixed)] * 2 + [pl.BlockSpec((1, n_exp), fixed)]
    args += [*_split_bf16(rw), rb.reshape(1, n_exp)]
    return pl.pallas_call(
        _prenorm_router_kernel, grid=(m // tm,), in_specs=in_specs,
        out_specs=[pl.BlockSpec((tm, d), row), pl.BlockSpec((tm, n_exp), row)],
        out_shape=[jax.ShapeDtypeStruct((m, d), BF16), jax.ShapeDtypeStruct((m, n_exp), F32)],
        compiler_params=_cparams("arbitrary"), name="prenorm_router",
    )(*args)


def prenorm_route(x, g, scale, shift, router):
    m, d = x.shape
    tm = _tile(m, 256)
    rw, rb = router
    n_exp = rw.shape[-1]
    row = lambda i: (i, 0)
    fixed = lambda i: (0, 0)
    vec = pl.BlockSpec((1, d), fixed)
    per_tok = pl.BlockSpec((tm, TOP_K), row)
    return pl.pallas_call(
        _prenorm_route_kernel, grid=(m // tm,),
        in_specs=[pl.BlockSpec((tm, d), row), vec, vec, vec,
                  pl.BlockSpec((d, n_exp), fixed), pl.BlockSpec((d, n_exp), fixed),
                  pl.BlockSpec((1, n_exp), fixed)],
        out_specs=[pl.BlockSpec((tm, d), row), per_tok, per_tok, per_tok, pl.BlockSpec((1, n_exp), fixed)],
        out_shape=[jax.ShapeDtypeStruct((m, d), BF16), jax.ShapeDtypeStruct((m, TOP_K), jnp.int32),
                   jax.ShapeDtypeStruct((m, TOP_K), F32), jax.ShapeDtypeStruct((m, TOP_K), jnp.int32),
                   jax.ShapeDtypeStruct((1, n_exp), jnp.int32)],
        scratch_shapes=[pltpu.VMEM((1, n_exp), F32)],
        compiler_params=_cparams("arbitrary"), name="prenorm_route",
    )(x, g.reshape(1, d), scale.reshape(1, d), shift.reshape(1, d), *_split_bf16(rw), rb.reshape(1, n_exp))


EXPERT_TILE = 256


def _expert_mlp_kernel(te_ref, nt_ref, x_ref, wgu_ref, bgu_ref, wd_ref, bd_ref, o_ref, wgu_sc, wd_sc):
    j = pl.program_id(0)
    used = j < nt_ref[0]

    @pl.when(jnp.logical_not(used))
    def _():
        o_ref[...] = jnp.zeros(o_ref.shape, o_ref.dtype)

    @pl.when(jnp.logical_and(used, jnp.logical_or(j == 0, te_ref[j] != te_ref[jnp.maximum(j - 1, 0)])))
    def _():
        wgu_sc[...] = wgu_ref[...].astype(BF16)
        wd_sc[...] = wd_ref[...].astype(BF16)

    @pl.when(used)
    def _():
        d_exp = wd_ref.shape[0]
        gu = jnp.dot(x_ref[...], wgu_sc[...], preferred_element_type=F32) + bgu_ref[...]
        gate = jnp.minimum(gu[:, :d_exp], SWIGLU_LIMIT)
        lin = jnp.clip(gu[:, d_exp:], -SWIGLU_LIMIT, SWIGLU_LIMIT)
        act = gate * _sigmoid(SWIGLU_ALPHA * gate) * (lin + 1.0)
        y = jnp.dot(act.astype(BF16), wd_sc[...], preferred_element_type=F32) + bd_ref[...]
        o_ref[...] = y.astype(o_ref.dtype)


def expert_mlp(xg, tile_expert, n_tiles_used, w_gu, b_gu, w_down, b_down, layer):
    p_rows, d = xg.shape
    _, n_exp, _, two_de = w_gu.shape
    d_exp = two_de // 2
    t = EXPERT_TILE
    grid_spec = pltpu.PrefetchScalarGridSpec(
        num_scalar_prefetch=2,
        grid=(p_rows // t,),
        in_specs=[
            pl.BlockSpec((t, d), lambda j, te, nt: (j, 0)),
            pl.BlockSpec((None, None, d, two_de), lambda j, te, nt: (layer, te[j], 0, 0)),
            pl.BlockSpec((None, 1, two_de), lambda j, te, nt: (te[j], 0, 0)),
            pl.BlockSpec((None, None, d_exp, d), lambda j, te, nt: (layer, te[j], 0, 0)),
            pl.BlockSpec((None, 1, d), lambda j, te, nt: (te[j], 0, 0)),
        ],
        out_specs=pl.BlockSpec((t, d), lambda j, te, nt: (j, 0)),
        scratch_shapes=[pltpu.VMEM((d, two_de), BF16), pltpu.VMEM((d_exp, d), BF16)],
    )
    return pl.pallas_call(
        _expert_mlp_kernel, grid_spec=grid_spec,
        out_shape=jax.ShapeDtypeStruct((p_rows, d), BF16),
        compiler_params=_cparams("arbitrary"), name="expert_mlp",
    )(tile_expert, n_tiles_used, xg, w_gu, b_gu.reshape(n_exp, 1, two_de), w_down, b_down.reshape(n_exp, 1, d))


def _combine_kernel(x_ref, gate_ref, ew_ref, *rest, next_norm):
    y_refs, rest = rest[:TOP_K], rest[TOP_K:]
    ew = ew_ref[...]
    acc = ew[:, 0:1] * y_refs[0][...].astype(F32)
    for k in range(1, TOP_K):
        acc = acc + ew[:, k:k + 1] * y_refs[k][...].astype(F32)
    if next_norm:
        g_ref, sc_ref, sh_ref, o_ref, h_ref = rest
        o_ref[...] = x_ref[...] + gate_ref[...] * acc
        h_ref[...] = _modulated_norm(o_ref, g_ref, sc_ref, sh_ref).astype(h_ref.dtype)
    else:
        (o_ref,) = rest
        o_ref[...] = x_ref[...] + gate_ref[...] * acc


def moe_combine(x, gate, ew, y4, next_norm=None):
    m, d = x.shape
    tm = _tile(m, 128)
    rows = pl.BlockSpec((tm, d), lambda i: (i, 0))
    vec = pl.BlockSpec((1, d), lambda i: (0, 0))
    in_specs = ([rows, vec, pl.BlockSpec((tm, TOP_K), lambda i: (i, 0))]
                + [pl.BlockSpec((None, tm, d), lambda i, k=k: (k, i, 0)) for k in range(TOP_K)])
    args = [x, gate.reshape(1, d), ew] + [y4] * TOP_K
    out_specs, out_shape = [rows], [jax.ShapeDtypeStruct((m, d), F32)]
    if next_norm is not None:
        in_specs += [vec] * 3
        args += [v.reshape(1, d) for v in next_norm]
        out_specs.append(rows)
        out_shape.append(jax.ShapeDtypeStruct((m, d), BF16))
    out = pl.pallas_call(
        functools.partial(_combine_kernel, next_norm=next_norm is not None), grid=(m // tm,),
        in_specs=in_specs, out_specs=out_specs, out_shape=out_shape,
        compiler_params=_cparams("arbitrary"), name="moe_combine",
    )(*args)
    return out if next_norm is not None else (out[0], None)


def moe_routed(h, eid, ew, rank, counts, w_gu, b_gu, w_down, b_down, layer):
    m, d = h.shape
    n_exp = counts.shape[-1]
    t = EXPERT_TILE
    p_rows = m * TOP_K + n_exp * t
    counts = counts.reshape(n_exp)
    tiles_per = (counts + t - 1) // t
    tile_end = jnp.cumsum(tiles_per)
    start = (tile_end - tiles_per) * t
    pos = start[eid] + rank
    tile_id = jnp.arange(p_rows // t, dtype=jnp.int32)
    tile_expert = jnp.minimum(jnp.sum((tile_end[None, :] <= tile_id[:, None]).astype(jnp.int32), axis=1),
                              n_exp - 1)
    n_used = tile_end[-1:].astype(jnp.int32)
    pos_t = pos.T.reshape(-1)
    src = (jnp.arange(p_rows, dtype=jnp.int32) % m).at[pos_t].set(
        jnp.tile(jnp.arange(m, dtype=jnp.int32), TOP_K), unique_indices=True, mode="promise_in_bounds")
    xg = h.at[src].get(mode="promise_in_bounds")
    yg = expert_mlp(xg, tile_expert, n_used, w_gu, b_gu, w_down, b_down, layer)
    return yg.at[pos_t].get(mode="promise_in_bounds", unique_indices=True).reshape(TOP_K, m, d)


def _mm_kernel(*refs, epilogue):
    a_ref, w_ref = refs[0], refs[1]
    extras, o_ref = refs[2:-1], refs[-1]
    acc = jnp.dot(a_ref[...], w_ref[...].astype(BF16), preferred_element_type=F32)
    o_ref[...] = epilogue(acc, *extras).astype(o_ref.dtype)


def matmul(a, w, w_spec, tm, tn, n_col_tiles, extras, epilogue, out_spec, out_shape, name):
    m, k = a.shape
    return pl.pallas_call(
        functools.partial(_mm_kernel, epilogue=epilogue),
        grid=(m // tm, n_col_tiles),
        in_specs=[pl.BlockSpec((tm, k), lambda i, j: (i, 0)), w_spec] + [s for _, s in extras],
        out_specs=out_spec,
        out_shape=out_shape,
        compiler_params=_cparams("arbitrary", "arbitrary"),
        name=name,
    )(a, w, *[x for x, _ in extras])


def _cast_epilogue(acc):
    return acc


def _residual_epilogue(acc, x_ref, gate_ref):
    return x_ref[...] + gate_ref[...] * acc


def _swiglu_epilogue(acc, b_ref, comb_ref):
    d_exp = acc.shape[-1] // 2
    gu = acc + b_ref[...]
    gate = jnp.minimum(gu[:, :d_exp], SWIGLU_LIMIT)
    lin = jnp.clip(gu[:, d_exp:], -SWIGLU_LIMIT, SWIGLU_LIMIT)
    act = gate * _sigmoid(SWIGLU_ALPHA * gate) * (lin + 1.0)
    comb = comb_ref[...]
    lane = lax.broadcasted_iota(jnp.int32, comb.shape, 1)
    cw = jnp.sum(jnp.where(lane == pl.program_id(1), comb, 0.0), axis=-1, keepdims=True)
    return act * cw


def _moe_down_epilogue(acc, comb_ref, bd_ref, x_ref, gate_ref):
    bias = jnp.dot(comb_ref[...].astype(BF16), bd_ref[...].astype(BF16), preferred_element_type=F32)
    return x_ref[...] + gate_ref[...] * (acc + bias)


def project_in(h, w_in, layer):
    m, d = h.shape
    width = w_in.shape[-1]
    tm, tn = _tile(m, 1024), _tile(width, 1024)
    return matmul(h, w_in, pl.BlockSpec((None, d, tn), lambda i, j: (layer, 0, j)), tm, tn, width // tn, [],
                  _cast_epilogue, pl.BlockSpec((tm, tn), lambda i, j: (i, j)),
                  jax.ShapeDtypeStruct((m, width), BF16), "project_in")


def project_out(merged, w_o, layer, x, gate):
    m, d = x.shape
    tm, tn = _tile(m, 1024), _tile(d, 512)
    tile = pl.BlockSpec((tm, tn), lambda i, j: (i, j))
    return matmul(merged, w_o, pl.BlockSpec((None, merged.shape[1], tn), lambda i, j: (layer, 0, j)),
                  tm, tn, d // tn,
                  [(x, tile), (gate.reshape(1, d), pl.BlockSpec((1, tn), lambda i, j: (0, j)))],
                  _residual_epilogue, tile, jax.ShapeDtypeStruct((m, d), F32), "project_out")


def moe(h, comb, w_gu, b_gu, w_down, b_down, layer, x, gate):
    m, d = h.shape
    depth, n_exp, _, two_de = w_gu.shape
    d_exp = two_de // 2
    tm = _tile(m, 1024)
    act = matmul(
        h, w_gu, pl.BlockSpec((None, None, d, two_de), lambda i, j: (layer, j, 0, 0)), tm, two_de, n_exp,
        [(b_gu.reshape(n_exp, 1, two_de), pl.BlockSpec((None, 1, two_de), lambda i, j: (j, 0, 0))),
         (comb, pl.BlockSpec((tm, n_exp), lambda i, j: (i, 0)))],
        _swiglu_epilogue, pl.BlockSpec((tm, d_exp), lambda i, j: (i, j)),
        jax.ShapeDtypeStruct((m, n_exp * d_exp), BF16), "moe_up")
    tm, tn = _tile(m, 512), _tile(d, 256)
    tile = pl.BlockSpec((tm, tn), lambda i, j: (i, j))
    return matmul(
        act, w_down.reshape(depth, n_exp * d_exp, d),
        pl.BlockSpec((None, n_exp * d_exp, tn), lambda i, j: (layer, 0, j)), tm, tn, d // tn,
        [(comb, pl.BlockSpec((tm, n_exp), lambda i, j: (i, 0))),
         (b_down, pl.BlockSpec((n_exp, tn), lambda i, j: (0, j))),
         (x, tile), (gate.reshape(1, d), pl.BlockSpec((1, tn), lambda i, j: (0, j)))],
        _moe_down_epilogue, tile, jax.ShapeDtypeStruct((m, d), F32), "moe_down")


def _chunk_norm(x, g):
    return x * lax.rsqrt(jnp.mean(x * x, axis=-1, keepdims=True) + EPS) * g


def _qk_prep_kernel(*refs, rope):
    if rope:
        dq_ref, dk_ref, nq_ref, nk_ref, g_ref, cos_ref, sin_ref = refs[:7]
        cos, sin = cos_ref[...], sin_ref[...]
        lane = lax.broadcasted_iota(jnp.int32, cos.shape, 1)
        lower = (lane & (HEAD_DIM // 4)) == 0
    else:
        dq_ref, dk_ref, nq_ref, nk_ref, g_ref = refs[:5]
    outs = refs[-4:]
    ins = (dq_ref, dk_ref, nq_ref, nk_ref)
    for idx in range(4):
        g = g_ref[idx:idx + 1, :]
        rot = rope and idx < 2
        scale = Q_SCALE if idx in (0, 2) else 1.0
        for c in range(ins[idx].shape[-1] // HEAD_DIM):
            sl = slice(c * HEAD_DIM, (c + 1) * HEAD_DIM)
            y = _chunk_norm(ins[idx][:, sl].astype(F32), g)
            if rot:
                swapped = jnp.where(lower, pltpu.roll(y, HEAD_DIM - HEAD_DIM // 4, axis=1),
                                    pltpu.roll(y, HEAD_DIM // 4, axis=1))
                y = y * cos + swapped * sin
            outs[idx][:, sl] = (y * scale).astype(BF16)


def qk_prep(p, gains, rope, bw):
    m = p.shape[0]
    tm = _tile(m, 512)
    col = lambda c: pl.BlockSpec((tm, bw), lambda i, c=c: (i, c))
    in_specs = [col(0), col(1), col(3), col(4), pl.BlockSpec((4, HEAD_DIM), lambda i: (0, 0))]
    args = [p, p, p, p, gains]
    if rope is not None:
        in_specs += [pl.BlockSpec((tm, HEAD_DIM), lambda i: (i, 0))] * 2
        args += list(rope)
    return pl.pallas_call(
        functools.partial(_qk_prep_kernel, rope=rope is not None),
        grid=(m // tm,), in_specs=in_specs,
        out_specs=[pl.BlockSpec((tm, bw), lambda i: (i, 0))] * 4,
        out_shape=[jax.ShapeDtypeStruct((m, bw), BF16)] * 4,
        compiler_params=_cparams("arbitrary"), name="qk_prep",
    )(*args)


def rope_tables(n):
    t = jnp.arange(n, dtype=jnp.int32)
    row = (t // GRID_W).astype(F32)
    col = (t % GRID_W).astype(F32)
    n_freq = HEAD_DIM // 4
    inv = ROPE_BASE ** (-jnp.arange(n_freq, dtype=F32) / n_freq)
    ar, ac = row[:, None] * inv, col[:, None] * inv
    ang = jnp.concatenate([ar, ar, ac, ac], axis=-1)
    lane = jnp.arange(HEAD_DIM)
    sign = jnp.where((lane & n_freq) == 0, -1.0, 1.0).astype(F32)
    return jnp.cos(ang), jnp.sin(ang) * sign


def _diff_attn_kernel(lam_ref, q_ref, k_ref, v_ref, *rest, n_main, has_extra, out_scale):
    if has_extra:
        kx_ref, vx_ref, g_ref, o_ref, m_sc, l_sc, acc_sc = rest
    else:
        g_ref, o_ref, m_sc, l_sc, acc_sc = rest
    kj = pl.program_id(2)
    n_steps = n_main + (1 if has_extra else 0)

    @pl.when(kj == 0)
    def _():
        m_sc[...] = jnp.full(m_sc.shape, -jnp.inf, F32)
        l_sc[...] = jnp.zeros(l_sc.shape, F32)
        acc_sc[...] = jnp.zeros(acc_sc.shape, F32)

    def update(kr, vr):
        v = vr[...]
        for hf in range(2):
            sl = slice(hf * HEAD_DIM, (hf + 1) * HEAD_DIM)
            s = lax.dot_general(q_ref[:, sl], kr[:, sl], (((1,), (1,)), ((), ())),
                                preferred_element_type=F32)
            m_prev = m_sc[hf]
            m_new = jnp.maximum(m_prev, jnp.max(s, axis=-1, keepdims=True))
            alpha = jnp.exp2(m_prev - m_new)
            p = jnp.exp2(s - m_new)
            l_sc[hf] = alpha * l_sc[hf] + jnp.sum(p, axis=-1, keepdims=True)
            acc_sc[hf] = alpha * acc_sc[hf] + jnp.dot(p.astype(BF16), v, preferred_element_type=F32)
            m_sc[hf] = m_new

    if has_extra:
        pl.when(kj < n_main)(lambda: update(k_ref, v_ref))
        pl.when(kj == n_main)(lambda: update(kx_ref, vx_ref))
    else:
        update(k_ref, v_ref)

    @pl.when(kj == n_steps - 1)
    def _():
        _diff_finalize(lam_ref, g_ref, o_ref, acc_sc[0], acc_sc[1], l_sc[0], l_sc[1], out_scale)


def _diff_finalize(lam_ref, g_ref, o_ref, acc0, acc1, l0, l1, out_scale):
    lam = lam_ref[0, 0]
    o = acc0 * (1.0 / l0) - lam * (acc1 * (1.0 / l1))
    y = o * lax.rsqrt(jnp.mean(o * o, axis=-1, keepdims=True) + EPS)
    o_ref[...] = (y * g_ref[...] * out_scale).astype(o_ref.dtype)


def _diff_attn_bounded_kernel(lam_ref, q_ref, k_ref, v_ref, kx_ref, vx_ref, g_ref, o_ref, l_sc, acc_sc,
                              *, tk, n_main, out_scale):
    l_sc[...] = jnp.zeros(l_sc.shape, F32)
    acc_sc[...] = jnp.zeros(acc_sc.shape, F32)

    def accumulate(k, v):
        for hf in range(2):
            sl = slice(hf * HEAD_DIM, (hf + 1) * HEAD_DIM)
            s = lax.dot_general(q_ref[:, sl], k[:, sl], (((1,), (1,)), ((), ())), preferred_element_type=F32)
            p = jnp.exp2(s)
            part = p[:, 0:HEAD_DIM]
            for c in range(1, p.shape[1] // HEAD_DIM):
                part = part + p[:, c * HEAD_DIM:(c + 1) * HEAD_DIM]
            l_sc[hf] += part
            acc_sc[hf] += jnp.dot(p.astype(BF16), v, preferred_element_type=F32)

    def body(j, carry):
        off = pl.multiple_of(j * tk, tk)
        accumulate(k_ref[pl.ds(off, tk), :], v_ref[pl.ds(off, tk), :])
        return carry

    lax.fori_loop(0, n_main, body, 0, unroll=math.gcd(n_main, 4))
    accumulate(kx_ref[...], vx_ref[...])
    _diff_finalize(lam_ref, g_ref, o_ref, acc_sc[0], acc_sc[1],
                   jnp.sum(l_sc[0], axis=-1, keepdims=True), jnp.sum(l_sc[1], axis=-1, keepdims=True), out_scale)


def diff_attention_bounded(q, k, v_src, v_col0, lam, sub_g, out_scale, extra):
    mq, bw = q.shape
    mk = k.shape[0]
    hw = 2 * HEAD_DIM
    tq, tk = _tile(mq, 1024), _tile(mk, 1024)
    kx, vx_src, vx_col0 = extra
    mx = kx.shape[0]
    once = pl.Buffered(1)
    return pl.pallas_call(
        functools.partial(_diff_attn_bounded_kernel, tk=tk, n_main=mk // tk, out_scale=out_scale),
        grid=(bw // hw, mq // tq),
        in_specs=[
            pl.BlockSpec(memory_space=pltpu.SMEM),
            pl.BlockSpec((tq, hw), lambda h, i: (i, h)),
            pl.BlockSpec((mk, hw), lambda h, i: (0, h), pipeline_mode=once),
            pl.BlockSpec((mk, hw), lambda h, i: (0, v_col0 + h), pipeline_mode=once),
            pl.BlockSpec((mx, hw), lambda h, i: (0, h)),
            pl.BlockSpec((mx, hw), lambda h, i: (0, vx_col0 + h)),
            pl.BlockSpec((1, hw), lambda h, i: (0, 0)),
        ],
        out_specs=pl.BlockSpec((tq, hw), lambda h, i: (i, h)),
        out_shape=jax.ShapeDtypeStruct((mq, bw), BF16),
        scratch_shapes=[pltpu.VMEM((2, tq, HEAD_DIM), F32), pltpu.VMEM((2, tq, hw), F32)],
        compiler_params=_cparams("arbitrary", "arbitrary"),
        name="diff_attention_bounded",
    )(lam.reshape(1, 1).astype(F32), q, k, v_src, kx, vx_src, sub_g.reshape(1, hw))


def score_bound_log2(g_q, g_k):
    return Q_SCALE * HEAD_DIM * jnp.max(jnp.abs(g_q)) * jnp.max(jnp.abs(g_k)) * 1.01


def diff_attention(q, k, v_src, v_col0, lam, sub_g, out_scale, extra=None):
    mq, bw = q.shape
    mk = k.shape[0]
    hw = 2 * HEAD_DIM
    heads = bw // hw
    tq, tk = _tile(mq, 1024), _tile(mk, 1024)
    n_main = mk // tk
    last = n_main - 1
    kmap = lambda h, i, j: (jnp.minimum(j, last), h)
    in_specs = [
        pl.BlockSpec(memory_space=pltpu.SMEM),
        pl.BlockSpec((tq, hw), lambda h, i, j: (i, h)),
        pl.BlockSpec((tk, hw), kmap),
        pl.BlockSpec((tk, hw), lambda h, i, j: (jnp.minimum(j, last), v_col0 + h)),
    ]
    args = [lam.reshape(1, 1).astype(F32), q, k, v_src]
    if extra is not None:
        kx, vx_src, vx_col0 = extra
        mx = kx.shape[0]
        in_specs += [pl.BlockSpec((mx, hw), lambda h, i, j: (0, h)),
                     pl.BlockSpec((mx, hw), lambda h, i, j: (0, vx_col0 + h))]
        args += [kx, vx_src]
    in_specs.append(pl.BlockSpec((1, hw), lambda h, i, j: (0, 0)))
    args.append(sub_g.reshape(1, hw))
    return pl.pallas_call(
        functools.partial(_diff_attn_kernel, n_main=n_main, has_extra=extra is not None, out_scale=out_scale),
        grid=(heads, mq // tq, n_main + (extra is not None)),
        in_specs=in_specs,
        out_specs=pl.BlockSpec((tq, hw), lambda h, i, j: (i, h)),
        out_shape=jax.ShapeDtypeStruct((mq, bw), BF16),
        scratch_shapes=[pltpu.VMEM((2, tq, 1), F32), pltpu.VMEM((2, tq, 1), F32), pltpu.VMEM((2, tq, hw), F32)],
        compiler_params=_cparams("arbitrary", "arbitrary", "arbitrary"),
        name="diff_attention",
    )(*args)


def _na_kernel(q_ref, k_ref, v_ref, kc_ref, vc_ref, bias_ref, o_ref, *, rows, bounded):
    b = pl.program_id(1)
    start_row = jnp.clip(b * NA_Q_ROWS - NA_WIN_R // 2, 0, rows - NA_K_ROWS)
    start = pl.multiple_of(start_row * GRID_W, GRID_W * (NA_WIN_R // 2))
    n_keys = NA_K_ROWS * GRID_W
    q = q_ref[...]
    k = k_ref[pl.ds(start, n_keys), :]
    v = v_ref[pl.ds(start, n_keys), :]
    nt = (((1,), (1,)), ((), ()))
    s_loc = lax.dot_general(q, k, nt, preferred_element_type=F32) + bias_ref[...]
    s_ctx = lax.dot_general(q, kc_ref[...], nt, preferred_element_type=F32)
    if bounded:
        p_loc, p_ctx = jnp.exp2(s_loc), jnp.exp2(s_ctx)
    else:
        m = jnp.maximum(jnp.max(s_loc, axis=-1, keepdims=True), jnp.max(s_ctx, axis=-1, keepdims=True))
        p_loc, p_ctx = jnp.exp2(s_loc - m), jnp.exp2(s_ctx - m)
    denom = jnp.sum(p_loc, axis=-1, keepdims=True) + jnp.sum(p_ctx, axis=-1, keepdims=True)
    o = (jnp.dot(p_loc.astype(BF16), v, preferred_element_type=F32)
         + jnp.dot(p_ctx.astype(BF16), vc_ref[...], preferred_element_type=F32))
    o_ref[...] = (o * (1.0 / denom)).astype(o_ref.dtype)


def na_bias_tables(rpb, rows):
    n_blocks = rows // NA_Q_ROWS
    wr = min(NA_WIN_R, rows)
    w = jnp.arange(GRID_W)
    col_start = jnp.clip(w - NA_WIN_C // 2, 0, GRID_W - NA_WIN_C)
    kc = jnp.arange(GRID_W)
    valid_c = (kc[None, :] >= col_start[:, None]) & (kc[None, :] < col_start[:, None] + NA_WIN_C)
    rel_c = jnp.clip(kc[None, :] - w[:, None] + (NA_WIN_C - 1), 0, 2 * NA_WIN_C - 2)
    heads, n_rel = rpb.shape[0], rpb.shape[1]
    blocks = jnp.where(valid_c[None, None], rpb[:, :, rel_c] * LOG2E, MASKED).astype(F32)
    masked = jnp.full((heads, n_rel, GRID_W, GRID_W), MASKED, F32)
    shifted = jnp.concatenate([blocks[:, 1:], masked[:, :1]], axis=1)
    pairs = jnp.concatenate([jnp.concatenate([blocks, shifted], axis=-1),
                             jnp.concatenate([masked, blocks], axis=-1),
                             jnp.concatenate([blocks, masked], axis=-1),
                             jnp.concatenate([masked[:, :1], masked[:, :1]], axis=-1)], axis=1)
    plan = []
    for blk in (0, 1, n_blocks - 1):
        start = min(max(blk * NA_Q_ROWS - NA_WIN_R // 2, 0), rows - NA_K_ROWS)
        per_row = []
        for r_i in range(NA_Q_ROWS):
            r = blk * NA_Q_ROWS + r_i
            row_start = min(max(r - NA_WIN_R // 2, 0), rows - wr)
            rel = lambda kr: kr - r + (NA_WIN_R - 1)
            inside = lambda kr: row_start <= kr < row_start + wr
            per_pair = []
            for j in range(NA_K_ROWS // 2):
                left, right = start + 2 * j, start + 2 * j + 1
                if inside(left) and inside(right):
                    per_pair.append(rel(left))
                elif inside(right):
                    per_pair.append(n_rel + rel(right))
                elif inside(left):
                    per_pair.append(2 * n_rel + rel(left))
                else:
                    per_pair.append(3 * n_rel)
            per_row.append(per_pair)
        plan.append(per_row)
    assert 2 * GRID_W == HEAD_DIM
    tq, tkeys = NA_Q_ROWS * GRID_W, NA_K_ROWS * GRID_W
    return pl.pallas_call(
        functools.partial(_na_table_kernel, plan=plan),
        grid=(heads,),
        in_specs=[pl.BlockSpec((None, 3 * n_rel + 1, GRID_W, 2 * GRID_W), lambda h: (h, 0, 0, 0))],
        out_specs=pl.BlockSpec((None, 3, tq, tkeys), lambda h: (h, 0, 0, 0)),
        out_shape=jax.ShapeDtypeStruct((heads, 3, tq, tkeys), F32),
        compiler_params=_cparams("arbitrary"), name="na_bias_tables",
    )(pairs)


def _na_table_kernel(pairs_ref, o_ref, *, plan):
    for v, per_row in enumerate(plan):
        for r_i, per_pair in enumerate(per_row):
            for j, idx in enumerate(per_pair):
                o_ref[v, r_i * GRID_W:(r_i + 1) * GRID_W, j * 2 * GRID_W:(j + 1) * 2 * GRID_W] = pairs_ref[idx]


def neighbourhood_attention(q, k, v_src, v_col0, kc, vc_src, vc_col0, bias, bounded):
    s, bw = q.shape
    heads = bw // HEAD_DIM
    rows = s // GRID_W
    n_blocks = rows // NA_Q_ROWS
    assert n_blocks >= 3 and rows % NA_Q_ROWS == 0
    tq = NA_Q_ROWS * GRID_W
    lc = kc.shape[0]
    variant = lambda b: jnp.where(b == 0, 0, jnp.where(b == n_blocks - 1, 2, 1))
    return pl.pallas_call(
        functools.partial(_na_kernel, rows=rows, bounded=bounded),
        grid=(heads, n_blocks),
        in_specs=[
            pl.BlockSpec((tq, HEAD_DIM), lambda h, b: (b, h)),
            pl.BlockSpec((s, HEAD_DIM), lambda h, b: (0, h)),
            pl.BlockSpec((s, HEAD_DIM), lambda h, b: (0, v_col0 + h)),
            pl.BlockSpec((lc, HEAD_DIM), lambda h, b: (0, h)),
            pl.BlockSpec((lc, HEAD_DIM), lambda h, b: (0, vc_col0 + h)),
            pl.BlockSpec((None, None, tq, NA_K_ROWS * GRID_W), lambda h, b: (h, variant(b), 0, 0)),
        ],
        out_specs=pl.BlockSpec((tq, HEAD_DIM), lambda h, b: (b, h)),
        out_shape=jax.ShapeDtypeStruct((s, bw), BF16),
        compiler_params=_cparams("arbitrary", "arbitrary"),
        name="neighbourhood_attention",
    )(q, k, v_src, kc, vc_src, bias)


def _dense_attn_kernel(q_ref, k_ref, v_ref, o_ref):
    s = lax.dot_general(q_ref[...], k_ref[...], (((1,), (1,)), ((), ())), preferred_element_type=F32)
    p = jnp.exp2(s - jnp.max(s, axis=-1, keepdims=True))
    o = jnp.dot(p.astype(BF16), v_ref[...], preferred_element_type=F32)
    o_ref[...] = (o * (1.0 / jnp.sum(p, axis=-1, keepdims=True))).astype(o_ref.dtype)


def dense_attention(q, k, v_src, v_col0):
    m, bw = q.shape
    heads = bw // HEAD_DIM
    blk = lambda c0: pl.BlockSpec((m, HEAD_DIM), lambda h: (0, c0 + h))
    return pl.pallas_call(
        _dense_attn_kernel, grid=(heads,), in_specs=[blk(0), blk(0), blk(v_col0)], out_specs=blk(0),
        out_shape=jax.ShapeDtypeStruct((m, bw), BF16),
        compiler_params=_cparams("arbitrary"), name="dense_attention",
    )(q, k, v_src)


def _merge_up_kernel(od_ref, on_ref, cb_ref, cc_ref, cx_ref, ccp_ref, cxp_ref, ccn_ref, cxn_ref, cw_ref,
                     g0_ref, g1_ref, g2_ref, wd_ref, wn_ref, wc_ref, o_ref, oc_sc):
    i, j = pl.program_id(0), pl.program_id(1)

    @pl.when(j == 0)
    def _():
        u = cc_ref[...].astype(F32) * cx_ref[...].astype(F32)
        tm = u.shape[0]
        halo = ccp_ref.shape[0]
        prev_row = ccp_ref[halo - 1:halo, :].astype(F32) * cxp_ref[halo - 1:halo, :].astype(F32)
        next_row = ccn_ref[0:1, :].astype(F32) * cxn_ref[0:1, :].astype(F32)
        prev_row = jnp.where(i == 0, 0.0, prev_row)
        next_row = jnp.where(i == pl.num_programs(0) - 1, 0.0, next_row)
        r = lax.broadcasted_iota(jnp.int32, u.shape, 0)
        u_prev = jnp.where(r == 0, prev_row, pltpu.roll(u, 1, axis=0))
        u_next = jnp.where(r == tm - 1, next_row, pltpu.roll(u, tm - 1, axis=0))
        conv = cw_ref[0:1, :] * u_prev + cw_ref[1:2, :] * u + cw_ref[2:3, :] * u_next
        oc_sc[...] = (cb_ref[...].astype(F32) * conv).astype(BF16)

    acc = _sigmoid(g0_ref[...].astype(F32)) * jnp.dot(od_ref[...], wd_ref[...], preferred_element_type=F32)
    acc += _sigmoid(g1_ref[...].astype(F32)) * jnp.dot(on_ref[...], wn_ref[...], preferred_element_type=F32)
    acc += _sigmoid(g2_ref[...].astype(F32)) * jnp.dot(oc_sc[...], wc_ref[...], preferred_element_type=F32)
    o_ref[...] = acc.astype(o_ref.dtype)


def merge_up(o_diff, o_na, p, conv_w, w_up_diff, w_up_na, w_up_conv, layer, d):
    m, bw = o_diff.shape
    tm, tn = _tile(m, 512), math.gcd(_tile(d, 1024), bw)
    halo = 8
    n_halo = m // halo
    row_blk = lambda c: pl.BlockSpec((tm, bw), lambda i, j, c=c: (i, c))
    prev_blk = lambda c: pl.BlockSpec((halo, bw), lambda i, j, c=c: (jnp.maximum(i * (tm // halo) - 1, 0), c))
    next_blk = lambda c: pl.BlockSpec((halo, bw), lambda i, j, c=c: (jnp.minimum((i + 1) * (tm // halo), n_halo - 1), c))
    gate0 = 9 * bw // tn
    gate_blk = lambda b: pl.BlockSpec((tm, tn), lambda i, j, b=b: (i, gate0 + b * (d // tn) + j))
    w_blk = pl.BlockSpec((None, bw, tn), lambda i, j: (layer, 0, j))
    assert (9 * bw) % tn == 0
    return pl.pallas_call(
        _merge_up_kernel,
        grid=(m // tm, d // tn),
        in_specs=[row_blk(0), row_blk(0), row_blk(6), row_blk(7), row_blk(8),
                  prev_blk(7), prev_blk(8), next_blk(7), next_blk(8),
                  pl.BlockSpec((CONV_K, bw), lambda i, j: (0, 0)),
                  gate_blk(0), gate_blk(1), gate_blk(2), w_blk, w_blk, w_blk],
        out_specs=pl.BlockSpec((tm, tn), lambda i, j: (i, j)),
        out_shape=jax.ShapeDtypeStruct((m, d), BF16),
        scratch_shapes=[pltpu.VMEM((tm, bw), BF16)],
        compiler_params=_cparams("arbitrary", "arbitrary"),
        name="merge_up",
    )(o_diff, o_na, p, p, p, p, p, p, p, conv_w, p, p, p, w_up_diff, w_up_na, w_up_conv)


def kernel(x, c, ctx, c_ctx, w_mod_a, w_mod_b, b_mod, norm1_g, w_in, diff_qn_g, diff_kn_g, lambda_q1, lambda_k1, lambda_q2, lambda_k2, diff_subln_g, na_qn_g, na_kn_g, na_rpb, conv_w, w_up_diff, w_up_na, w_up_conv, w_o, norm2_g, router_w, router_b, w_gu, b_gu, w_down, b_down):
    batch, s, d = x.shape
    assert batch == 1 and ctx.shape[0] == 1
    depth = w_in.shape[0]
    bw = w_up_diff.shape[1]
    dv0 = 2 * bw // (2 * HEAD_DIM)
    nv0 = 5 * bw // HEAD_DIM
    rows = s // GRID_W

    cond = jnp.zeros((8, d), F32).at[0].set(c[0]).at[1].set(c_ctx)
    mod = modulation(cond, w_mod_a, w_mod_b, b_mod).reshape(depth, 8, N_MOD, d)
    rope = rope_tables(s)
    w_in, w_o = w_in.astype(BF16), w_o.astype(BF16)
    ups = tuple(w.astype(BF16) for w in (w_up_diff, w_up_na, w_up_conv))

    xl, xc = x[0], ctx[0]
    h_next = None
    for i in range(depth):
        last = i == depth - 1
        lam_init = 0.8 - 0.6 * math.exp(-0.3 * i)
        lam = (jnp.exp(jnp.sum(lambda_q1[i] * lambda_k1[i])) - jnp.exp(jnp.sum(lambda_q2[i] * lambda_k2[i]))
               + lam_init)
        ml, mc = mod[i, 0], mod[i, 1]
        gains = jnp.stack([diff_qn_g[i], diff_kn_g[i], na_qn_g[i], na_kn_g[i]])

        h_in = prenorm(xl, norm1_g[i], ml[1], ml[0]) if h_next is None else h_next
        p_l = project_in(h_in, w_in, i)
        p_c = project_in(prenorm(xc, norm1_g[i], mc[1], mc[0]), w_in, i)
        dq_l, dk_l, nq_l, nk_l = qk_prep(p_l, gains, rope, bw)
        dq_c, dk_c, nq_c, nk_c = qk_prep(p_c, gains, None, bw)

        diff_args = (dq_l, dk_l, p_l, dv0, lam, diff_subln_g[i], 1.0 - lam_init, (dk_c, p_c, dv0))
        o_diff_l = lax.cond(score_bound_log2(diff_qn_g[i], diff_kn_g[i]) <= SCORE_BOUND_LOG2,
                            lambda: diff_attention_bounded(*diff_args), lambda: diff_attention(*diff_args))
        na_args = (nq_l, nk_l, p_l, nv0, nk_c, p_c, nv0, na_bias_tables(na_rpb[i], rows))
        na_bound = score_bound_log2(na_qn_g[i], na_kn_g[i]) + jnp.max(jnp.abs(na_rpb[i])) * LOG2E
        o_na_l = lax.cond(na_bound <= SCORE_BOUND_LOG2, lambda: neighbourhood_attention(*na_args, True),
                          lambda: neighbourhood_attention(*na_args, False))
        xl = project_out(merge_up(o_diff_l, o_na_l, p_l, conv_w[i], *ups, i, d), w_o, i, xl, ml[2])
        if not last:
            o_diff_c = diff_attention(dq_c, dk_c, p_c, dv0, lam, diff_subln_g[i], 1.0 - lam_init)
            o_na_c = dense_attention(nq_c, nk_c, p_c, nv0)
            xc = project_out(merge_up(o_diff_c, o_na_c, p_c, conv_w[i], *ups, i, d), w_o, i, xc, mc[2])

        router = (router_w[i], router_b[i])
        h_l, eid, ew, rank, counts = prenorm_route(xl, norm2_g[i], ml[4], ml[3], router)
        y4 = moe_routed(h_l, eid, ew, rank, counts, w_gu, b_gu[i], w_down, b_down[i], i)
        next_norm = None if last else (norm1_g[i + 1], mod[i + 1, 0, 1], mod[i + 1, 0, 0])
        xl, h_next = moe_combine(xl, ml[5], ew, y4, next_norm)
        if not last:
            h_c, comb_c = prenorm(xc, norm2_g[i], mc[4], mc[3], router)
            xc = moe(h_c, comb_c, w_gu, b_gu[i], w_down, b_down[i], i, xc, mc[5])
    return xl[None]
```

```python
import functools
import math

import jax
import jax.numpy as jnp
from jax import lax
from jax.experimental import pallas as pl
from jax.experimental.pallas import tpu as pltpu

F32 = jnp.float32
BF16 = jnp.bfloat16
HIGHEST = lax.Precision.HIGHEST

GRID_W = 64
HEAD_DIM = 128
NA_WIN_R = 8
NA_WIN_C = 16
CONV_K = 3
N_BRANCH = 3
N_MOD = 6
TOP_K = 4
SWIGLU_LIMIT = 7.0
SWIGLU_ALPHA = 1.702
ROPE_BASE = 10000.0
EPS = 1e-6
LOG2E = math.log2(math.e)
Q_SCALE = HEAD_DIM ** -0.5 * LOG2E
MASKED = -1e30
SCORE_BOUND_LOG2 = 60.0

V7X_VMEM_LIMIT_BYTES = 56 * 1024 * 1024
NA_Q_ROWS = 8
NA_K_ROWS = NA_Q_ROWS + NA_WIN_R
NA_Q_SPLIT = 2


def _cparams(*sem):
    return pltpu.CompilerParams(dimension_semantics=sem, vmem_limit_bytes=V7X_VMEM_LIMIT_BYTES)


def _tile(n, target):
    if n <= target:
        return n
    t = target
    while n % t:
        t -= 8
    assert t > 0, (n, target)
    return t


def _sigmoid(x):
    return 1.0 / (1.0 + jnp.exp(-x))


def _mod_kernel(c_ref, wa_ref, wb_ref, b_ref, o_ref):
    c = c_ref[...]
    s = c * _sigmoid(c)
    m1 = jnp.dot(s, wa_ref[...], precision=HIGHEST, preferred_element_type=F32)
    o_ref[...] = jnp.dot(m1, wb_ref[...], precision=HIGHEST, preferred_element_type=F32) + b_ref[...]


def modulation(cond, w_a, w_b, b):
    depth, d, r = w_a.shape
    width = w_b.shape[-1]
    rows = cond.shape[0]
    tn = _tile(width, 4096)
    return pl.pallas_call(
        _mod_kernel,
        grid=(depth, width // tn),
        in_specs=[
            pl.BlockSpec((rows, d), lambda l, j: (0, 0)),
            pl.BlockSpec((None, d, r), lambda l, j: (l, 0, 0)),
            pl.BlockSpec((None, r, tn), lambda l, j: (l, 0, j)),
            pl.BlockSpec((None, 1, tn), lambda l, j: (l, 0, j)),
        ],
        out_specs=pl.BlockSpec((None, rows, tn), lambda l, j: (l, 0, j)),
        out_shape=jax.ShapeDtypeStruct((depth, rows, width), F32),
        compiler_params=_cparams("arbitrary", "arbitrary"),
        name="modulation",
    )(cond, w_a, w_b, b.reshape(depth, 1, width))


def _modulated_norm(x_ref, g_ref, sc_ref, sh_ref):
    x = x_ref[...]
    y = x * lax.rsqrt(jnp.mean(x * x, axis=-1, keepdims=True) + EPS)
    return (y * g_ref[...]) * (1.0 + sc_ref[...]) + sh_ref[...]


def _prenorm_kernel(x_ref, g_ref, sc_ref, sh_ref, o_ref):
    o_ref[...] = _modulated_norm(x_ref, g_ref, sc_ref, sh_ref).astype(o_ref.dtype)


def _split_bf16(x):
    hi = x.astype(BF16)
    return hi, (x - hi.astype(F32)).astype(BF16)


def _route(h, rwh_ref, rwl_ref, rb_ref):
    h_hi, h_lo = _split_bf16(h)
    w_hi = rwh_ref[...]
    logits = (jnp.dot(h_hi, w_hi, preferred_element_type=F32) + jnp.dot(h_lo, w_hi, preferred_element_type=F32)
              + jnp.dot(h_hi, rwl_ref[...], preferred_element_type=F32) + rb_ref[...])
    n_exp = logits.shape[-1]
    lane = lax.broadcasted_iota(jnp.int32, logits.shape, 1).astype(F32)
    sels, ids, vals = [], [], []
    for _ in range(TOP_K):
        m = jnp.max(logits, axis=-1, keepdims=True)
        first = jnp.min(jnp.where(logits == m, lane, float(n_exp)), axis=-1, keepdims=True)
        sel = lane == first
        sels.append(sel)
        ids.append(first)
        vals.append(m)
        logits = jnp.where(sel, -jnp.inf, logits)
    es = [jnp.exp(v - vals[0]) for v in vals]
    inv = 1.0 / functools.reduce(lambda a, b: a + b, es)
    return sels, ids, [e * inv for e in es]


def _prenorm_router_kernel(x_ref, g_ref, sc_ref, sh_ref, rwh_ref, rwl_ref, rb_ref, o_ref, comb_ref):
    h = _modulated_norm(x_ref, g_ref, sc_ref, sh_ref)
    o_ref[...] = h.astype(o_ref.dtype)
    sels, _, wts = _route(h, rwh_ref, rwl_ref, rb_ref)
    comb = jnp.zeros(comb_ref.shape, F32)
    for sel, w in zip(sels, wts):
        comb = comb + jnp.where(sel, w, 0.0)
    comb_ref[...] = comb


def _columns(cols, width):
    lane = lax.broadcasted_iota(jnp.int32, (cols[0].shape[0], width), 1)
    out = jnp.zeros(lane.shape, cols[0].dtype)
    for k, c in enumerate(cols):
        out = jnp.where(lane == k, c, out)
    return out


def _prenorm_route_kernel(x_ref, g_ref, sc_ref, sh_ref, rwh_ref, rwl_ref, rb_ref,
                          o_ref, eid_ref, ew_ref, rank_ref, cnt_ref, carry_sc):
    @pl.when(pl.program_id(0) == 0)
    def _():
        carry_sc[...] = jnp.zeros(carry_sc.shape, F32)

    h = _modulated_norm(x_ref, g_ref, sc_ref, sh_ref)
    o_ref[...] = h.astype(o_ref.dtype)
    sels, ids, wts = _route(h, rwh_ref, rwl_ref, rb_ref)
    tm = h.shape[0]
    chosen = functools.reduce(lambda a, b: a + b, [jnp.where(s, 1.0, 0.0) for s in sels])
    r = lax.broadcasted_iota(jnp.int32, (tm, tm), 0)
    c = lax.broadcasted_iota(jnp.int32, (tm, tm), 1)
    earlier = jnp.where(c < r, 1.0, 0.0).astype(BF16)
    before = jnp.dot(earlier, chosen.astype(BF16), preferred_element_type=F32) + carry_sc[...]
    ranks = [jnp.sum(jnp.where(s, before, 0.0), axis=-1, keepdims=True) for s in sels]
    carry_sc[...] += jnp.sum(chosen, axis=0, keepdims=True)
    eid_ref[...] = _columns(ids, TOP_K).astype(jnp.int32)
    ew_ref[...] = _columns(wts, TOP_K)
    rank_ref[...] = _columns(ranks, TOP_K).astype(jnp.int32)
    cnt_ref[...] = carry_sc[...].astype(jnp.int32)


def prenorm(x, g, scale, shift, router=None):
    m, d = x.shape
    tm = _tile(m, 256)
    row = lambda i: (i, 0)
    fixed = lambda i: (0, 0)
    vec = pl.BlockSpec((1, d), fixed)
    in_specs = [pl.BlockSpec((tm, d), row), vec, vec, vec]
    args = [x, g.reshape(1, d), scale.reshape(1, d), shift.reshape(1, d)]
    if router is None:
        return pl.pallas_call(
            _prenorm_kernel, grid=(m // tm,), in_specs=in_specs,
            out_specs=pl.BlockSpec((tm, d), row),
            out_shape=jax.ShapeDtypeStruct((m, d), BF16),
            compiler_params=_cparams("arbitrary"), name="prenorm",
        )(*args)
    rw, rb = router
    n_exp = rw.shape[-1]
    in_specs += [pl.BlockSpec((d, n_exp), fixed)] * 2 + [pl.BlockSpec((1, n_exp), fixed)]
    args += [*_split_bf16(rw), rb.reshape(1, n_exp)]
    return pl.pallas_call(
        _prenorm_router_kernel, grid=(m // tm,), in_specs=in_specs,
        out_specs=[pl.BlockSpec((tm, d), row), pl.BlockSpec((tm, n_exp), row)],
        out_shape=[jax.ShapeDtypeStruct((m, d), BF16), jax.ShapeDtypeStruct((m, n_exp), F32)],
        compiler_params=_cparams("arbitrary"), name="prenorm_router",
    )(*args)


def prenorm_route(x, g, scale, shift, router):
    m, d = x.shape
    tm = _tile(m, 256)
    rw, rb = router
    n_exp = rw.shape[-1]
    row = lambda i: (i, 0)
    fixed = lambda i: (0, 0)
    vec = pl.BlockSpec((1, d), fixed)
    per_tok = pl.BlockSpec((tm, TOP_K), row)
    return pl.pallas_call(
        _prenorm_route_kernel, grid=(m // tm,),
        in_specs=[pl.BlockSpec((tm, d), row), vec, vec, vec,
                  pl.BlockSpec((d, n_exp), fixed), pl.BlockSpec((d, n_exp), fixed),
                  pl.BlockSpec((1, n_exp), fixed)],
        out_specs=[pl.BlockSpec((tm, d), row), per_tok, per_tok, per_tok, pl.BlockSpec((1, n_exp), fixed)],
        out_shape=[jax.ShapeDtypeStruct((m, d), BF16), jax.ShapeDtypeStruct((m, TOP_K), jnp.int32),
                   jax.ShapeDtypeStruct((m, TOP_K), F32), jax.ShapeDtypeStruct((m, TOP_K), jnp.int32),
                   jax.ShapeDtypeStruct((1, n_exp), jnp.int32)],
        scratch_shapes=[pltpu.VMEM((1, n_exp), F32)],
        compiler_params=_cparams("arbitrary"), name="prenorm_route",
    )(x, g.reshape(1, d), scale.reshape(1, d), shift.reshape(1, d), *_split_bf16(rw), rb.reshape(1, n_exp))


EXPERT_TILE = 256


def _expert_mlp_kernel(te_ref, nt_ref, x_ref, wgu_ref, bgu_ref, wd_ref, bd_ref, o_ref, wgu_sc, wd_sc):
    j = pl.program_id(0)
    used = j < nt_ref[0]

    @pl.when(jnp.logical_not(used))
    def _():
        o_ref[...] = jnp.zeros(o_ref.shape, o_ref.dtype)

    @pl.when(jnp.logical_and(used, jnp.logical_or(j == 0, te_ref[j] != te_ref[jnp.maximum(j - 1, 0)])))
    def _():
        wgu_sc[...] = wgu_ref[...].astype(BF16)
        wd_sc[...] = wd_ref[...].astype(BF16)

    @pl.when(used)
    def _():
        d_exp = wd_ref.shape[0]
        gu = jnp.dot(x_ref[...], wgu_sc[...], preferred_element_type=F32) + bgu_ref[...]
        gate = jnp.minimum(gu[:, :d_exp], SWIGLU_LIMIT)
        lin = jnp.clip(gu[:, d_exp:], -SWIGLU_LIMIT, SWIGLU_LIMIT)
        act = gate * _sigmoid(SWIGLU_ALPHA * gate) * (lin + 1.0)
        y = jnp.dot(act.astype(BF16), wd_sc[...], preferred_element_type=F32) + bd_ref[...]
        o_ref[...] = y.astype(o_ref.dtype)


def expert_mlp(xg, tile_expert, n_tiles_used, w_gu, b_gu, w_down, b_down, layer):
    p_rows, d = xg.shape
    _, n_exp, _, two_de = w_gu.shape
    d_exp = two_de // 2
    t = EXPERT_TILE
    grid_spec = pltpu.PrefetchScalarGridSpec(
        num_scalar_prefetch=2,
        grid=(p_rows // t,),
        in_specs=[
            pl.BlockSpec((t, d), lambda j, te, nt: (j, 0)),
            pl.BlockSpec((None, None, d, two_de), lambda j, te, nt: (layer, te[j], 0, 0)),
            pl.BlockSpec((None, 1, two_de), lambda j, te, nt: (te[j], 0, 0)),
            pl.BlockSpec((None, None, d_exp, d), lambda j, te, nt: (layer, te[j], 0, 0)),
            pl.BlockSpec((None, 1, d), lambda j, te, nt: (te[j], 0, 0)),
        ],
        out_specs=pl.BlockSpec((t, d), lambda j, te, nt: (j, 0)),
        scratch_shapes=[pltpu.VMEM((d, two_de), BF16), pltpu.VMEM((d_exp, d), BF16)],
    )
    return pl.pallas_call(
        _expert_mlp_kernel, grid_spec=grid_spec,
        out_shape=jax.ShapeDtypeStruct((p_rows, d), BF16),
        compiler_params=_cparams("arbitrary"), name="expert_mlp",
    )(tile_expert, n_tiles_used, xg, w_gu, b_gu.reshape(n_exp, 1, two_de), w_down, b_down.reshape(n_exp, 1, d))


def _combine_kernel(x_ref, gate_ref, ew_ref, *rest, next_norm):
    y_refs, rest = rest[:TOP_K], rest[TOP_K:]
    ew = ew_ref[...]
    acc = ew[:, 0:1] * y_refs[0][...].astype(F32)
    for k in range(1, TOP_K):
        acc = acc + ew[:, k:k + 1] * y_refs[k][...].astype(F32)
    if next_norm:
        g_ref, sc_ref, sh_ref, o_ref, h_ref = rest
        o_ref[...] = x_ref[...] + gate_ref[...] * acc
        h_ref[...] = _modulated_norm(o_ref, g_ref, sc_ref, sh_ref).astype(h_ref.dtype)
    else:
        (o_ref,) = rest
        o_ref[...] = x_ref[...] + gate_ref[...] * acc


def moe_combine(x, gate, ew, y4, next_norm=None):
    m, d = x.shape
    tm = _tile(m, 128)
    rows = pl.BlockSpec((tm, d), lambda i: (i, 0))
    vec = pl.BlockSpec((1, d), lambda i: (0, 0))
    in_specs = ([rows, vec, pl.BlockSpec((tm, TOP_K), lambda i: (i, 0))]
                + [pl.BlockSpec((None, tm, d), lambda i, k=k: (k, i, 0)) for k in range(TOP_K)])
    args = [x, gate.reshape(1, d), ew] + [y4] * TOP_K
    out_specs, out_shape = [rows], [jax.ShapeDtypeStruct((m, d), F32)]
    if next_norm is not None:
        in_specs += [vec] * 3
        args += [v.reshape(1, d) for v in next_norm]
        out_specs.append(rows)
        out_shape.append(jax.ShapeDtypeStruct((m, d), BF16))
    out = pl.pallas_call(
        functools.partial(_combine_kernel, next_norm=next_norm is not None), grid=(m // tm,),
        in_specs=in_specs, out_specs=out_specs, out_shape=out_shape,
        compiler_params=_cparams("arbitrary"), name="moe_combine",
    )(*args)
    return out if next_norm is not None else (out[0], None)


def moe_routed(h, eid, ew, rank, counts, w_gu, b_gu, w_down, b_down, layer):
    m, d = h.shape
    n_exp = counts.shape[-1]
    t = EXPERT_TILE
    p_rows = m * TOP_K + n_exp * t
    counts = counts.reshape(n_exp)
    tiles_per = (counts + t - 1) // t
    tile_end = jnp.cumsum(tiles_per)
    start = (tile_end - tiles_per) * t
    pos = start[eid] + rank
    tile_id = jnp.arange(p_rows // t, dtype=jnp.int32)
    tile_expert = jnp.minimum(jnp.sum((tile_end[None, :] <= tile_id[:, None]).astype(jnp.int32), axis=1),
                              n_exp - 1)
    n_used = tile_end[-1:].astype(jnp.int32)
    pos_t = pos.T.reshape(-1)
    src = (jnp.arange(p_rows, dtype=jnp.int32) % m).at[pos_t].set(
        jnp.tile(jnp.arange(m, dtype=jnp.int32), TOP_K), unique_indices=True, mode="promise_in_bounds")
    xg = h.at[src].get(mode="promise_in_bounds")
    yg = expert_mlp(xg, tile_expert, n_used, w_gu, b_gu, w_down, b_down, layer)
    return yg.at[pos_t].get(mode="promise_in_bounds", unique_indices=True).reshape(TOP_K, m, d)


def _mm_kernel(*refs, epilogue):
    a_ref, w_ref = refs[0], refs[1]
    extras, o_ref = refs[2:-1], refs[-1]
    acc = jnp.dot(a_ref[...], w_ref[...].astype(BF16), preferred_element_type=F32)
    o_ref[...] = epilogue(acc, *extras).astype(o_ref.dtype)


def matmul(a, w, w_spec, tm, tn, n_col_tiles, extras, epilogue, out_spec, out_shape, name):
    m, k = a.shape
    return pl.pallas_call(
        functools.partial(_mm_kernel, epilogue=epilogue),
        grid=(m // tm, n_col_tiles),
        in_specs=[pl.BlockSpec((tm, k), lambda i, j: (i, 0)), w_spec] + [s for _, s in extras],
        out_specs=out_spec,
        out_shape=out_shape,
        compiler_params=_cparams("arbitrary", "arbitrary"),
        name=name,
    )(a, w, *[x for x, _ in extras])


def _cast_epilogue(acc):
    return acc


def _residual_epilogue(acc, x_ref, gate_ref):
    return x_ref[...] + gate_ref[...] * acc


def _swiglu_epilogue(acc, b_ref, comb_ref):
    d_exp = acc.shape[-1] // 2
    gu = acc + b_ref[...]
    gate = jnp.minimum(gu[:, :d_exp], SWIGLU_LIMIT)
    lin = jnp.clip(gu[:, d_exp:], -SWIGLU_LIMIT, SWIGLU_LIMIT)
    act = gate * _sigmoid(SWIGLU_ALPHA * gate) * (lin + 1.0)
    comb = comb_ref[...]
    lane = lax.broadcasted_iota(jnp.int32, comb.shape, 1)
    cw = jnp.sum(jnp.where(lane == pl.program_id(1), comb, 0.0), axis=-1, keepdims=True)
    return act * cw


def _moe_down_epilogue(acc, comb_ref, bd_ref, x_ref, gate_ref):
    bias = jnp.dot(comb_ref[...].astype(BF16), bd_ref[...].astype(BF16), preferred_element_type=F32)
    return x_ref[...] + gate_ref[...] * (acc + bias)


def project_in(h, w_in, layer):
    m, d = h.shape
    width = w_in.shape[-1]
    tm, tn = _tile(m, 1024), _tile(width, 1024)
    return matmul(h, w_in, pl.BlockSpec((None, d, tn), lambda i, j: (layer, 0, j)), tm, tn, width // tn, [],
                  _cast_epilogue, pl.BlockSpec((tm, tn), lambda i, j: (i, j)),
                  jax.ShapeDtypeStruct((m, width), BF16), "project_in")


def project_out(merged, w_o, layer, x, gate):
    m, d = x.shape
    tm, tn = _tile(m, 1024), _tile(d, 512)
    tile = pl.BlockSpec((tm, tn), lambda i, j: (i, j))
    return matmul(merged, w_o, pl.BlockSpec((None, merged.shape[1], tn), lambda i, j: (layer, 0, j)),
                  tm, tn, d // tn,
                  [(x, tile), (gate.reshape(1, d), pl.BlockSpec((1, tn), lambda i, j: (0, j)))],
                  _residual_epilogue, tile, jax.ShapeDtypeStruct((m, d), F32), "project_out")


def moe(h, comb, w_gu, b_gu, w_down, b_down, layer, x, gate):
    m, d = h.shape
    depth, n_exp, _, two_de = w_gu.shape
    d_exp = two_de // 2
    tm = _tile(m, 1024)
    act = matmul(
        h, w_gu, pl.BlockSpec((None, None, d, two_de), lambda i, j: (layer, j, 0, 0)), tm, two_de, n_exp,
        [(b_gu.reshape(n_exp, 1, two_de), pl.BlockSpec((None, 1, two_de), lambda i, j: (j, 0, 0))),
         (comb, pl.BlockSpec((tm, n_exp), lambda i, j: (i, 0)))],
        _swiglu_epilogue, pl.BlockSpec((tm, d_exp), lambda i, j: (i, j)),
        jax.ShapeDtypeStruct((m, n_exp * d_exp), BF16), "moe_up")
    tm, tn = _tile(m, 512), _tile(d, 256)
    tile = pl.BlockSpec((tm, tn), lambda i, j: (i, j))
    return matmul(
        act, w_down.reshape(depth, n_exp * d_exp, d),
        pl.BlockSpec((None, n_exp * d_exp, tn), lambda i, j: (layer, 0, j)), tm, tn, d // tn,
        [(comb, pl.BlockSpec((tm, n_exp), lambda i, j: (i, 0))),
         (b_down, pl.BlockSpec((n_exp, tn), lambda i, j: (0, j))),
         (x, tile), (gate.reshape(1, d), pl.BlockSpec((1, tn), lambda i, j: (0, j)))],
        _moe_down_epilogue, tile, jax.ShapeDtypeStruct((m, d), F32), "moe_down")


def _chunk_norm(x, g):
    return x * lax.rsqrt(jnp.mean(x * x, axis=-1, keepdims=True) + EPS) * g


def _qk_prep_kernel(*refs, rope):
    if rope:
        dq_ref, dk_ref, nq_ref, nk_ref, g_ref, cos_ref, sin_ref = refs[:7]
        cos, sin = cos_ref[...], sin_ref[...]
        lane = lax.broadcasted_iota(jnp.int32, cos.shape, 1)
        lower = (lane & (HEAD_DIM // 4)) == 0
    else:
        dq_ref, dk_ref, nq_ref, nk_ref, g_ref = refs[:5]
    outs = refs[-4:]
    ins = (dq_ref, dk_ref, nq_ref, nk_ref)
    for idx in range(4):
        g = g_ref[idx:idx + 1, :]
        rot = rope and idx < 2
        scale = Q_SCALE if idx in (0, 2) else 1.0
        for c in range(ins[idx].shape[-1] // HEAD_DIM):
            sl = slice(c * HEAD_DIM, (c + 1) * HEAD_DIM)
            y = _chunk_norm(ins[idx][:, sl].astype(F32), g)
            if rot:
                swapped = jnp.where(lower, pltpu.roll(y, HEAD_DIM - HEAD_DIM // 4, axis=1),
                                    pltpu.roll(y, HEAD_DIM // 4, axis=1))
                y = y * cos + swapped * sin
            outs[idx][:, sl] = (y * scale).astype(BF16)


def qk_prep(p, gains, rope, bw):
    m = p.shape[0]
    tm = _tile(m, 512)
    col = lambda c: pl.BlockSpec((tm, bw), lambda i, c=c: (i, c))
    in_specs = [col(0), col(1), col(3), col(4), pl.BlockSpec((4, HEAD_DIM), lambda i: (0, 0))]
    args = [p, p, p, p, gains]
    if rope is not None:
        in_specs += [pl.BlockSpec((tm, HEAD_DIM), lambda i: (i, 0))] * 2
        args += list(rope)
    return pl.pallas_call(
        functools.partial(_qk_prep_kernel, rope=rope is not None),
        grid=(m // tm,), in_specs=in_specs,
        out_specs=[pl.BlockSpec((tm, bw), lambda i: (i, 0))] * 4,
        out_shape=[jax.ShapeDtypeStruct((m, bw), BF16)] * 4,
        compiler_params=_cparams("arbitrary"), name="qk_prep",
    )(*args)


def rope_tables(n):
    t = jnp.arange(n, dtype=jnp.int32)
    row = (t // GRID_W).astype(F32)
    col = (t % GRID_W).astype(F32)
    n_freq = HEAD_DIM // 4
    inv = ROPE_BASE ** (-jnp.arange(n_freq, dtype=F32) / n_freq)
    ar, ac = row[:, None] * inv, col[:, None] * inv
    ang = jnp.concatenate([ar, ar, ac, ac], axis=-1)
    lane = jnp.arange(HEAD_DIM)
    sign = jnp.where((lane & n_freq) == 0, -1.0, 1.0).astype(F32)
    return jnp.cos(ang), jnp.sin(ang) * sign


def _diff_attn_kernel(lam_ref, q_ref, k_ref, v_ref, *rest, n_main, has_extra, out_scale):
    if has_extra:
        kx_ref, vx_ref, g_ref, o_ref, m_sc, l_sc, acc_sc = rest
    else:
        g_ref, o_ref, m_sc, l_sc, acc_sc = rest
    kj = pl.program_id(2)
    n_steps = n_main + (1 if has_extra else 0)

    @pl.when(kj == 0)
    def _():
        m_sc[...] = jnp.full(m_sc.shape, -jnp.inf, F32)
        l_sc[...] = jnp.zeros(l_sc.shape, F32)
        acc_sc[...] = jnp.zeros(acc_sc.shape, F32)

    def update(kr, vr):
        v = vr[...]
        for hf in range(2):
            sl = slice(hf * HEAD_DIM, (hf + 1) * HEAD_DIM)
            s = lax.dot_general(q_ref[:, sl], kr[:, sl], (((1,), (1,)), ((), ())),
                                preferred_element_type=F32)
            m_prev = m_sc[hf]
            m_new = jnp.maximum(m_prev, jnp.max(s, axis=-1, keepdims=True))
            alpha = jnp.exp2(m_prev - m_new)
            p = jnp.exp2(s - m_new)
            l_sc[hf] = alpha * l_sc[hf] + jnp.sum(p, axis=-1, keepdims=True)
            acc_sc[hf] = alpha * acc_sc[hf] + jnp.dot(p.astype(BF16), v, preferred_element_type=F32)
            m_sc[hf] = m_new

    if has_extra:
        pl.when(kj < n_main)(lambda: update(k_ref, v_ref))
        pl.when(kj == n_main)(lambda: update(kx_ref, vx_ref))
    else:
        update(k_ref, v_ref)

    @pl.when(kj == n_steps - 1)
    def _():
        _diff_finalize(lam_ref, g_ref, o_ref, acc_sc[0], acc_sc[1], l_sc[0], l_sc[1], out_scale)


def _diff_finalize(lam_ref, g_ref, o_ref, acc0, acc1, l0, l1, out_scale):
    lam = lam_ref[0, 0]
    o = acc0 * (1.0 / l0) - lam * (acc1 * (1.0 / l1))
    y = o * lax.rsqrt(jnp.mean(o * o, axis=-1, keepdims=True) + EPS)
    o_ref[...] = (y * g_ref[...] * out_scale).astype(o_ref.dtype)


def _diff_attn_bounded_kernel(lam_ref, q_ref, k_ref, v_ref, kx_ref, vx_ref, g_ref, o_ref, l_sc, acc_sc,
                              *, tk, n_main, out_scale):
    l_sc[...] = jnp.zeros(l_sc.shape, F32)
    acc_sc[...] = jnp.zeros(acc_sc.shape, F32)

    def accumulate(k, v):
        for hf in range(2):
            sl = slice(hf * HEAD_DIM, (hf + 1) * HEAD_DIM)
            s = lax.dot_general(q_ref[:, sl], k[:, sl], (((1,), (1,)), ((), ())), preferred_element_type=F32)
            p = jnp.exp2(s)
            part = p[:, 0:HEAD_DIM]
            for c in range(1, p.shape[1] // HEAD_DIM):
                part = part + p[:, c * HEAD_DIM:(c + 1) * HEAD_DIM]
            l_sc[hf] += part
            acc_sc[hf] += jnp.dot(p.astype(BF16), v, preferred_element_type=F32)

    def body(j, carry):
        off = pl.multiple_of(j * tk, tk)
        accumulate(k_ref[pl.ds(off, tk), :], v_ref[pl.ds(off, tk), :])
        return carry

    lax.fori_loop(0, n_main, body, 0, unroll=math.gcd(n_main, 4))
    accumulate(kx_ref[...], vx_ref[...])
    _diff_finalize(lam_ref, g_ref, o_ref, acc_sc[0], acc_sc[1],
                   jnp.sum(l_sc[0], axis=-1, keepdims=True), jnp.sum(l_sc[1], axis=-1, keepdims=True), out_scale)


def diff_attention_bounded(q, k, v_src, v_col0, lam, sub_g, out_scale, extra):
    mq, bw = q.shape
    mk = k.shape[0]
    hw = 2 * HEAD_DIM
    tq, tk = _tile(mq, 1024), _tile(mk, 1024)
    kx, vx_src, vx_col0 = extra
    mx = kx.shape[0]
    once = pl.Buffered(1)
    return pl.pallas_call(
        functools.partial(_diff_attn_bounded_kernel, tk=tk, n_main=mk // tk, out_scale=out_scale),
        grid=(bw // hw, mq // tq),
        in_specs=[
            pl.BlockSpec(memory_space=pltpu.SMEM),
            pl.BlockSpec((tq, hw), lambda h, i: (i, h)),
            pl.BlockSpec((mk, hw), lambda h, i: (0, h), pipeline_mode=once),
            pl.BlockSpec((mk, hw), lambda h, i: (0, v_col0 + h), pipeline_mode=once),
            pl.BlockSpec((mx, hw), lambda h, i: (0, h)),
            pl.BlockSpec((mx, hw), lambda h, i: (0, vx_col0 + h)),
            pl.BlockSpec((1, hw), lambda h, i: (0, 0)),
        ],
        out_specs=pl.BlockSpec((tq, hw), lambda h, i: (i, h)),
        out_shape=jax.ShapeDtypeStruct((mq, bw), BF16),
        scratch_shapes=[pltpu.VMEM((2, tq, HEAD_DIM), F32), pltpu.VMEM((2, tq, hw), F32)],
        compiler_params=_cparams("arbitrary", "arbitrary"),
        name="diff_attention_bounded",
    )(lam.reshape(1, 1).astype(F32), q, k, v_src, kx, vx_src, sub_g.reshape(1, hw))


def score_bound_log2(g_q, g_k):
    return Q_SCALE * HEAD_DIM * jnp.max(jnp.abs(g_q)) * jnp.max(jnp.abs(g_k)) * 1.01


def diff_attention(q, k, v_src, v_col0, lam, sub_g, out_scale, extra=None):
    mq, bw = q.shape
    mk = k.shape[0]
    hw = 2 * HEAD_DIM
    heads = bw // hw
    tq, tk = _tile(mq, 1024), _tile(mk, 1024)
    n_main = mk // tk
    last = n_main - 1
    kmap = lambda h, i, j: (jnp.minimum(j, last), h)
    in_specs = [
        pl.BlockSpec(memory_space=pltpu.SMEM),
        pl.BlockSpec((tq, hw), lambda h, i, j: (i, h)),
        pl.BlockSpec((tk, hw), kmap),
        pl.BlockSpec((tk, hw), lambda h, i, j: (jnp.minimum(j, last), v_col0 + h)),
    ]
    args = [lam.reshape(1, 1).astype(F32), q, k, v_src]
    if extra is not None:
        kx, vx_src, vx_col0 = extra
        mx = kx.shape[0]
        in_specs += [pl.BlockSpec((mx, hw), lambda h, i, j: (0, h)),
                     pl.BlockSpec((mx, hw), lambda h, i, j: (0, vx_col0 + h))]
        args += [kx, vx_src]
    in_specs.append(pl.BlockSpec((1, hw), lambda h, i, j: (0, 0)))
    args.append(sub_g.reshape(1, hw))
    return pl.pallas_call(
        functools.partial(_diff_attn_kernel, n_main=n_main, has_extra=extra is not None, out_scale=out_scale),
        grid=(heads, mq // tq, n_main + (extra is not None)),
        in_specs=in_specs,
        out_specs=pl.BlockSpec((tq, hw), lambda h, i, j: (i, h)),
        out_shape=jax.ShapeDtypeStruct((mq, bw), BF16),
        scratch_shapes=[pltpu.VMEM((2, tq, 1), F32), pltpu.VMEM((2, tq, 1), F32), pltpu.VMEM((2, tq, hw), F32)],
        compiler_params=_cparams("arbitrary", "arbitrary", "arbitrary"),
        name="diff_attention",
    )(*args)


def _na_kernel(q_ref, k_ref, v_ref, kc_ref, vc_ref, bias_ref, o_ref, *, rows, bounded):
    b = pl.program_id(1)
    start_row = jnp.clip(b * NA_Q_ROWS - NA_WIN_R // 2, 0, rows - NA_K_ROWS)
    start = pl.multiple_of(start_row * GRID_W, GRID_W * (NA_WIN_R // 2))
    n_keys = NA_K_ROWS * GRID_W
    k = k_ref[pl.ds(start, n_keys), :]
    v = v_ref[pl.ds(start, n_keys), :]
    nt = (((1,), (1,)), ((), ()))
    part = q_ref.shape[0] // NA_Q_SPLIT
    for c in range(NA_Q_SPLIT):
        rs = slice(c * part, (c + 1) * part)
        q = q_ref[rs, :]
        s_loc = lax.dot_general(q, k, nt, preferred_element_type=F32) + bias_ref[rs, :]
        s_ctx = lax.dot_general(q, kc_ref[...], nt, preferred_element_type=F32)
        if bounded:
            p_loc, p_ctx = jnp.exp2(s_loc), jnp.exp2(s_ctx)
        else:
            m = jnp.maximum(jnp.max(s_loc, axis=-1, keepdims=True), jnp.max(s_ctx, axis=-1, keepdims=True))
            p_loc, p_ctx = jnp.exp2(s_loc - m), jnp.exp2(s_ctx - m)
        denom = jnp.sum(p_loc, axis=-1, keepdims=True) + jnp.sum(p_ctx, axis=-1, keepdims=True)
        o = (jnp.dot(p_loc.astype(BF16), v, preferred_element_type=F32)
             + jnp.dot(p_ctx.astype(BF16), vc_ref[...], preferred_element_type=F32))
        o_ref[rs, :] = (o * (1.0 / denom)).astype(o_ref.dtype)


def na_bias_tables(rpb, rows):
    n_blocks = rows // NA_Q_ROWS
    wr = min(NA_WIN_R, rows)
    w = jnp.arange(GRID_W)
    col_start = jnp.clip(w - NA_WIN_C // 2, 0, GRID_W - NA_WIN_C)
    kc = jnp.arange(GRID_W)
    valid_c = (kc[None, :] >= col_start[:, None]) & (kc[None, :] < col_start[:, None] + NA_WIN_C)
    rel_c = jnp.clip(kc[None, :] - w[:, None] + (NA_WIN_C - 1), 0, 2 * NA_WIN_C - 2)
    heads, n_rel = rpb.shape[0], rpb.shape[1]
    blocks = jnp.where(valid_c[None, None], rpb[:, :, rel_c] * LOG2E, MASKED).astype(F32)
    masked = jnp.full((heads, n_rel, GRID_W, GRID_W), MASKED, F32)
    shifted = jnp.concatenate([blocks[:, 1:], masked[:, :1]], axis=1)
    pairs = jnp.concatenate([jnp.concatenate([blocks, shifted], axis=-1),
                             jnp.concatenate([masked, blocks], axis=-1),
                             jnp.concatenate([blocks, masked], axis=-1),
                             jnp.concatenate([masked[:, :1], masked[:, :1]], axis=-1)], axis=1)
    plan = []
    for blk in (0, 1, n_blocks - 1):
        start = min(max(blk * NA_Q_ROWS - NA_WIN_R // 2, 0), rows - NA_K_ROWS)
        per_row = []
        for r_i in range(NA_Q_ROWS):
            r = blk * NA_Q_ROWS + r_i
            row_start = min(max(r - NA_WIN_R // 2, 0), rows - wr)
            rel = lambda kr: kr - r + (NA_WIN_R - 1)
            inside = lambda kr: row_start <= kr < row_start + wr
            per_pair = []
            for j in range(NA_K_ROWS // 2):
                left, right = start + 2 * j, start + 2 * j + 1
                if inside(left) and inside(right):
                    per_pair.append(rel(left))
                elif inside(right):
                    per_pair.append(n_rel + rel(right))
                elif inside(left):
                    per_pair.append(2 * n_rel + rel(left))
                else:
                    per_pair.append(3 * n_rel)
            per_row.append(per_pair)
        plan.append(per_row)
    assert 2 * GRID_W == HEAD_DIM
    tq, tkeys = NA_Q_ROWS * GRID_W, NA_K_ROWS * GRID_W
    return pl.pallas_call(
        functools.partial(_na_table_kernel, plan=plan),
        grid=(heads,),
        in_specs=[pl.BlockSpec((None, 3 * n_rel + 1, GRID_W, 2 * GRID_W), lambda h: (h, 0, 0, 0))],
        out_specs=pl.BlockSpec((None, 3, tq, tkeys), lambda h: (h, 0, 0, 0)),
        out_shape=jax.ShapeDtypeStruct((heads, 3, tq, tkeys), F32),
        compiler_params=_cparams("arbitrary"), name="na_bias_tables",
    )(pairs)


def _na_table_kernel(pairs_ref, o_ref, *, plan):
    for v, per_row in enumerate(plan):
        for r_i, per_pair in enumerate(per_row):
            for j, idx in enumerate(per_pair):
                o_ref[v, r_i * GRID_W:(r_i + 1) * GRID_W, j * 2 * GRID_W:(j + 1) * 2 * GRID_W] = pairs_ref[idx]


def neighbourhood_attention(q, k, v_src, v_col0, kc, vc_src, vc_col0, bias, bounded):
    s, bw = q.shape
    heads = bw // HEAD_DIM
    rows = s // GRID_W
    n_blocks = rows // NA_Q_ROWS
    assert n_blocks >= 3 and rows % NA_Q_ROWS == 0
    tq = NA_Q_ROWS * GRID_W
    lc = kc.shape[0]
    variant = lambda b: jnp.where(b == 0, 0, jnp.where(b == n_blocks - 1, 2, 1))
    return pl.pallas_call(
        functools.partial(_na_kernel, rows=rows, bounded=bounded),
        grid=(heads, n_blocks),
        in_specs=[
            pl.BlockSpec((tq, HEAD_DIM), lambda h, b: (b, h)),
            pl.BlockSpec((s, HEAD_DIM), lambda h, b: (0, h)),
            pl.BlockSpec((s, HEAD_DIM), lambda h, b: (0, v_col0 + h)),
            pl.BlockSpec((lc, HEAD_DIM), lambda h, b: (0, h)),
            pl.BlockSpec((lc, HEAD_DIM), lambda h, b: (0, vc_col0 + h)),
            pl.BlockSpec((None, None, tq, NA_K_ROWS * GRID_W), lambda h, b: (h, variant(b), 0, 0)),
        ],
        out_specs=pl.BlockSpec((tq, HEAD_DIM), lambda h, b: (b, h)),
        out_shape=jax.ShapeDtypeStruct((s, bw), BF16),
        compiler_params=_cparams("arbitrary", "arbitrary"),
        name="neighbourhood_attention",
    )(q, k, v_src, kc, vc_src, bias)


def _dense_attn_kernel(q_ref, k_ref, v_ref, o_ref):
    s = lax.dot_general(q_ref[...], k_ref[...], (((1,), (1,)), ((), ())), preferred_element_type=F32)
    p = jnp.exp2(s - jnp.max(s, axis=-1, keepdims=True))
    o = jnp.dot(p.astype(BF16), v_ref[...], preferred_element_type=F32)
    o_ref[...] = (o * (1.0 / jnp.sum(p, axis=-1, keepdims=True))).astype(o_ref.dtype)


def dense_attention(q, k, v_src, v_col0):
    m, bw = q.shape
    heads = bw // HEAD_DIM
    blk = lambda c0: pl.BlockSpec((m, HEAD_DIM), lambda h: (0, c0 + h))
    return pl.pallas_call(
        _dense_attn_kernel, grid=(heads,), in_specs=[blk(0), blk(0), blk(v_col0)], out_specs=blk(0),
        out_shape=jax.ShapeDtypeStruct((m, bw), BF16),
        compiler_params=_cparams("arbitrary"), name="dense_attention",
    )(q, k, v_src)


def _merge_up_kernel(od_ref, on_ref, cb_ref, cc_ref, cx_ref, ccp_ref, cxp_ref, ccn_ref, cxn_ref, cw_ref,
                     g0_ref, g1_ref, g2_ref, wd_ref, wn_ref, wc_ref, o_ref, oc_sc):
    i, j = pl.program_id(0), pl.program_id(1)

    @pl.when(j == 0)
    def _():
        u = cc_ref[...].astype(F32) * cx_ref[...].astype(F32)
        tm = u.shape[0]
        halo = ccp_ref.shape[0]
        prev_row = ccp_ref[halo - 1:halo, :].astype(F32) * cxp_ref[halo - 1:halo, :].astype(F32)
        next_row = ccn_ref[0:1, :].astype(F32) * cxn_ref[0:1, :].astype(F32)
        prev_row = jnp.where(i == 0, 0.0, prev_row)
        next_row = jnp.where(i == pl.num_programs(0) - 1, 0.0, next_row)
        r = lax.broadcasted_iota(jnp.int32, u.shape, 0)
        u_prev = jnp.where(r == 0, prev_row, pltpu.roll(u, 1, axis=0))
        u_next = jnp.where(r == tm - 1, next_row, pltpu.roll(u, tm - 1, axis=0))
        conv = cw_ref[0:1, :] * u_prev + cw_ref[1:2, :] * u + cw_ref[2:3, :] * u_next
        oc_sc[...] = (cb_ref[...].astype(F32) * conv).astype(BF16)

    acc = _sigmoid(g0_ref[...].astype(F32)) * jnp.dot(od_ref[...], wd_ref[...], preferred_element_type=F32)
    acc += _sigmoid(g1_ref[...].astype(F32)) * jnp.dot(on_ref[...], wn_ref[...], preferred_element_type=F32)
    acc += _sigmoid(g2_ref[...].astype(F32)) * jnp.dot(oc_sc[...], wc_ref[...], preferred_element_type=F32)
    o_ref[...] = acc.astype(o_ref.dtype)


def merge_up(o_diff, o_na, p, conv_w, w_up_diff, w_up_na, w_up_conv, layer, d):
    m, bw = o_diff.shape
    tm, tn = _tile(m, 512), math.gcd(_tile(d, 1024), bw)
    halo = 8
    n_halo = m // halo
    row_blk = lambda c: pl.BlockSpec((tm, bw), lambda i, j, c=c: (i, c))
    prev_blk = lambda c: pl.BlockSpec((halo, bw), lambda i, j, c=c: (jnp.maximum(i * (tm // halo) - 1, 0), c))
    next_blk = lambda c: pl.BlockSpec((halo, bw), lambda i, j, c=c: (jnp.minimum((i + 1) * (tm // halo), n_halo - 1), c))
    gate0 = 9 * bw // tn
    gate_blk = lambda b: pl.BlockSpec((tm, tn), lambda i, j, b=b: (i, gate0 + b * (d // tn) + j))
    w_blk = pl.BlockSpec((None, bw, tn), lambda i, j: (layer, 0, j))
    assert (9 * bw) % tn == 0
    return pl.pallas_call(
        _merge_up_kernel,
        grid=(m // tm, d // tn),
        in_specs=[row_blk(0), row_blk(0), row_blk(6), row_blk(7), row_blk(8),
                  prev_blk(7), prev_blk(8), next_blk(7), next_blk(8),
                  pl.BlockSpec((CONV_K, bw), lambda i, j: (0, 0)),
                  gate_blk(0), gate_blk(1), gate_blk(2), w_blk, w_blk, w_blk],
        out_specs=pl.BlockSpec((tm, tn), lambda i, j: (i, j)),
        out_shape=jax.ShapeDtypeStruct((m, d), BF16),
        scratch_shapes=[pltpu.VMEM((tm, bw), BF16)],
        compiler_params=_cparams("arbitrary", "arbitrary"),
        name="merge_up",
    )(o_diff, o_na, p, p, p, p, p, p, p, conv_w, p, p, p, w_up_diff, w_up_na, w_up_conv)


def kernel(x, c, ctx, c_ctx, w_mod_a, w_mod_b, b_mod, norm1_g, w_in, diff_qn_g, diff_kn_g, lambda_q1, lambda_k1, lambda_q2, lambda_k2, diff_subln_g, na_qn_g, na_kn_g, na_rpb, conv_w, w_up_diff, w_up_na, w_up_conv, w_o, norm2_g, router_w, router_b, w_gu, b_gu, w_down, b_down):
    batch, s, d = x.shape
    assert batch == 1 and ctx.shape[0] == 1
    depth = w_in.shape[0]
    bw = w_up_diff.shape[1]
    dv0 = 2 * bw // (2 * HEAD_DIM)
    nv0 = 5 * bw // HEAD_DIM
    rows = s // GRID_W

    cond = jnp.zeros((8, d), F32).at[0].set(c[0]).at[1].set(c_ctx)
    mod = modulation(cond, w_mod_a, w_mod_b, b_mod).reshape(depth, 8, N_MOD, d)
    rope = rope_tables(s)
    w_in, w_o = w_in.astype(BF16), w_o.astype(BF16)
    ups = tuple(w.astype(BF16) for w in (w_up_diff, w_up_na, w_up_conv))

    xl, xc = x[0], ctx[0]
    h_next = None
    for i in range(depth):
        last = i == depth - 1
        lam_init = 0.8 - 0.6 * math.exp(-0.3 * i)
        lam = (jnp.exp(jnp.sum(lambda_q1[i] * lambda_k1[i])) - jnp.exp(jnp.sum(lambda_q2[i] * lambda_k2[i]))
               + lam_init)
        ml, mc = mod[i, 0], mod[i, 1]
        gains = jnp.stack([diff_qn_g[i], diff_kn_g[i], na_qn_g[i], na_kn_g[i]])

        h_in = prenorm(xl, norm1_g[i], ml[1], ml[0]) if h_next is None else h_next
        p_l = project_in(h_in, w_in, i)
        p_c = project_in(prenorm(xc, norm1_g[i], mc[1], mc[0]), w_in, i)
        dq_l, dk_l, nq_l, nk_l = qk_prep(p_l, gains, rope, bw)
        dq_c, dk_c, nq_c, nk_c = qk_prep(p_c, gains, None, bw)

        diff_args = (dq_l, dk_l, p_l, dv0, lam, diff_subln_g[i], 1.0 - lam_init, (dk_c, p_c, dv0))
        o_diff_l = lax.cond(score_bound_log2(diff_qn_g[i], diff_kn_g[i]) <= SCORE_BOUND_LOG2,
                            lambda: diff_attention_bounded(*diff_args), lambda: diff_attention(*diff_args))
        na_args = (nq_l, nk_l, p_l, nv0, nk_c, p_c, nv0, na_bias_tables(na_rpb[i], rows))
        na_bound = score_bound_log2(na_qn_g[i], na_kn_g[i]) + jnp.max(jnp.abs(na_rpb[i])) * LOG2E
        o_na_l = lax.cond(na_bound <= SCORE_BOUND_LOG2, lambda: neighbourhood_attention(*na_args, True),
                          lambda: neighbourhood_attention(*na_args, False))
        xl = project_out(merge_up(o_diff_l, o_na_l, p_l, conv_w[i], *ups, i, d), w_o, i, xl, ml[2])
        if not last:
            o_diff_c = diff_attention(dq_c, dk_c, p_c, dv0, lam, diff_subln_g[i], 1.0 - lam_init)
            o_na_c = dense_attention(nq_c, nk_c, p_c, nv0)
            xc = project_out(merge_up(o_diff_c, o_na_c, p_c, conv_w[i], *ups, i, d), w_o, i, xc, mc[2])

        router = (router_w[i], router_b[i])
        h_l, eid, ew, rank, counts = prenorm_route(xl, norm2_g[i], ml[4], ml[3], router)
        y4 = moe_routed(h_l, eid, ew, rank, counts, w_gu, b_gu[i], w_down, b_down[i], i)
        next_norm = None if last else (norm1_g[i + 1], mod[i + 1, 0, 1], mod[i + 1, 0, 0])
        xl, h_next = moe_combine(xl, ml[5], ew, y4, next_norm)
        if not last:
            h_c, comb_c = prenorm(xc, norm2_g[i], mc[4], mc[3], router)
            xc = moe(h_c, comb_c, w_gu, b_gu[i], w_down, b_down[i], i, xc, mc[5])
    return xl[None]
```
